```python
import jax, jax.numpy as jnp
from jax import lax
import numpy as np

D_MODEL = 1024
BATCH = 8
SEQ = 2048
DEPTH = 4

GRID_W = 64
N_MEM = 256
HEAD_DIM = 64
N_MIX_HEADS = D_MODEL // HEAD_DIM
ML_HEADS = (3 * N_MIX_HEADS) // 8
NA_HEADS = (N_MIX_HEADS - ML_HEADS) // 2
GLA_HEADS = N_MIX_HEADS - ML_HEADS - NA_HEADS
ML_DIM = ML_HEADS * HEAD_DIM
NA_DIM = NA_HEADS * HEAD_DIM
GLA_DK = HEAD_DIM // 2
GLA_KDIM = GLA_HEADS * GLA_DK
GLA_VDIM = GLA_HEADS * HEAD_DIM
D_MIX = ML_DIM + NA_DIM + GLA_VDIM
GLA_RANK = 16
GLA_TAU = 16.0
NA_KH_MAX = 8
NA_KW = 16
ML_CHUNK = 128
GLA_CHUNK = 64
X_HEADS = 4
X_HEAD_DIM = D_MODEL // X_HEADS
D_FF = ((8 * D_MODEL // 3 + 127) // 128) * 128
EPS = 1e-6
IN_SIZES = (ML_DIM, ML_DIM, ML_DIM, ML_DIM, 4 * ML_HEADS,
            NA_DIM, NA_DIM, NA_DIM,
            GLA_KDIM, GLA_KDIM, GLA_VDIM, GLA_VDIM, 2 * GLA_RANK)
IN_COLS = sum(IN_SIZES)

kernel_name = 'hybrid_mlstm_natten_gla_encoder'


def rms_norm(x, g):
    xf = x.astype(jnp.float32)
    y = xf * lax.rsqrt(jnp.mean(xf * xf, axis=-1, keepdims=True) + EPS)
    return (y * g.astype(jnp.float32)).astype(x.dtype)


def head_rms_norm(h, g, n_heads):
    B, T, C = h.shape
    hh = h.reshape(B, T, n_heads, C // n_heads)
    hh = hh * lax.rsqrt(jnp.mean(hh * hh, axis=-1, keepdims=True) + EPS)
    return hh.reshape(B, T, C) * g.astype(jnp.float32)


def dwconv_centred(x, w, b):
    xp = jnp.pad(x, ((0, 0), (1, 1), (0, 0)))
    return xp[:, :-2] * w[0] + xp[:, 1:-1] * w[1] + xp[:, 2:] * w[2] + b


def to_chunks(a, L):
    B, H, T = a.shape[:3]
    return jnp.moveaxis(a.reshape((B, H, T // L, L) + a.shape[3:]), 2, 0)


def from_chunks(a):
    a = jnp.moveaxis(a, 0, 2)
    return a.reshape(a.shape[:2] + (-1,) + a.shape[4:])


def mlstm_chunkwise(q, k, v, i_pre, f_pre):
    B, H, T, d = q.shape
    L = ML_CHUNK
    logf = jax.nn.log_sigmoid(f_pre)
    xs = tuple(to_chunks(a, L) for a in (q, k, v, i_pre, logf))
    causal = jnp.tril(jnp.ones((L, L), dtype=bool))

    def step(carry, inp):
        C, n, m = carry
        qb, kb, vb, ib, fb = inp
        b = jnp.cumsum(fb, axis=-1)
        g = b[..., -1]
        D = b[..., :, None] - b[..., None, :] + ib[..., None, :]
        D = jnp.where(causal, D, -jnp.inf)
        inter = b + m[..., None]
        m_row = jnp.maximum(inter, jnp.max(D, axis=-1))
        w_inter = jnp.exp(inter - m_row)
        s = jnp.einsum('bhid,bhjd->bhij', qb, kb) * jnp.exp(D - m_row[..., None])
        num = jnp.einsum('bhij,bhje->bhie', s, vb) + w_inter[..., None] * jnp.einsum('bhid,bhde->bhie', qb, C)
        den = jnp.sum(s, axis=-1) + w_inter * jnp.einsum('bhid,bhd->bhi', qb, n)
        h = num / jnp.maximum(jnp.abs(den), jnp.exp(-m_row))[..., None]
        tail = g[..., None] - b + ib
        m_new = jnp.maximum(g + m, jnp.max(tail, axis=-1))
        wk = jnp.exp(tail - m_new[..., None])
        decay = jnp.exp(g + m - m_new)
        C_new = decay[..., None, None] * C + jnp.einsum('bhj,bhjd,bhje->bhde', wk, kb, vb)
        n_new = decay[..., None] * n + jnp.einsum('bhj,bhjd->bhd', wk, kb)
        return (C_new, n_new, m_new), h

    init = (jnp.zeros((B, H, d, d), jnp.float32), jnp.zeros((B, H, d), jnp.float32), jnp.zeros((B, H), jnp.float32))
    _, hs = lax.scan(step, init, xs)
    return from_chunks(hs)


def gla_chunked(q, k, v, log_a):
    B, H, T, dk = q.shape
    dv = v.shape[-1]
    L = GLA_CHUNK
    xs = tuple(to_chunks(a, L) for a in (q, k, v, log_a))
    causal = jnp.tril(jnp.ones((L, L), dtype=bool))

    def step(S, inp):
        qb, kb, vb, ab = inp
        Bc = jnp.cumsum(ab, axis=2)
        Btot = Bc[:, :, -1]
        q_dec = qb * jnp.exp(Bc)
        A = jnp.einsum('bhid,bhjd->bhij', q_dec, kb * jnp.exp(-Bc))
        A = jnp.where(causal, A, 0.0)
        o = jnp.einsum('bhij,bhje->bhie', A, vb) + jnp.einsum('bhid,bhde->bhie', q_dec, S)
        k_tail = kb * jnp.exp(Btot[:, :, None, :] - Bc)
        S_new = jnp.exp(Btot)[..., None] * S + jnp.einsum('bhjd,bhje->bhde', k_tail, vb)
        return S_new, o

    _, os_ = lax.scan(step, jnp.zeros((B, H, dk, dv), jnp.float32), xs)
    return from_chunks(os_)


def neighbourhood_attention(q, k, v, rpb):
    B, T, H, d = q.shape
    rows = T // GRID_W
    kh = min(NA_KH_MAX, rows)
    qg = q.reshape(B, rows, GRID_W, H, d)
    kg = k.reshape(B, rows, GRID_W, H, d)
    vg = v.reshape(B, rows, GRID_W, H, d)
    r = jnp.arange(rows)
    row_start = jnp.clip(r - kh // 2, 0, rows - kh)
    key_rows = row_start[:, None] + jnp.arange(kh)[None, :]
    k_band = jnp.take(kg, key_rows, axis=1)
    v_band = jnp.take(vg, key_rows, axis=1)
    c = jnp.arange(GRID_W)
    col_start = jnp.clip(c - NA_KW // 2, 0, GRID_W - NA_KW)
    in_win = (c[None, :] >= col_start[:, None]) & (c[None, :] < col_start[:, None] + NA_KW)
    dr = key_rows - r[:, None] + (NA_KH_MAX - 1)
    dc = jnp.clip(c[None, :] - c[:, None], -(NA_KW - 1), NA_KW - 1) + (NA_KW - 1)
    bias = rpb.astype(jnp.float32)[:, dr[:, None, :, None], dc[None, :, None, :]]
    s = jnp.einsum('brqhd,brkwhd->bhrqkw', qg, k_band).astype(jnp.float32) * (d ** -0.5) + bias
    s = jnp.where(in_win[:, None, :], s, -jnp.inf)
    p = jax.nn.softmax(s.reshape(B, H, rows, GRID_W, kh * GRID_W), axis=-1).reshape(s.shape)
    o = jnp.einsum('bhrqkw,brkwhd->brqhd', p.astype(v.dtype), v_band)
    return o.reshape(B, T, H * d)


def hybrid_mixer(h, w_in, ml_conv_w, ml_conv_b, ml_gate_b, ml_norm_g, na_rpb,
                 gla_a_w2, gla_a_b, gla_norm_g, w_out):
    B, T, _ = h.shape
    f32 = jnp.float32
    proj = h @ w_in
    pts = []
    acc = 0
    for s_ in IN_SIZES[:-1]:
        acc += s_
        pts.append(acc)
    (ml_q, ml_k, ml_v, ml_o, ml_g, na_q, na_k, na_v,
     g_q, g_k, g_v, g_g, g_a) = jnp.split(proj, pts, axis=-1)

    def heads(a, n):
        return a.reshape(B, T, n, -1).transpose(0, 2, 1, 3).astype(f32)

    def flip(a):
        return jnp.flip(a, axis=2)

    qk = jax.nn.silu(dwconv_centred(jnp.concatenate([ml_q, ml_k], axis=-1), ml_conv_w, ml_conv_b))
    mq = heads(qk[..., :ML_DIM], ML_HEADS)
    mk = heads(qk[..., ML_DIM:], ML_HEADS) * (HEAD_DIM ** -0.5)
    mv = heads(ml_v, ML_HEADS)
    gates = ml_g.reshape(B, T, 4, ML_HEADS).astype(f32) + ml_gate_b.astype(f32)
    gates = gates.transpose(2, 0, 3, 1)
    h_f = mlstm_chunkwise(mq, mk, mv, gates[0], gates[1])
    h_b = flip(mlstm_chunkwise(flip(mq), flip(mk), flip(mv), flip(gates[2]), flip(gates[3])))
    ml_h = (h_f + h_b).transpose(0, 2, 1, 3).reshape(B, T, ML_DIM)
    ml_out = head_rms_norm(ml_h, ml_norm_g, ML_HEADS) * jax.nn.sigmoid(ml_o.astype(f32))

    na_out = neighbourhood_attention(na_q.reshape(B, T, NA_HEADS, HEAD_DIM),
                                     na_k.reshape(B, T, NA_HEADS, HEAD_DIM),
                                     na_v.reshape(B, T, NA_HEADS, HEAD_DIM), na_rpb).astype(f32)

    gq = heads(g_q, GLA_HEADS) * (GLA_DK ** -0.5)
    gk = heads(g_k, GLA_HEADS)
    gv = heads(g_v, GLA_HEADS)
    z = jnp.einsum('btzr,zrk->zbtk', g_a.reshape(B, T, 2, GLA_RANK).astype(f32), gla_a_w2.astype(f32))
    z = z + gla_a_b.astype(f32)[:, None, None, :]
    log_a = jax.nn.log_sigmoid(z) / GLA_TAU
    log_a = log_a.reshape(2, B, T, GLA_HEADS, GLA_DK).transpose(0, 1, 3, 2, 4)
    o_f = gla_chunked(gq, gk, gv, log_a[0])
    o_b = flip(gla_chunked(flip(gq), flip(gk), flip(gv), flip(log_a[1])))
    gla_h = (o_f + o_b).transpose(0, 2, 1, 3).reshape(B, T, GLA_VDIM)
    gla_out = head_rms_norm(gla_h, gla_norm_g, GLA_HEADS) * jax.nn.silu(g_g.astype(f32))

    mixed = jnp.concatenate([ml_out, na_out, gla_out], axis=-1).astype(h.dtype)
    return mixed @ w_out


def memory_cross_attention(h, mem_n, wq, wk, wv, wo):
    B, T, _ = h.shape
    M = mem_n.shape[1]
    q = (h @ wq).reshape(B, T, X_HEADS, X_HEAD_DIM)
    k = (mem_n @ wk).reshape(B, M, X_HEADS, X_HEAD_DIM)
    v = (mem_n @ wv).reshape(B, M, X_HEADS, X_HEAD_DIM)
    s = jnp.einsum('bthd,bmhd->bhtm', q, k).astype(jnp.float32) * (X_HEAD_DIM ** -0.5)
    p = jax.nn.softmax(s, axis=-1).astype(v.dtype)
    o = jnp.einsum('bhtm,bmhd->bthd', p, v).reshape(B, T, D_MODEL)
    return o @ wo


def conv_ffn(h, w_up, conv_w, conv_b, w_down):
    u = h @ w_up
    a, g = u[..., :D_FF], u[..., D_FF:]
    a = dwconv_centred(a, conv_w, conv_b)
    return (jax.nn.gelu(a) * g) @ w_down


def setup_inputs(seed: int = 0) -> dict:
    key = jax.random.key(seed)
    ks = jax.random.split(key, 26)
    f32 = jnp.float32

    def nrm(k, shape, scale):
        return jax.random.normal(k, shape, f32) * scale

    def gain(k, shape):
        return 1.0 + nrm(k, shape, 0.02)

    f_bias = jnp.linspace(3.0, 6.0, ML_HEADS, dtype=f32)
    zeros_h = jnp.zeros((ML_HEADS,), f32)
    gate_base = jnp.stack([zeros_h, f_bias, zeros_h, f_bias])
    return {
        'x': nrm(ks[0], (BATCH, SEQ, D_MODEL), 1.0),
        'mem': nrm(ks[1], (BATCH, N_MEM, D_MODEL), 1.0),
        'mix_norm_g': gain(ks[2], (DEPTH, D_MODEL)),
        'w_in': nrm(ks[3], (DEPTH, D_MODEL, IN_COLS), D_MODEL ** -0.5),
        'ml_conv_w': nrm(ks[4], (DEPTH, 3, 2 * ML_DIM), 3 ** -0.5),
        'ml_conv_b': nrm(ks[5], (DEPTH, 2 * ML_DIM), 0.02),
        'ml_gate_b': gate_base[None] + nrm(ks[6], (DEPTH, 4, ML_HEADS), 0.1),
        'ml_norm_g': gain(ks[7], (DEPTH, ML_DIM)),
        'na_rpb': nrm(ks[8], (DEPTH, NA_HEADS, 2 * NA_KH_MAX - 1, 2 * NA_KW - 1), 0.1),
        'gla_a_w2': nrm(ks[9], (DEPTH, 2, GLA_RANK, GLA_KDIM), GLA_RANK ** -0.5),
        'gla_a_b': nrm(ks[10], (DEPTH, 2, GLA_KDIM), 0.1),
        'gla_norm_g': gain(ks[11], (DEPTH, GLA_VDIM)),
        'w_out': nrm(ks[12], (DEPTH, D_MIX, D_MODEL), D_MIX ** -0.5),
        'xattn_norm_g': gain(ks[13], (DEPTH, D_MODEL)),
        'mem_norm_g': gain(ks[14], (DEPTH, D_MODEL)),
        'w_xq': nrm(ks[15], (DEPTH, D_MODEL, D_MODEL), D_MODEL ** -0.5),
        'w_xk': nrm(ks[16], (DEPTH, D_MODEL, D_MODEL), D_MODEL ** -0.5),
        'w_xv': nrm(ks[17], (DEPTH, D_MODEL, D_MODEL), D_MODEL ** -0.5),
        'w_xo': nrm(ks[18], (DEPTH, D_MODEL, D_MODEL), D_MODEL ** -0.5),
        'ffn_norm_g': gain(ks[19], (DEPTH, D_MODEL)),
        'w_up': nrm(ks[20], (DEPTH, D_MODEL, 2 * D_FF), D_MODEL ** -0.5),
        'ffn_conv_w': nrm(ks[21], (DEPTH, 3, D_FF), 3 ** -0.5),
        'ffn_conv_b': nrm(ks[22], (DEPTH, D_FF), 0.02),
        'w_down': nrm(ks[23], (DEPTH, D_FF, D_MODEL), D_FF ** -0.5),
        'final_norm_g': gain(ks[24], (D_MODEL,)),
    }


def reference(x, mem, mix_norm_g, w_in, ml_conv_w, ml_conv_b, ml_gate_b, ml_norm_g, na_rpb,
              gla_a_w2, gla_a_b, gla_norm_g, w_out, xattn_norm_g, mem_norm_g,
              w_xq, w_xk, w_xv, w_xo, ffn_norm_g, w_up, ffn_conv_w, ffn_conv_b, w_down,
              final_norm_g):
    h = x
    for l in range(DEPTH):
        h = h + hybrid_mixer(rms_norm(h, mix_norm_g[l]), w_in[l], ml_conv_w[l], ml_conv_b[l],
                             ml_gate_b[l], ml_norm_g[l], na_rpb[l], gla_a_w2[l], gla_a_b[l],
                             gla_norm_g[l], w_out[l])
        h = h + memory_cross_attention(rms_norm(h, xattn_norm_g[l]), rms_norm(mem, mem_norm_g[l]),
                                       w_xq[l], w_xk[l], w_xv[l], w_xo[l])
        h = h + conv_ffn(rms_norm(h, ffn_norm_g[l]), w_up[l], ffn_conv_w[l], ffn_conv_b[l], w_down[l])
    return rms_norm(h, final_norm_g)
```

```python
import functools

import numpy as np
import jax
import jax.numpy as jnp
from jax import lax
from jax.experimental import pallas as pl
from jax.experimental.pallas import tpu as pltpu

F32 = jnp.float32
BF16 = jnp.bfloat16

HEAD_DIM = 64
ML_HEADS = 6
NA_HEADS = 5
GLA_HEADS = 5
ML_DIM = ML_HEADS * HEAD_DIM
NA_DIM = NA_HEADS * HEAD_DIM
GLA_DK = 32
GLA_KDIM = GLA_HEADS * GLA_DK
GLA_VDIM = GLA_HEADS * HEAD_DIM
GLA_RANK = 16
GLA_TAU = 16.0
GRID_W = 64
NA_KH = 8
NA_KW = 16
ML_CHUNK = 128
GLA_CHUNK = 64
X_HEADS = 4
EPS = 1e-6
IN_SIZES = (ML_DIM, ML_DIM, ML_DIM, ML_DIM, 4 * ML_HEADS,
            NA_DIM, NA_DIM, NA_DIM,
            GLA_KDIM, GLA_KDIM, GLA_VDIM, GLA_VDIM, 2 * GLA_RANK)

LANES = 128
SUBLANES = 8

PAIR_W = 2 * HEAD_DIM
ML_COLS = 4 * ML_DIM
NA_PAD = 3 * PAIR_W
NA_COLS = 3 * NA_PAD
GLA_K_PAD = 256
GLA_V_PAD = 3 * PAIR_W
GLA_COLS = 2 * GLA_K_PAD + 2 * GLA_V_PAD
GATE_COLS = LANES
GLA_GATE_OFF = 4 * SUBLANES
IN_COLS_PAD = ML_COLS + NA_COLS + GLA_COLS + GATE_COLS

ROW_TILE = 512
VMEM_LIMIT = 56 * 1024 * 1024

NT_DIMS = (((1,), (1,)), ((), ()))
TN_DIMS = (((0,), (0,)), ((), ()))


def _cparams(*sem):
    return pltpu.CompilerParams(dimension_semantics=sem, vmem_limit_bytes=VMEM_LIMIT)


def _const_spec(shape):
    nd = len(shape)
    return pl.BlockSpec(shape, lambda *_: (0,) * nd)


def _rms(x, g):
    ms = jnp.mean(x * x, axis=-1, keepdims=True)
    return x * lax.rsqrt(ms + EPS) * g


def _log_sigmoid(x):
    return jnp.minimum(x, 0.0) - jnp.log1p(jnp.exp(-jnp.abs(x)))


def _split_bf16(x):
    hi = x.astype(BF16)
    lo = (x - hi.astype(F32)).astype(BF16)
    return hi, lo


def _group_mean_sq(x, group):
    n = x.shape[-1]
    r = lax.broadcasted_iota(jnp.int32, (n, n), 0) // group
    c = lax.broadcasted_iota(jnp.int32, (n, n), 1) // group
    ones = jnp.where(r == c, 1.0, 0.0).astype(BF16)
    hi, lo = _split_bf16(x * x)
    s = jnp.dot(hi, ones, preferred_element_type=F32) + jnp.dot(lo, ones, preferred_element_type=F32)
    return s * (1.0 / group)


def _inproj_kernel(h_ref, g_ref, w_ref, ml_ref, na_ref, gla_ref, gate_ref):
    xn = _rms(h_ref[...], g_ref[...]).astype(BF16)
    off = 0
    for ref, chunk in ((ml_ref, 512), (na_ref, 384), (gla_ref, 256), (gate_ref, 128)):
        width = ref.shape[-1]
        for c in range(0, width, chunk):
            acc = jnp.dot(xn, w_ref[:, off + c:off + c + chunk], preferred_element_type=F32)
            ref[:, c:c + chunk] = acc.astype(ref.dtype)
        off += width


def _inproj(h, g, w):
    n, d = h.shape
    tm = ROW_TILE
    row = lambda cols: pl.BlockSpec((tm, cols), lambda i: (i, 0))
    return pl.pallas_call(
        _inproj_kernel,
        out_shape=(jax.ShapeDtypeStruct((n, ML_COLS), BF16),
                   jax.ShapeDtypeStruct((n, NA_COLS), BF16),
                   jax.ShapeDtypeStruct((n, GLA_COLS), BF16),
                   jax.ShapeDtypeStruct((n, GATE_COLS), F32)),
        grid=(n // tm,),
        in_specs=[row(d), _const_spec((1, d)), _const_spec((d, IN_COLS_PAD))],
        out_specs=(row(ML_COLS), row(NA_COLS), row(GLA_COLS), row(GATE_COLS)),
        compiler_params=_cparams("parallel"),
        name="inproj",
    )(h, g, w)


def _lane_cumsum(x, reverse):
    n = x.shape[-1]
    lane = lax.broadcasted_iota(jnp.int32, x.shape, 1)
    s = 1
    while s < n:
        if reverse:
            x = x + jnp.where(lane < n - s, pltpu.roll(x, n - s, 1), 0.0)
        else:
            x = x + jnp.where(lane >= s, pltpu.roll(x, s, 1), 0.0)
        s *= 2
    return x


def _mlstm_kernel(x_ref, gates_ref, cw_ref, cb_ref, gb_ref, ng_ref, out_ref,
                  xpad_ref, qk_ref, hsum_ref, c_ref):
    T = x_ref.shape[0]
    L = ML_CHUNK
    NC = T // L
    QK = 2 * ML_DIM
    RB = 256

    xpad_ref[0:SUBLANES, :] = jnp.zeros((SUBLANES, QK), F32)
    xpad_ref[T + SUBLANES:T + 2 * SUBLANES, :] = jnp.zeros((SUBLANES, QK), F32)
    for r0 in range(0, T, RB):
        xpad_ref[SUBLANES + r0:SUBLANES + r0 + RB, :] = x_ref[r0:r0 + RB, 0:QK].astype(F32)
    col = lax.broadcasted_iota(jnp.int32, (1, QK), 1)
    kscale = jnp.where(col >= ML_DIM, HEAD_DIM ** -0.5, 1.0)
    for r0 in range(0, T, RB):
        xm = xpad_ref[SUBLANES - 1 + r0:SUBLANES - 1 + r0 + RB, :]
        x0 = xpad_ref[SUBLANES + r0:SUBLANES + r0 + RB, :]
        xp = xpad_ref[SUBLANES + 1 + r0:SUBLANES + 1 + r0 + RB, :]
        y = xm * cw_ref[0:1, :] + x0 * cw_ref[1:2, :] + xp * cw_ref[2:3, :] + cb_ref[...]
        y = y * jax.nn.sigmoid(y) * kscale
        qk_ref[r0:r0 + RB, :] = y.astype(BF16)

    hsum_ref[...] = jnp.zeros(hsum_ref.shape, F32)
    c_ref[...] = jnp.zeros(c_ref.shape, F32)

    lane = lax.broadcasted_iota(jnp.int32, (1, PAIR_W), 1)
    head_mask = (lane < HEAD_DIM, lane >= HEAD_DIM)
    ri = lax.broadcasted_iota(jnp.int32, (L, L), 0)
    ci = lax.broadcasted_iota(jnp.int32, (L, L), 1)
    tri = (ci <= ri, ci >= ri)

    def body(it, carry):
        m_reps = list(carry)
        chunk = (it, NC - 1 - it)
        rows, arows = [], []
        for d in range(2):
            r0 = pl.multiple_of(chunk[d] * L, L)
            gt = (gates_ref[pl.ds(r0, L), :] + gb_ref[...]).T
            ig = gt[2 * SUBLANES * d:2 * SUBLANES * d + SUBLANES, :]
            logf = _log_sigmoid(gt[2 * SUBLANES * d + SUBLANES:2 * SUBLANES * (d + 1), :])
            b = _lane_cumsum(logf, reverse=(d == 1))
            a = ig - b
            g = jnp.sum(logf, axis=1, keepdims=True)
            amax = jnp.max(a, axis=1, keepdims=True)
            m_old = m_reps[d]
            m_new = jnp.maximum(g + m_old, g + amax)
            inter = b + m_old
            wk = jnp.exp(g + a - m_new)
            decay = jnp.exp(g + m_old - m_new)
            m_reps[d] = m_new
            rows += [b, inter, wk]
            arows.append((a, decay))
        pad = jnp.zeros((L - len(rows) * SUBLANES, L), F32)
        cols = jnp.concatenate(rows + [pad], axis=0).T
        for d in range(2):
            r0 = pl.multiple_of(chunk[d] * L, L)
            a, decay = arows[d]
            for p in range(ML_HEADS // 2):
                q_p = qk_ref[pl.ds(r0, L), p * PAIR_W:(p + 1) * PAIR_W]
                k_p = qk_ref[pl.ds(r0, L), ML_DIM + p * PAIR_W:ML_DIM + (p + 1) * PAIR_W]
                v_p = x_ref[pl.ds(r0, L), 2 * ML_DIM + p * PAIR_W:2 * ML_DIM + (p + 1) * PAIR_W]
                contrib = None
                for half in range(2):
                    h = 2 * p + half
                    hm = head_mask[half]
                    cb0 = 3 * SUBLANES * d
                    b_col = cols[:, cb0 + h:cb0 + h + 1]
                    inter_col = cols[:, cb0 + SUBLANES + h:cb0 + SUBLANES + h + 1]
                    wk_col = cols[:, cb0 + 2 * SUBLANES + h:cb0 + 2 * SUBLANES + h + 1]
                    q_m = jnp.where(hm, q_p, jnp.zeros_like(q_p))
                    v_aug = jnp.where(hm, v_p, jnp.ones_like(v_p))
                    s_raw = lax.dot_general(q_m, k_p, NT_DIMS, preferred_element_type=F32)
                    dmat = jnp.where(tri[d], b_col + a[h:h + 1, :], -jnp.inf)
                    m_row = jnp.maximum(inter_col, jnp.max(dmat, axis=1, keepdims=True))
                    s = (s_raw * jnp.exp(dmat - m_row)).astype(BF16)
                    c_old = c_ref[d * ML_HEADS + h]
                    num = (jnp.dot(s, v_aug, preferred_element_type=F32)
                           + jnp.exp(inter_col - m_row)
                           * jnp.dot(q_m, c_old.astype(BF16), preferred_element_type=F32))
                    den = pltpu.roll(num, HEAD_DIM, 1)
                    hout = num / jnp.maximum(jnp.abs(den), jnp.exp(-m_row))
                    hout = jnp.where(hm, hout, 0.0)
                    contrib = hout if contrib is None else contrib + hout
                    k_w = jnp.where(hm, k_p.astype(F32) * wk_col, 0.0).astype(BF16)
                    kv = lax.dot_general(k_w, v_aug, TN_DIMS, preferred_element_type=F32)
                    c_ref[d * ML_HEADS + h] = decay[h:h + 1, :] * c_old + kv
                hsum_ref[pl.ds(r0, L), p * PAIR_W:(p + 1) * PAIR_W] += contrib
        return tuple(m_reps)

    m0 = jnp.zeros((SUBLANES, L), F32)
    lax.fori_loop(0, NC, body, (m0, m0))

    for r0 in range(0, T, RB):
        hs = hsum_ref[r0:r0 + RB, :]
        y = hs * lax.rsqrt(_group_mean_sq(hs, HEAD_DIM) + EPS) * ng_ref[...]
        o = x_ref[r0:r0 + RB, 3 * ML_DIM:4 * ML_DIM].astype(F32)
        out_ref[r0:r0 + RB, :] = (y * jax.nn.sigmoid(o)).astype(BF16)


def _mlstm(ml, gates, cw, cb, gb, ng, batch):
    n = ml.shape[0]
    T = n // batch
    return pl.pallas_call(
        _mlstm_kernel,
        out_shape=jax.ShapeDtypeStruct((n, ML_DIM), BF16),
        grid=(batch,),
        in_specs=[pl.BlockSpec((T, ML_COLS), lambda b: (b, 0)),
                  pl.BlockSpec((T, GATE_COLS), lambda b: (b, 0)),
                  _const_spec((3, 2 * ML_DIM)), _const_spec((1, 2 * ML_DIM)),
                  _const_spec((1, GATE_COLS)), _const_spec((1, ML_DIM))],
        out_specs=pl.BlockSpec((T, ML_DIM), lambda b: (b, 0)),
        scratch_shapes=[pltpu.VMEM((T + 2 * SUBLANES, 2 * ML_DIM), F32),
                        pltpu.VMEM((T, 2 * ML_DIM), BF16),
                        pltpu.VMEM((T, ML_DIM), F32),
                        pltpu.VMEM((2 * ML_HEADS, PAIR_W, PAIR_W), F32)],
        compiler_params=_cparams("parallel"),
        name="mlstm",
    )(ml, gates, cw, cb, gb, ng)


def _gla_kernel(x_ref, gates_ref, w2_ref, ab_ref, ng_ref, out_ref, la_ref, osum_ref, st_ref):
    T = x_ref.shape[0]
    L = GLA_CHUNK
    NC = T // L
    KP, VP = GLA_K_PAD, GLA_V_PAD
    RB = 256

    for r0 in range(0, T, RB):
        z = jnp.dot(gates_ref[r0:r0 + RB, :].astype(BF16), w2_ref[...],
                    preferred_element_type=F32) + ab_ref[...]
        la_ref[r0:r0 + RB, :] = _log_sigmoid(z) * (1.0 / GLA_TAU)

    osum_ref[...] = jnp.zeros(osum_ref.shape, F32)
    st_ref[...] = jnp.zeros(st_ref.shape, F32)

    ri = lax.broadcasted_iota(jnp.int32, (L, L), 0)
    ci = lax.broadcasted_iota(jnp.int32, (L, L), 1)
    tri = (ci <= ri, ci >= ri)
    tri_b = tuple(jnp.where(t, 1.0, 0.0).astype(BF16) for t in tri)
    qhead = lax.broadcasted_iota(jnp.int32, (1, KP), 1) // GLA_DK
    vhead = lax.broadcasted_iota(jnp.int32, (1, VP), 1) // HEAD_DIM
    st_mask = (lax.broadcasted_iota(jnp.int32, (VP, KP), 0) // HEAD_DIM
               == lax.broadcasted_iota(jnp.int32, (VP, KP), 1) // GLA_DK)

    def body(it, carry):
        chunk = (it, NC - 1 - it)
        for d in range(2):
            r0 = pl.multiple_of(chunk[d] * L, L)
            la = la_ref[pl.ds(r0, L), d * KP:(d + 1) * KP]
            hi, lo = _split_bf16(la)
            bc = (jnp.dot(tri_b[d], hi, preferred_element_type=F32)
                  + jnp.dot(tri_b[d], lo, preferred_element_type=F32))
            btot = bc[L - 1:L, :] if d == 0 else bc[0:1, :]
            q = x_ref[pl.ds(r0, L), 0:KP].astype(F32) * (GLA_DK ** -0.5)
            k = x_ref[pl.ds(r0, L), KP:2 * KP].astype(F32)
            v = x_ref[pl.ds(r0, L), 2 * KP:2 * KP + VP]
            q_dec = q * jnp.exp(bc)
            k_dec = (k * jnp.exp(-bc)).astype(BF16)
            st_old = st_ref[d]
            o = lax.dot_general(q_dec.astype(BF16), st_old.astype(BF16), NT_DIMS,
                                preferred_element_type=F32)
            for h in range(GLA_HEADS):
                q_h = jnp.where(qhead == h, q_dec, 0.0).astype(BF16)
                a_h = lax.dot_general(q_h, k_dec, NT_DIMS, preferred_element_type=F32)
                a_h = jnp.where(tri[d], a_h, 0.0).astype(BF16)
                o_h = jnp.dot(a_h, v, preferred_element_type=F32)
                o = o + jnp.where(vhead == h, o_h, 0.0)
            k_tail = (k * jnp.exp(btot - bc)).astype(BF16)
            kv = lax.dot_general(v, k_tail, TN_DIMS, preferred_element_type=F32)
            st_ref[d] = st_old * jnp.exp(btot) + jnp.where(st_mask, kv, 0.0)
            osum_ref[pl.ds(r0, L), :] += o
        return carry

    lax.fori_loop(0, NC, body, 0)

    for r0 in range(0, T, RB):
        hs = osum_ref[r0:r0 + RB, :]
        y = hs * lax.rsqrt(_group_mean_sq(hs, HEAD_DIM) + EPS) * ng_ref[...]
        g = x_ref[r0:r0 + RB, 2 * KP + VP:2 * KP + 2 * VP].astype(F32)
        out_ref[r0:r0 + RB, :] = (y * (g * jax.nn.sigmoid(g))).astype(BF16)


def _gla(x, gates, w2, ab, ng, batch):
    n = x.shape[0]
    T = n // batch
    return pl.pallas_call(
        _gla_kernel,
        out_shape=jax.ShapeDtypeStruct((n, GLA_V_PAD), BF16),
        grid=(batch,),
        in_specs=[pl.BlockSpec((T, GLA_COLS), lambda b: (b, 0)),
                  pl.BlockSpec((T, GATE_COLS), lambda b: (b, 0)),
                  _const_spec((GATE_COLS, 2 * GLA_K_PAD)), _const_spec((1, 2 * GLA_K_PAD)),
                  _const_spec((1, GLA_V_PAD))],
        out_specs=pl.BlockSpec((T, GLA_V_PAD), lambda b: (b, 0)),
        scratch_shapes=[pltpu.VMEM((T, 2 * GLA_K_PAD), F32),
                        pltpu.VMEM((T, GLA_V_PAD), F32),
                        pltpu.VMEM((2, GLA_V_PAD, GLA_K_PAD), F32)],
        compiler_params=_cparams("parallel"),
        name="gla",
    )(x, gates, w2, ab, ng)


NA_CLASSES = 8


def _na_kernel(q_ref, k_ref, v_ref, bias_ref, out_ref, *, n_heads_in_pair):
    T = q_ref.shape[0]
    W = GRID_W
    rows = T // W
    band = NA_KH * W
    lane = lax.broadcasted_iota(jnp.int32, (1, PAIR_W), 1)
    head_mask = (lane < HEAD_DIM, lane >= HEAD_DIM)
    scale = HEAD_DIM ** -0.5

    def body(r, carry):
        rs = jnp.clip(r - NA_KH // 2, 0, rows - NA_KH)
        cls = jnp.where(r < NA_KH // 2, r, jnp.where(r > rows - NA_KH // 2, r - (rows - NA_KH), NA_KH // 2))
        q0 = pl.multiple_of(r * W, W)
        k0 = pl.multiple_of(rs * W, W)
        q = q_ref[pl.ds(q0, W), :]
        kb = k_ref[pl.ds(k0, band), :]
        vb = v_ref[pl.ds(k0, band), :]
        outs = []
        for half in range(n_heads_in_pair):
            q_m = jnp.where(head_mask[half], q, jnp.zeros_like(q))
            s = lax.dot_general(q_m, kb, NT_DIMS, preferred_element_type=F32) * scale
            s = s + bias_ref[half, cls]
            m = jnp.max(s, axis=1, keepdims=True)
            p = jnp.exp(s - m)
            l = jnp.sum(p, axis=1, keepdims=True)
            outs.append(jnp.dot(p.astype(BF16), vb, preferred_element_type=F32) / l)
        second = outs[1] if n_heads_in_pair == 2 else 0.0
        out_ref[pl.ds(q0, W), :] = jnp.where(head_mask[0], outs[0], second).astype(BF16)
        return carry

    lax.fori_loop(0, rows, body, 0)


def _na(na, bias, batch):
    n = na.shape[0]
    T = n // batch
    npairs = NA_PAD // PAIR_W
    outs = []
    for p0, p1, nh in ((0, NA_HEADS // 2, 2), (NA_HEADS // 2, npairs, 1)):
        npp = p1 - p0
        outs.append(pl.pallas_call(
            functools.partial(_na_kernel, n_heads_in_pair=nh),
            out_shape=jax.ShapeDtypeStruct((n, npp * PAIR_W), BF16),
            grid=(npp, batch),
            in_specs=[pl.BlockSpec((T, PAIR_W), lambda p, b, o=p0: (b, o + p)),
                      pl.BlockSpec((T, PAIR_W), lambda p, b, o=npairs + p0: (b, o + p)),
                      pl.BlockSpec((T, PAIR_W), lambda p, b, o=2 * npairs + p0: (b, o + p)),
                      pl.BlockSpec((2, NA_CLASSES, GRID_W, NA_KH * GRID_W),
                                   lambda p, b, o=p0: (o + p, 0, 0, 0))],
            out_specs=pl.BlockSpec((T, PAIR_W), lambda p, b: (b, p)),
            compiler_params=_cparams("parallel", "parallel"),
            name=f"natten{nh}",
        )(na, na, na, bias))
    return outs


def _na_bias_table(rpb, rows):
    kh = NA_KH
    rep_rows = np.array(list(range(kh // 2)) + [kh // 2] + list(range(rows - kh // 2 + 1, rows)))
    row_start = np.clip(rep_rows - kh // 2, 0, rows - kh)
    dr = row_start[:, None] + np.arange(kh)[None, :] - rep_rows[:, None] + (kh - 1)
    c = np.arange(GRID_W)
    col_start = np.clip(c - NA_KW // 2, 0, GRID_W - NA_KW)
    in_win = (c[None, :] >= col_start[:, None]) & (c[None, :] < col_start[:, None] + NA_KW)
    dc = np.clip(c[None, :] - c[:, None], -(NA_KW - 1), NA_KW - 1) + (NA_KW - 1)
    tab = rpb.astype(F32)[:, dr[:, None, :, None], dc[None, :, None, :]]
    tab = jnp.where(in_win[None, None, :, None, :], tab, -jnp.inf)
    tab = tab.reshape(rpb.shape[0], len(rep_rows), GRID_W, kh * GRID_W)
    return jnp.pad(tab, ((0, 2 * (NA_PAD // PAIR_W) - rpb.shape[0]), (0, 0), (0, 0), (0, 0)))


def _outproj_kernel(ml_ref, na0_ref, na1_ref, gla_ref, w_ref, h_ref, o_ref):
    acc = h_ref[...]
    off = 0
    for ref in (ml_ref, na0_ref, na1_ref, gla_ref):
        width = ref.shape[-1]
        acc = acc + jnp.dot(ref[...], w_ref[off:off + width, :], preferred_element_type=F32)
        off += width
    o_ref[...] = acc


def _outproj(ml, na0, na1, gla, w, h):
    n, d = h.shape
    tm = ROW_TILE
    row = lambda cols: pl.BlockSpec((tm, cols), lambda i: (i, 0))
    return pl.pallas_call(
        _outproj_kernel,
        out_shape=jax.ShapeDtypeStruct((n, d), F32),
        grid=(n // tm,),
        in_specs=[row(ml.shape[1]), row(na0.shape[1]), row(na1.shape[1]), row(gla.shape[1]),
                  _const_spec(w.shape), row(d)],
        out_specs=row(d),
        compiler_params=_cparams("parallel"),
        name="outproj",
    )(ml, na0, na1, gla, w, h)


def _memkv_kernel(m_ref, g_ref, wk_ref, wv_ref, k_ref, v_ref):
    mn = _rms(m_ref[...], g_ref[...]).astype(BF16)
    k_ref[...] = jnp.dot(mn, wk_ref[...], preferred_element_type=F32).astype(BF16)
    v_ref[...] = jnp.dot(mn, wv_ref[...], preferred_element_type=F32).astype(BF16)


def _memkv(mem, g, wk, wv):
    n, d = mem.shape
    tm = min(ROW_TILE, n)
    row = pl.BlockSpec((tm, d), lambda i: (i, 0))
    return pl.pallas_call(
        _memkv_kernel,
        out_shape=(jax.ShapeDtypeStruct((n, d), BF16), jax.ShapeDtypeStruct((n, d), BF16)),
        grid=(n // tm,),
        in_specs=[row, _const_spec((1, d)), _const_spec((d, d)), _const_spec((d, d))],
        out_specs=(row, row),
        compiler_params=_cparams("parallel"),
        name="memkv",
    )(mem, g, wk, wv)


def _xattn_kernel(h_ref, g_ref, wq_ref, k_ref, v_ref, wo_ref, o_ref):
    h = h_ref[...]
    d = h.shape[-1]
    hd = d // X_HEADS
    hn = _rms(h, g_ref[...]).astype(BF16)
    q = jnp.dot(hn, wq_ref[...], preferred_element_type=F32).astype(BF16)
    acc = h
    for i in range(X_HEADS):
        sl = slice(i * hd, (i + 1) * hd)
        s = lax.dot_general(q[:, sl], k_ref[:, sl], NT_DIMS, preferred_element_type=F32) * (hd ** -0.5)
        m = jnp.max(s, axis=1, keepdims=True)
        p = jnp.exp(s - m)
        l = jnp.sum(p, axis=1, keepdims=True)
        o = (jnp.dot(p.astype(BF16), v_ref[:, sl], preferred_element_type=F32) / l).astype(BF16)
        acc = acc + jnp.dot(o, wo_ref[sl, :], preferred_element_type=F32)
    o_ref[...] = acc


def _xattn(h, g, wq, k, v, wo, batch):
    n, d = h.shape
    tm = ROW_TILE
    tiles_per_seq = (n // batch) // tm
    n_mem = k.shape[0] // batch
    row = pl.BlockSpec((tm, d), lambda i: (i, 0))
    kv = pl.BlockSpec((n_mem, d), lambda i: (i // tiles_per_seq, 0))
    return pl.pallas_call(
        _xattn_kernel,
        out_shape=jax.ShapeDtypeStruct((n, d), F32),
        grid=(n // tm,),
        in_specs=[row, _const_spec((1, d)), _const_spec((d, d)), kv, kv, _const_spec((d, d))],
        out_specs=row,
        compiler_params=_cparams("parallel"),
        name="xattn",
    )(h, g, wq, k, v, wo)


FFN_CHUNK = 256


def _gelu_tanh(x):
    return 0.5 * x * (1.0 + jnp.tanh(np.sqrt(2.0 / np.pi) * (x + 0.044715 * (x * x * x))))


def _ffn_kernel(h_ref, hp_ref, hx_ref, g_ref, wup_ref, cw_ref, cb_ref, wdn_ref, o_ref, *, tiles_per_seq):
    i = pl.program_id(0)
    h = h_ref[...]
    tm = h.shape[0]
    dff = wdn_ref.shape[0]
    hn = _rms(h, g_ref[...]).astype(BF16)
    halo = jnp.concatenate([hp_ref[...], hx_ref[...]], axis=0)
    halo_n = _rms(halo, g_ref[...]).astype(BF16)
    has_prev = (i % tiles_per_seq != 0).astype(F32)
    has_next = (i % tiles_per_seq != tiles_per_seq - 1).astype(F32)
    row = lax.broadcasted_iota(jnp.int32, (tm, 1), 0)
    acc = h
    for c in range(0, dff, FFN_CHUNK):
        wa = wup_ref[:, c:c + FFN_CHUNK]
        a = jnp.dot(hn, wa, preferred_element_type=F32)
        ah = jnp.dot(halo_n, wa, preferred_element_type=F32)
        gate = jnp.dot(hn, wup_ref[:, dff + c:dff + c + FFN_CHUNK], preferred_element_type=F32)
        a_prev = jnp.where(row == 0, ah[SUBLANES - 1:SUBLANES, :] * has_prev, pltpu.roll(a, 1, 0))
        a_next = jnp.where(row == tm - 1, ah[SUBLANES:SUBLANES + 1, :] * has_next, pltpu.roll(a, tm - 1, 0))
        cw = cw_ref[:, c:c + FFN_CHUNK]
        ac = a_prev * cw[0:1, :] + a * cw[1:2, :] + a_next * cw[2:3, :] + cb_ref[:, c:c + FFN_CHUNK]
        act = (_gelu_tanh(ac) * gate).astype(BF16)
        acc = acc + jnp.dot(act, wdn_ref[c:c + FFN_CHUNK, :], preferred_element_type=F32)
    o_ref[...] = acc


def _ffn(h, g, wup, cw, cb, wdn, batch):
    n, d = h.shape
    tm = ROW_TILE
    tiles_per_seq = (n // batch) // tm
    hb = tm // SUBLANES
    last = n // SUBLANES - 1
    row = pl.BlockSpec((tm, d), lambda i: (i, 0))
    return pl.pallas_call(
        functools.partial(_ffn_kernel, tiles_per_seq=tiles_per_seq),
        out_shape=jax.ShapeDtypeStruct((n, d), F32),
        grid=(n // tm,),
        in_specs=[row,
                  pl.BlockSpec((SUBLANES, d), lambda i: (jnp.maximum(i * hb - 1, 0), 0)),
                  pl.BlockSpec((SUBLANES, d), lambda i: (jnp.minimum((i + 1) * hb, last), 0)),
                  _const_spec((1, d)), _const_spec(wup.shape), _const_spec(cw.shape),
                  _const_spec(cb.shape), _const_spec(wdn.shape)],
        out_specs=row,
        compiler_params=_cparams("parallel"),
        name="convffn",
    )(h, h, h, g, wup, cw, cb, wdn)


def _norm_kernel(h_ref, g_ref, o_ref):
    o_ref[...] = _rms(h_ref[...], g_ref[...])


def _final_norm(h, g):
    n, d = h.shape
    tm = ROW_TILE
    row = pl.BlockSpec((tm, d), lambda i: (i, 0))
    return pl.pallas_call(
        _norm_kernel,
        out_shape=jax.ShapeDtypeStruct((n, d), F32),
        grid=(n // tm,),
        in_specs=[row, _const_spec((1, d))],
        out_specs=row,
        compiler_params=_cparams("parallel"),
        name="final_norm",
    )(h, g)


def _pad_cols(a, n):
    return jnp.pad(a, ((0, 0), (0, n - a.shape[1])))


def _pad_rows(a, n):
    return jnp.pad(a, ((0, n - a.shape[0]), (0, 0)))


def _prep_w_in(w):
    offs = np.concatenate([[0], np.cumsum(IN_SIZES)])
    seg = [w[:, offs[i]:offs[i + 1]] for i in range(len(IN_SIZES))]
    ml = seg[0:4]
    na = [_pad_cols(s, NA_PAD) for s in seg[5:8]]
    gla = [_pad_cols(seg[8], GLA_K_PAD), _pad_cols(seg[9], GLA_K_PAD),
           _pad_cols(seg[10], GLA_V_PAD), _pad_cols(seg[11], GLA_V_PAD)]
    gates = [_pad_cols(seg[4][:, j * ML_HEADS:(j + 1) * ML_HEADS], SUBLANES) for j in range(4)]
    gates = _pad_cols(jnp.concatenate(gates + [seg[12]], axis=1), GATE_COLS)
    return jnp.concatenate(ml + na + gla + [gates], axis=1).astype(BF16)


def _prep_gate_bias(gb):
    return _pad_cols(jnp.concatenate([_pad_cols(gb[j:j + 1], SUBLANES) for j in range(4)], axis=1), GATE_COLS)


def _prep_gla_gate(w2, ab):
    w = jnp.zeros((GATE_COLS, 2 * GLA_K_PAD), F32)
    b = jnp.zeros((1, 2 * GLA_K_PAD), F32)
    for z in range(2):
        r0 = GLA_GATE_OFF + z * GLA_RANK
        w = w.at[r0:r0 + GLA_RANK, z * GLA_K_PAD:z * GLA_K_PAD + GLA_KDIM].set(w2[z])
        b = b.at[0, z * GLA_K_PAD:z * GLA_K_PAD + GLA_KDIM].set(ab[z])
    return w.astype(BF16), b


def _prep_w_out(w):
    parts = [w[0:ML_DIM], _pad_rows(w[ML_DIM:ML_DIM + NA_DIM], NA_PAD),
             _pad_rows(w[ML_DIM + NA_DIM:], GLA_V_PAD)]
    return jnp.concatenate(parts, axis=0).astype(BF16)


def kernel(x, mem, mix_norm_g, w_in, ml_conv_w, ml_conv_b, ml_gate_b, ml_norm_g, na_rpb,
           gla_a_w2, gla_a_b, gla_norm_g, w_out, xattn_norm_g, mem_norm_g,
           w_xq, w_xk, w_xv, w_xo, ffn_norm_g, w_up, ffn_conv_w, ffn_conv_b, w_down,
           final_norm_g):
    batch, seq, d = x.shape
    depth = w_in.shape[0]
    rows = seq // GRID_W
    assert seq % ROW_TILE == 0 and seq % ML_CHUNK == 0 and rows >= NA_KH
    h = x.reshape(batch * seq, d)
    mem2 = mem.reshape(batch * mem.shape[1], d)
    r2 = lambda v: v.reshape(1, -1)
    for l in range(depth):
        ml, na, gla, gates = _inproj(h, r2(mix_norm_g[l]), _prep_w_in(w_in[l]))
        ml_out = _mlstm(ml, gates, ml_conv_w[l], r2(ml_conv_b[l]), _prep_gate_bias(ml_gate_b[l]),
                        r2(ml_norm_g[l]), batch)
        na0, na1 = _na(na, _na_bias_table(na_rpb[l], rows), batch)
        w2, ab = _prep_gla_gate(gla_a_w2[l], gla_a_b[l])
        gla_out = _gla(gla, gates, w2, ab, _pad_cols(r2(gla_norm_g[l]), GLA_V_PAD), batch)
        h = _outproj(ml_out, na0, na1, gla_out, _prep_w_out(w_out[l]), h)
        k, v = _memkv(mem2, r2(mem_norm_g[l]), w_xk[l].astype(BF16), w_xv[l].astype(BF16))
        h = _xattn(h, r2(xattn_norm_g[l]), w_xq[l].astype(BF16), k, v, w_xo[l].astype(BF16), batch)
        h = _ffn(h, r2(ffn_norm_g[l]), w_up[l].astype(BF16), ffn_conv_w[l], r2(ffn_conv_b[l]),
                 w_down[l].astype(BF16), batch)
    return _final_norm(h, final_norm_g.reshape(1, d)).reshape(batch, seq, d)
```

```python
import functools

import numpy as np
import jax
import jax.numpy as jnp
from jax import lax
from jax.experimental import pallas as pl
from jax.experimental.pallas import tpu as pltpu

F32 = jnp.float32
BF16 = jnp.bfloat16

HEAD_DIM = 64
ML_HEADS = 6
NA_HEADS = 5
GLA_HEADS = 5
ML_DIM = ML_HEADS * HEAD_DIM
NA_DIM = NA_HEADS * HEAD_DIM
GLA_DK = 32
GLA_KDIM = GLA_HEADS * GLA_DK
GLA_VDIM = GLA_HEADS * HEAD_DIM
GLA_RANK = 16
GLA_TAU = 16.0
GRID_W = 64
NA_KH = 8
NA_KW = 16
ML_CHUNK = 128
GLA_CHUNK = 64
X_HEADS = 4
EPS = 1e-6
IN_SIZES = (ML_DIM, ML_DIM, ML_DIM, ML_DIM, 4 * ML_HEADS,
            NA_DIM, NA_DIM, NA_DIM,
            GLA_KDIM, GLA_KDIM, GLA_VDIM, GLA_VDIM, 2 * GLA_RANK)

LANES = 128
SUBLANES = 8

PAIR_W = 2 * HEAD_DIM
ML_COLS = 4 * ML_DIM
NA_PAD = 3 * PAIR_W
NA_COLS = 3 * NA_PAD
GLA_K_PAD = 256
GLA_V_PAD = 3 * PAIR_W
GLA_COLS = 2 * GLA_K_PAD + 2 * GLA_V_PAD
GATE_COLS = LANES
GLA_GATE_OFF = 4 * SUBLANES
IN_COLS_PAD = ML_COLS + NA_COLS + GLA_COLS + GATE_COLS

ROW_TILE = 512
VMEM_LIMIT = 56 * 1024 * 1024

NT_DIMS = (((1,), (1,)), ((), ()))
TN_DIMS = (((0,), (0,)), ((), ()))


def _cparams(*sem):
    return pltpu.CompilerParams(dimension_semantics=sem, vmem_limit_bytes=VMEM_LIMIT)


def _const_spec(shape):
    nd = len(shape)
    return pl.BlockSpec(shape, lambda *_: (0,) * nd)


def _rms(x, g):
    ms = jnp.mean(x * x, axis=-1, keepdims=True)
    return x * lax.rsqrt(ms + EPS) * g


def _log_sigmoid(x):
    return jnp.minimum(x, 0.0) - jnp.log1p(jnp.exp(-jnp.abs(x)))


def _split_bf16(x):
    hi = x.astype(BF16)
    lo = (x - hi.astype(F32)).astype(BF16)
    return hi, lo


def _group_mean_sq(x, group):
    n = x.shape[-1]
    r = lax.broadcasted_iota(jnp.int32, (n, n), 0) // group
    c = lax.broadcasted_iota(jnp.int32, (n, n), 1) // group
    ones = jnp.where(r == c, 1.0, 0.0).astype(BF16)
    hi, lo = _split_bf16(x * x)
    s = jnp.dot(hi, ones, preferred_element_type=F32) + jnp.dot(lo, ones, preferred_element_type=F32)
    return s * (1.0 / group)


def _inproj_kernel(h_ref, g_ref, w_ref, ml_ref, na_ref, gla_ref, gate_ref):
    xn = _rms(h_ref[...], g_ref[...]).astype(BF16)
    off = 0
    for ref, chunk in ((ml_ref, 512), (na_ref, 384), (gla_ref, 256), (gate_ref, 128)):
        width = ref.shape[-1]
        for c in range(0, width, chunk):
            acc = jnp.dot(xn, w_ref[:, off + c:off + c + chunk], preferred_element_type=F32)
            ref[:, c:c + chunk] = acc.astype(ref.dtype)
        off += width


def _inproj(h, g, w):
    n, d = h.shape
    tm = ROW_TILE
    row = lambda cols: pl.BlockSpec((tm, cols), lambda i: (i, 0))
    return pl.pallas_call(
        _inproj_kernel,
        out_shape=(jax.ShapeDtypeStruct((n, ML_COLS), BF16),
                   jax.ShapeDtypeStruct((n, NA_COLS), BF16),
                   jax.ShapeDtypeStruct((n, GLA_COLS), BF16),
                   jax.ShapeDtypeStruct((n, GATE_COLS), F32)),
        grid=(n // tm,),
        in_specs=[row(d), _const_spec((1, d)), _const_spec((d, IN_COLS_PAD))],
        out_specs=(row(ML_COLS), row(NA_COLS), row(GLA_COLS), row(GATE_COLS)),
        compiler_params=_cparams("parallel"),
        name="inproj",
    )(h, g, w)


def _lane_scan(x, op, fill, reverse):
    n = x.shape[-1]
    lane = lax.broadcasted_iota(jnp.int32, x.shape, 1)
    s = 1
    while s < n:
        if reverse:
            x = op(x, jnp.where(lane < n - s, pltpu.roll(x, n - s, 1), fill))
        else:
            x = op(x, jnp.where(lane >= s, pltpu.roll(x, s, 1), fill))
        s *= 2
    return x


def _mlstm_kernel(x_ref, gates_ref, cw_ref, cb_ref, gb_ref, ng_ref, out_ref,
                  xpad_ref, ktok_ref, qT_ref, vT_ref, gT_ref, hsumT_ref, ct_ref):
    T = x_ref.shape[0]
    L = ML_CHUNK
    NC = T // L
    QK = 2 * ML_DIM
    RB = 256

    xpad_ref[0:SUBLANES, :] = jnp.zeros((SUBLANES, QK), F32)
    xpad_ref[T + SUBLANES:T + 2 * SUBLANES, :] = jnp.zeros((SUBLANES, QK), F32)
    for r0 in range(0, T, RB):
        xpad_ref[SUBLANES + r0:SUBLANES + r0 + RB, :] = x_ref[r0:r0 + RB, 0:QK].astype(F32)
    col = lax.broadcasted_iota(jnp.int32, (1, QK), 1)
    kscale = jnp.where(col >= ML_DIM, HEAD_DIM ** -0.5, 1.0)
    for r0 in range(0, T, RB):
        xm = xpad_ref[SUBLANES - 1 + r0:SUBLANES - 1 + r0 + RB, :]
        x0 = xpad_ref[SUBLANES + r0:SUBLANES + r0 + RB, :]
        xp = xpad_ref[SUBLANES + 1 + r0:SUBLANES + 1 + r0 + RB, :]
        y = xm * cw_ref[0:1, :] + x0 * cw_ref[1:2, :] + xp * cw_ref[2:3, :] + cb_ref[...]
        y = y * jax.nn.sigmoid(y) * kscale
        ktok_ref[r0:r0 + RB, :] = y[:, ML_DIM:].astype(BF16)
        q_t = y[:, 0:ML_DIM].T
        v_t = x_ref[r0:r0 + RB, 2 * ML_DIM:3 * ML_DIM].astype(F32).T
        g_t = (gates_ref[r0:r0 + RB, :] + gb_ref[...]).T
        for j in range(RB // L):
            qT_ref[r0 // L + j] = q_t[:, j * L:(j + 1) * L].astype(BF16)
            vT_ref[r0 // L + j] = v_t[:, j * L:(j + 1) * L].astype(BF16)
            gT_ref[r0 // L + j] = g_t[0:4 * SUBLANES, j * L:(j + 1) * L]

    hsumT_ref[...] = jnp.zeros(hsumT_ref.shape, F32)
    ct_ref[...] = jnp.zeros(ct_ref.shape, F32)

    sub = lax.broadcasted_iota(jnp.int32, (PAIR_W, L), 0)
    head_rows = (sub < HEAD_DIM, sub >= HEAD_DIM)
    ri = lax.broadcasted_iota(jnp.int32, (L, L), 0)
    ci = lax.broadcasted_iota(jnp.int32, (L, L), 1)
    tri = (ri <= ci, ri >= ci)

    def body(it, carry):
        m_reps = list(carry)
        chunk = (it, NC - 1 - it)
        gq = []
        for d in range(2):
            gt = gT_ref[chunk[d]]
            ig = gt[2 * SUBLANES * d:2 * SUBLANES * d + SUBLANES, :]
            logf = _log_sigmoid(gt[2 * SUBLANES * d + SUBLANES:2 * SUBLANES * (d + 1), :])
            b = _lane_scan(logf, jnp.add, 0.0, reverse=(d == 1))
            a = ig - b
            g = jnp.sum(logf, axis=1, keepdims=True)
            m_old = m_reps[d]
            big_m = jnp.maximum(m_old, _lane_scan(a, jnp.maximum, -jnp.inf, reverse=(d == 1)))
            m_new = jnp.maximum(g + m_old, g + jnp.max(a, axis=1, keepdims=True))
            m_reps[d] = m_new
            gq.append(dict(a=a, big_m=big_m, w_inter=jnp.exp(m_old - big_m), e=jnp.exp(-(b + big_m)),
                           wk=jnp.exp(g + a - m_new), decay=jnp.exp(g + m_old - m_new)))
        a_pad = jnp.zeros((L - 2 * SUBLANES, L), F32)
        a_cols = jnp.concatenate([gq[0]["a"], gq[1]["a"], a_pad], axis=0).T

        items = []
        for d in range(2):
            c = chunk[d]
            r0 = pl.multiple_of(c * L, L)
            for p in range(ML_HEADS // 2):
                rows = slice(p * PAIR_W, (p + 1) * PAIR_W)
                k_p = ktok_ref[pl.ds(r0, L), rows]
                q_tp = qT_ref[c, rows, :]
                v_tp = vT_ref[c, rows, :]
                for half in range(2):
                    h = 2 * p + half
                    q_tm = jnp.where(head_rows[half], q_tp, jnp.zeros_like(q_tp))
                    v_ta = jnp.where(head_rows[half], v_tp, jnp.ones_like(v_tp))
                    ct_old = ct_ref[d * ML_HEADS + h]
                    s_t = jnp.dot(k_p, q_tm, preferred_element_type=F32)
                    inter_t = jnp.dot(ct_old.astype(BF16), q_tm, preferred_element_type=F32)
                    v_tw = (v_ta.astype(F32) * gq[d]["wk"][h:h + 1, :]).astype(BF16)
                    kv_t = jnp.dot(v_tw, k_p, preferred_element_type=F32)
                    items.append((d, c, p, half, h, v_ta, ct_old, s_t, inter_t, kv_t))
        s2 = []
        for d, c, p, half, h, v_ta, ct_old, s_t, inter_t, kv_t in items:
            dm = jnp.where(tri[d], a_cols[:, SUBLANES * d + h:SUBLANES * d + h + 1]
                           - gq[d]["big_m"][h:h + 1, :], -jnp.inf)
            s2.append((s_t * jnp.exp(dm)).astype(BF16))
        num = [jnp.dot(it_[5], s, preferred_element_type=F32) for it_, s in zip(items, s2)]
        for idx in range(0, len(items), 2):
            outs = []
            for (d, c, p, half, h, v_ta, ct_old, s_t, inter_t, kv_t), n_t in zip(items[idx:idx + 2],
                                                                                 num[idx:idx + 2]):
                n_t = n_t + gq[d]["w_inter"][h:h + 1, :] * inter_t
                den = jnp.concatenate([n_t[HEAD_DIM:], n_t[:HEAD_DIM]], axis=0)
                outs.append(n_t / jnp.maximum(jnp.abs(den), gq[d]["e"][h:h + 1, :]))
                ct_ref[d * ML_HEADS + h] = gq[d]["decay"][h:h + 1, :] * ct_old + kv_t
            hsumT_ref[c, p * PAIR_W:(p + 1) * PAIR_W, :] += jnp.where(head_rows[0], outs[0], outs[1])
        return tuple(m_reps)

    m0 = jnp.zeros((SUBLANES, L), F32)
    lax.fori_loop(0, NC, body, (m0, m0))

    for c in range(NC):
        hs_t = hsumT_ref[c]
        parts = []
        for h in range(ML_HEADS):
            blk = hs_t[h * HEAD_DIM:(h + 1) * HEAD_DIM, :]
            ms = jnp.mean(blk * blk, axis=0, keepdims=True)
            parts.append(blk * lax.rsqrt(ms + EPS))
        y = jnp.concatenate(parts, axis=0).T * ng_ref[...]
        o = x_ref[c * L:(c + 1) * L, 3 * ML_DIM:4 * ML_DIM].astype(F32)
        out_ref[c * L:(c + 1) * L, :] = (y * jax.nn.sigmoid(o)).astype(BF16)


def _mlstm(ml, gates, cw, cb, gb, ng, batch):
    n = ml.shape[0]
    T = n // batch
    return pl.pallas_call(
        _mlstm_kernel,
        out_shape=jax.ShapeDtypeStruct((n, ML_DIM), BF16),
        grid=(batch,),
        in_specs=[pl.BlockSpec((T, ML_COLS), lambda b: (b, 0)),
                  pl.BlockSpec((T, GATE_COLS), lambda b: (b, 0)),
                  _const_spec((3, 2 * ML_DIM)), _const_spec((1, 2 * ML_DIM)),
                  _const_spec((1, GATE_COLS)), _const_spec((1, ML_DIM))],
        out_specs=pl.BlockSpec((T, ML_DIM), lambda b: (b, 0)),
        scratch_shapes=[pltpu.VMEM((T + 2 * SUBLANES, 2 * ML_DIM), F32),
                        pltpu.VMEM((T, ML_DIM), BF16),
                        pltpu.VMEM((T // ML_CHUNK, ML_DIM, ML_CHUNK), BF16),
                        pltpu.VMEM((T // ML_CHUNK, ML_DIM, ML_CHUNK), BF16),
                        pltpu.VMEM((T // ML_CHUNK, 4 * SUBLANES, ML_CHUNK), F32),
                        pltpu.VMEM((T // ML_CHUNK, ML_DIM, ML_CHUNK), F32),
                        pltpu.VMEM((2 * ML_HEADS, PAIR_W, PAIR_W), F32)],
        compiler_params=_cparams("parallel"),
        name="mlstm",
    )(ml, gates, cw, cb, gb, ng)


def _gla_kernel(x_ref, gates_ref, w2_ref, ab_ref, ng_ref, out_ref, bc_ref, osum_ref, st_ref):
    T = x_ref.shape[0]
    L = GLA_CHUNK
    NC = T // L
    KP, VP = GLA_K_PAD, GLA_V_PAD
    RB = 256

    ri = lax.broadcasted_iota(jnp.int32, (RB, RB), 0)
    ci = lax.broadcasted_iota(jnp.int32, (RB, RB), 1)
    same = (ri // L) == (ci // L)
    tri_blk = (jnp.where(same & (ci <= ri), 1.0, 0.0).astype(BF16),
               jnp.where(same & (ci >= ri), 1.0, 0.0).astype(BF16))
    for r0 in range(0, T, RB):
        z = jnp.dot(gates_ref[r0:r0 + RB, :].astype(BF16), w2_ref[...],
                    preferred_element_type=F32) + ab_ref[...]
        hi, lo = _split_bf16(_log_sigmoid(z) * (1.0 / GLA_TAU))
        for d in range(2):
            sl = slice(d * KP, (d + 1) * KP)
            bc_ref[r0:r0 + RB, sl] = (jnp.dot(tri_blk[d], hi[:, sl], preferred_element_type=F32)
                                      + jnp.dot(tri_blk[d], lo[:, sl], preferred_element_type=F32))

    osum_ref[...] = jnp.zeros(osum_ref.shape, F32)
    st_ref[...] = jnp.zeros(st_ref.shape, F32)

    ri = lax.broadcasted_iota(jnp.int32, (L, L), 0)
    ci = lax.broadcasted_iota(jnp.int32, (L, L), 1)
    tri = (ci <= ri, ci >= ri)
    qhead = lax.broadcasted_iota(jnp.int32, (1, KP), 1) // GLA_DK
    vhead = lax.broadcasted_iota(jnp.int32, (1, VP), 1) // HEAD_DIM
    st_mask = (lax.broadcasted_iota(jnp.int32, (VP, KP), 0) // HEAD_DIM
               == lax.broadcasted_iota(jnp.int32, (VP, KP), 1) // GLA_DK)

    def body(it, carry):
        chunk = (it, NC - 1 - it)
        w = []
        for d in range(2):
            r0 = pl.multiple_of(chunk[d] * L, L)
            bc = bc_ref[pl.ds(r0, L), d * KP:(d + 1) * KP]
            btot = bc[L - 1:L, :] if d == 0 else bc[0:1, :]
            q = x_ref[pl.ds(r0, L), 0:KP].astype(F32) * (GLA_DK ** -0.5)
            k = x_ref[pl.ds(r0, L), KP:2 * KP].astype(F32)
            v = x_ref[pl.ds(r0, L), 2 * KP:2 * KP + VP]
            q_dec = q * jnp.exp(bc)
            k_dec = (k * jnp.exp(-bc)).astype(BF16)
            k_tail = (k * jnp.exp(btot - bc)).astype(BF16)
            st_old = st_ref[d]
            o = lax.dot_general(q_dec.astype(BF16), st_old.astype(BF16), NT_DIMS,
                                preferred_element_type=F32)
            a = [lax.dot_general(jnp.where(qhead == h, q_dec, 0.0).astype(BF16), k_dec, NT_DIMS,
                                 preferred_element_type=F32) for h in range(GLA_HEADS)]
            kv = lax.dot_general(v, k_tail, TN_DIMS, preferred_element_type=F32)
            w.append((r0, btot, v, st_old, o, a, kv))
        ab = [[jnp.where(tri[d], a_h, 0.0).astype(BF16) for a_h in w[d][5]] for d in range(2)]
        oh = [[jnp.dot(a_h, w[d][2], preferred_element_type=F32) for a_h in ab[d]] for d in range(2)]
        for d in range(2):
            r0, btot, v, st_old, o, a, kv = w[d]
            for h in range(GLA_HEADS):
                o = o + jnp.where(vhead == h, oh[d][h], 0.0)
            st_ref[d] = st_old * jnp.exp(btot) + jnp.where(st_mask, kv, 0.0)
            osum_ref[pl.ds(r0, L), :] += o
        return carry

    lax.fori_loop(0, NC, body, 0)

    for r0 in range(0, T, RB):
        hs = osum_ref[r0:r0 + RB, :]
        y = hs * lax.rsqrt(_group_mean_sq(hs, HEAD_DIM) + EPS) * ng_ref[...]
        g = x_ref[r0:r0 + RB, 2 * KP + VP:2 * KP + 2 * VP].astype(F32)
        out_ref[r0:r0 + RB, :] = (y * (g * jax.nn.sigmoid(g))).astype(BF16)


def _gla(x, gates, w2, ab, ng, batch):
    n = x.shape[0]
    T = n // batch
    return pl.pallas_call(
        _gla_kernel,
        out_shape=jax.ShapeDtypeStruct((n, GLA_V_PAD), BF16),
        grid=(batch,),
        in_specs=[pl.BlockSpec((T, GLA_COLS), lambda b: (b, 0)),
                  pl.BlockSpec((T, GATE_COLS), lambda b: (b, 0)),
                  _const_spec((GATE_COLS, 2 * GLA_K_PAD)), _const_spec((1, 2 * GLA_K_PAD)),
                  _const_spec((1, GLA_V_PAD))],
        out_specs=pl.BlockSpec((T, GLA_V_PAD), lambda b: (b, 0)),
        scratch_shapes=[pltpu.VMEM((T, 2 * GLA_K_PAD), F32),
                        pltpu.VMEM((T, GLA_V_PAD), F32),
                        pltpu.VMEM((2, GLA_V_PAD, GLA_K_PAD), F32)],
        compiler_params=_cparams("parallel"),
        name="gla",
    )(x, gates, w2, ab, ng)


NA_CLASSES = 8
NA_UNROLL = 4


def _na_kernel(q_ref, k_ref, v_ref, bias_ref, out_ref, *, n_heads_in_pair):
    T = q_ref.shape[0]
    W = GRID_W
    rows = T // W
    band = NA_KH * W
    lane = lax.broadcasted_iota(jnp.int32, (1, PAIR_W), 1)
    head_mask = (lane < HEAD_DIM, lane >= HEAD_DIM)
    scale = HEAD_DIM ** -0.5

    def group(gi, carry):
        items = []
        for u in range(NA_UNROLL):
            r = gi * NA_UNROLL + u
            rs = jnp.clip(r - NA_KH // 2, 0, rows - NA_KH)
            cls = jnp.where(r < NA_KH // 2, r, jnp.where(r > rows - NA_KH // 2, r - (rows - NA_KH), NA_KH // 2))
            q0 = pl.multiple_of(r * W, W)
            k0 = pl.multiple_of(rs * W, W)
            q = q_ref[pl.ds(q0, W), :]
            kb = k_ref[pl.ds(k0, band), :]
            for half in range(n_heads_in_pair):
                q_m = jnp.where(head_mask[half], q, jnp.zeros_like(q))
                s = lax.dot_general(q_m, kb, NT_DIMS, preferred_element_type=F32)
                items.append((u, half, q0, k0, cls, s))
        probs = []
        for u, half, q0, k0, cls, s in items:
            s = s * scale + bias_ref[half, cls]
            m = jnp.max(s, axis=1, keepdims=True)
            p = jnp.exp(s - m)
            l = jnp.sum(p, axis=1, keepdims=True)
            probs.append((u, half, q0, k0, p.astype(BF16), l))
        outs = {}
        for u, half, q0, k0, p, l in probs:
            vb = v_ref[pl.ds(k0, band), :]
            outs[(u, half)] = (q0, jnp.dot(p, vb, preferred_element_type=F32) / l)
        for u in range(NA_UNROLL):
            q0, o0 = outs[(u, 0)]
            second = outs[(u, 1)][1] if n_heads_in_pair == 2 else 0.0
            out_ref[pl.ds(q0, W), :] = jnp.where(head_mask[0], o0, second).astype(BF16)
        return carry

    lax.fori_loop(0, rows // NA_UNROLL, group, 0)


def _na(na, bias, batch):
    n = na.shape[0]
    T = n // batch
    npairs = NA_PAD // PAIR_W
    outs = []
    for p0, p1, nh in ((0, NA_HEADS // 2, 2), (NA_HEADS // 2, npairs, 1)):
        npp = p1 - p0
        outs.append(pl.pallas_call(
            functools.partial(_na_kernel, n_heads_in_pair=nh),
            out_shape=jax.ShapeDtypeStruct((n, npp * PAIR_W), BF16),
            grid=(npp, batch),
            in_specs=[pl.BlockSpec((T, PAIR_W), lambda p, b, o=p0: (b, o + p)),
                      pl.BlockSpec((T, PAIR_W), lambda p, b, o=npairs + p0: (b, o + p)),
                      pl.BlockSpec((T, PAIR_W), lambda p, b, o=2 * npairs + p0: (b, o + p)),
                      pl.BlockSpec((2, NA_CLASSES, GRID_W, NA_KH * GRID_W),
                                   lambda p, b, o=p0: (o + p, 0, 0, 0))],
            out_specs=pl.BlockSpec((T, PAIR_W), lambda p, b: (b, p)),
            compiler_params=_cparams("parallel", "parallel"),
            name=f"natten{nh}",
        )(na, na, na, bias))
    return outs


def _na_bias_table(rpb, rows):
    kh = NA_KH
    rep_rows = np.array(list(range(kh // 2)) + [kh // 2] + list(range(rows - kh // 2 + 1, rows)))
    row_start = np.clip(rep_rows - kh // 2, 0, rows - kh)
    dr = row_start[:, None] + np.arange(kh)[None, :] - rep_rows[:, None] + (kh - 1)
    c = np.arange(GRID_W)
    col_start = np.clip(c - NA_KW // 2, 0, GRID_W - NA_KW)
    in_win = (c[None, :] >= col_start[:, None]) & (c[None, :] < col_start[:, None] + NA_KW)
    dc = np.clip(c[None, :] - c[:, None], -(NA_KW - 1), NA_KW - 1) + (NA_KW - 1)
    sel_r = jnp.asarray(np.eye(2 * kh - 1, dtype=np.float32)[dr])
    sel_c = jnp.asarray(np.eye(2 * NA_KW - 1, dtype=np.float32)[dc])
    tab = jnp.einsum('skr,hrc->hskc', sel_r, rpb.astype(F32), precision=lax.Precision.HIGHEST)
    tab = jnp.einsum('hskc,qwc->hsqkw', tab, sel_c, precision=lax.Precision.HIGHEST)
    tab = jnp.where(in_win[None, None, :, None, :], tab, -jnp.inf)
    tab = tab.reshape(rpb.shape[0], len(rep_rows), GRID_W, kh * GRID_W)
    return jnp.pad(tab, ((0, 2 * (NA_PAD // PAIR_W) - rpb.shape[0]), (0, 0), (0, 0), (0, 0)))


def _outproj_kernel(ml_ref, na0_ref, na1_ref, gla_ref, w_ref, h_ref, o_ref):
    acc = h_ref[...]
    off = 0
    for ref in (ml_ref, na0_ref, na1_ref, gla_ref):
        width = ref.shape[-1]
        acc = acc + jnp.dot(ref[...], w_ref[off:off + width, :], preferred_element_type=F32)
        off += width
    o_ref[...] = acc


def _outproj(ml, na0, na1, gla, w, h):
    n, d = h.shape
    tm = ROW_TILE
    row = lambda cols: pl.BlockSpec((tm, cols), lambda i: (i, 0))
    return pl.pallas_call(
        _outproj_kernel,
        out_shape=jax.ShapeDtypeStruct((n, d), F32),
        grid=(n // tm,),
        in_specs=[row(ml.shape[1]), row(na0.shape[1]), row(na1.shape[1]), row(gla.shape[1]),
                  _const_spec(w.shape), row(d)],
        out_specs=row(d),
        compiler_params=_cparams("parallel"),
        name="outproj",
    )(ml, na0, na1, gla, w, h)


def _memkv_kernel(m_ref, g_ref, wk_ref, wv_ref, k_ref, v_ref):
    mn = _rms(m_ref[...], g_ref[...]).astype(BF16)
    k_ref[...] = jnp.dot(mn, wk_ref[...], preferred_element_type=F32).astype(BF16)
    v_ref[...] = jnp.dot(mn, wv_ref[...], preferred_element_type=F32).astype(BF16)


def _memkv(mem, g, wk, wv):
    n, d = mem.shape
    tm = min(ROW_TILE, n)
    row = pl.BlockSpec((tm, d), lambda i: (i, 0))
    return pl.pallas_call(
        _memkv_kernel,
        out_shape=(jax.ShapeDtypeStruct((n, d), BF16), jax.ShapeDtypeStruct((n, d), BF16)),
        grid=(n // tm,),
        in_specs=[row, _const_spec((1, d)), _const_spec((d, d)), _const_spec((d, d))],
        out_specs=(row, row),
        compiler_params=_cparams("parallel"),
        name="memkv",
    )(mem, g, wk, wv)


def _xattn_kernel(h_ref, g_ref, wq_ref, k_ref, v_ref, wo_ref, o_ref):
    h = h_ref[...]
    d = h.shape[-1]
    hd = d // X_HEADS
    hn = _rms(h, g_ref[...]).astype(BF16)
    q = jnp.dot(hn, wq_ref[...], preferred_element_type=F32).astype(BF16)
    acc = h
    for i in range(X_HEADS):
        sl = slice(i * hd, (i + 1) * hd)
        s = lax.dot_general(q[:, sl], k_ref[:, sl], NT_DIMS, preferred_element_type=F32) * (hd ** -0.5)
        m = jnp.max(s, axis=1, keepdims=True)
        p = jnp.exp(s - m)
        l = jnp.sum(p, axis=1, keepdims=True)
        o = (jnp.dot(p.astype(BF16), v_ref[:, sl], preferred_element_type=F32) / l).astype(BF16)
        acc = acc + jnp.dot(o, wo_ref[sl, :], preferred_element_type=F32)
    o_ref[...] = acc


def _xattn(h, g, wq, k, v, wo, batch):
    n, d = h.shape
    tm = ROW_TILE
    tiles_per_seq = (n // batch) // tm
    n_mem = k.shape[0] // batch
    row = pl.BlockSpec((tm, d), lambda i: (i, 0))
    kv = pl.BlockSpec((n_mem, d), lambda i: (i // tiles_per_seq, 0))
    return pl.pallas_call(
        _xattn_kernel,
        out_shape=jax.ShapeDtypeStruct((n, d), F32),
        grid=(n // tm,),
        in_specs=[row, _const_spec((1, d)), _const_spec((d, d)), kv, kv, _const_spec((d, d))],
        out_specs=row,
        compiler_params=_cparams("parallel"),
        name="xattn",
    )(h, g, wq, k, v, wo)


FFN_CHUNK = 256


def _gelu_tanh(x):
    return 0.5 * x * (1.0 + jnp.tanh(np.sqrt(2.0 / np.pi) * (x + 0.044715 * (x * x * x))))


def _ffn_kernel(h_ref, hp_ref, hx_ref, g_ref, wup_ref, cw_ref, cb_ref, wdn_ref, o_ref, *, tiles_per_seq):
    i = pl.program_id(0)
    h = h_ref[...]
    tm = h.shape[0]
    dff = wdn_ref.shape[0]
    hn = _rms(h, g_ref[...]).astype(BF16)
    halo = jnp.concatenate([hp_ref[...], hx_ref[...]], axis=0)
    halo_n = _rms(halo, g_ref[...]).astype(BF16)
    has_prev = (i % tiles_per_seq != 0).astype(F32)
    has_next = (i % tiles_per_seq != tiles_per_seq - 1).astype(F32)
    row = lax.broadcasted_iota(jnp.int32, (tm, 1), 0)
    acc = h
    for c in range(0, dff, FFN_CHUNK):
        wa = wup_ref[:, c:c + FFN_CHUNK]
        a = jnp.dot(hn, wa, preferred_element_type=F32)
        ah = jnp.dot(halo_n, wa, preferred_element_type=F32)
        gate = jnp.dot(hn, wup_ref[:, dff + c:dff + c + FFN_CHUNK], preferred_element_type=F32)
        a_prev = jnp.where(row == 0, ah[SUBLANES - 1:SUBLANES, :] * has_prev, pltpu.roll(a, 1, 0))
        a_next = jnp.where(row == tm - 1, ah[SUBLANES:SUBLANES + 1, :] * has_next, pltpu.roll(a, tm - 1, 0))
        cw = cw_ref[:, c:c + FFN_CHUNK]
        ac = a_prev * cw[0:1, :] + a * cw[1:2, :] + a_next * cw[2:3, :] + cb_ref[:, c:c + FFN_CHUNK]
        act = (_gelu_tanh(ac) * gate).astype(BF16)
        acc = acc + jnp.dot(act, wdn_ref[c:c + FFN_CHUNK, :], preferred_element_type=F32)
    o_ref[...] = acc


def _ffn(h, g, wup, cw, cb, wdn, batch):
    n, d = h.shape
    tm = ROW_TILE
    tiles_per_seq = (n // batch) // tm
    hb = tm // SUBLANES
    last = n // SUBLANES - 1
    row = pl.BlockSpec((tm, d), lambda i: (i, 0))
    return pl.pallas_call(
        functools.partial(_ffn_kernel, tiles_per_seq=tiles_per_seq),
        out_shape=jax.ShapeDtypeStruct((n, d), F32),
        grid=(n // tm,),
        in_specs=[row,
                  pl.BlockSpec((SUBLANES, d), lambda i: (jnp.maximum(i * hb - 1, 0), 0)),
                  pl.BlockSpec((SUBLANES, d), lambda i: (jnp.minimum((i + 1) * hb, last), 0)),
                  _const_spec((1, d)), _const_spec(wup.shape), _const_spec(cw.shape),
                  _const_spec(cb.shape), _const_spec(wdn.shape)],
        out_specs=row,
        compiler_params=_cparams("parallel"),
        name="convffn",
    )(h, h, h, g, wup, cw, cb, wdn)


def _norm_kernel(h_ref, g_ref, o_ref):
    o_ref[...] = _rms(h_ref[...], g_ref[...])


def _final_norm(h, g):
    n, d = h.shape
    tm = ROW_TILE
    row = pl.BlockSpec((tm, d), lambda i: (i, 0))
    return pl.pallas_call(
        _norm_kernel,
        out_shape=jax.ShapeDtypeStruct((n, d), F32),
        grid=(n // tm,),
        in_specs=[row, _const_spec((1, d))],
        out_specs=row,
        compiler_params=_cparams("parallel"),
        name="final_norm",
    )(h, g)


def _pad_cols(a, n):
    return jnp.pad(a, ((0, 0), (0, n - a.shape[1])))


def _pad_rows(a, n):
    return jnp.pad(a, ((0, n - a.shape[0]), (0, 0)))


def _prep_w_in(w):
    offs = np.concatenate([[0], np.cumsum(IN_SIZES)])
    seg = [w[:, offs[i]:offs[i + 1]] for i in range(len(IN_SIZES))]
    ml = seg[0:4]
    na = [_pad_cols(s, NA_PAD) for s in seg[5:8]]
    gla = [_pad_cols(seg[8], GLA_K_PAD), _pad_cols(seg[9], GLA_K_PAD),
           _pad_cols(seg[10], GLA_V_PAD), _pad_cols(seg[11], GLA_V_PAD)]
    gates = [_pad_cols(seg[4][:, j * ML_HEADS:(j + 1) * ML_HEADS], SUBLANES) for j in range(4)]
    gates = _pad_cols(jnp.concatenate(gates + [seg[12]], axis=1), GATE_COLS)
    return jnp.concatenate(ml + na + gla + [gates], axis=1).astype(BF16)


def _prep_gate_bias(gb):
    return _pad_cols(jnp.concatenate([_pad_cols(gb[j:j + 1], SUBLANES) for j in range(4)], axis=1), GATE_COLS)


def _prep_gla_gate(w2, ab):
    w = jnp.zeros((GATE_COLS, 2 * GLA_K_PAD), F32)
    b = jnp.zeros((1, 2 * GLA_K_PAD), F32)
    for z in range(2):
        r0 = GLA_GATE_OFF + z * GLA_RANK
        w = w.at[r0:r0 + GLA_RANK, z * GLA_K_PAD:z * GLA_K_PAD + GLA_KDIM].set(w2[z])
        b = b.at[0, z * GLA_K_PAD:z * GLA_K_PAD + GLA_KDIM].set(ab[z])
    return w.astype(BF16), b


def _prep_w_out(w):
    parts = [w[0:ML_DIM], _pad_rows(w[ML_DIM:ML_DIM + NA_DIM], NA_PAD),
             _pad_rows(w[ML_DIM + NA_DIM:], GLA_V_PAD)]
    return jnp.concatenate(parts, axis=0).astype(BF16)


def kernel(x, mem, mix_norm_g, w_in, ml_conv_w, ml_conv_b, ml_gate_b, ml_norm_g, na_rpb,
           gla_a_w2, gla_a_b, gla_norm_g, w_out, xattn_norm_g, mem_norm_g,
           w_xq, w_xk, w_xv, w_xo, ffn_norm_g, w_up, ffn_conv_w, ffn_conv_b, w_down,
           final_norm_g):
    batch, seq, d = x.shape
    depth = w_in.shape[0]
    rows = seq // GRID_W
    assert seq % ROW_TILE == 0 and rows >= NA_KH and rows % NA_UNROLL == 0
    h = x.reshape(batch * seq, d)
    mem2 = mem.reshape(batch * mem.shape[1], d)
    r2 = lambda v: v.reshape(1, -1)
    for l in range(depth):
        ml, na, gla, gates = _inproj(h, r2(mix_norm_g[l]), _prep_w_in(w_in[l]))
        ml_out = _mlstm(ml, gates, ml_conv_w[l], r2(ml_conv_b[l]), _prep_gate_bias(ml_gate_b[l]),
                        r2(ml_norm_g[l]), batch)
        na0, na1 = _na(na, _na_bias_table(na_rpb[l], rows), batch)
        w2, ab = _prep_gla_gate(gla_a_w2[l], gla_a_b[l])
        gla_out = _gla(gla, gates, w2, ab, _pad_cols(r2(gla_norm_g[l]), GLA_V_PAD), batch)
        h = _outproj(ml_out, na0, na1, gla_out, _prep_w_out(w_out[l]), h)
        k, v = _memkv(mem2, r2(mem_norm_g[l]), w_xk[l].astype(BF16), w_xv[l].astype(BF16))
        h = _xattn(h, r2(xattn_norm_g[l]), w_xq[l].astype(BF16), k, v, w_xo[l].astype(BF16), batch)
        h = _ffn(h, r2(ffn_norm_g[l]), w_up[l].astype(BF16), ffn_conv_w[l], r2(ffn_conv_b[l]),
                 w_down[l].astype(BF16), batch)
    return _final_norm(h, final_norm_g.reshape(1, d)).reshape(batch, seq, d)
```

```python
import functools

import numpy as np
import jax
import jax.numpy as jnp
from jax import lax
from jax.experimental import pallas as pl
from jax.experimental.pallas import tpu as pltpu

F32 = jnp.float32
BF16 = jnp.bfloat16

HEAD_DIM = 64
ML_HEADS = 6
NA_HEADS = 5
GLA_HEADS = 5
ML_DIM = ML_HEADS * HEAD_DIM
NA_DIM = NA_HEADS * HEAD_DIM
GLA_DK = 32
GLA_KDIM = GLA_HEADS * GLA_DK
GLA_VDIM = GLA_HEADS * HEAD_DIM
GLA_RANK = 16
GLA_TAU = 16.0
GRID_W = 64
NA_KH = 8
NA_KW = 16
ML_CHUNK = 128
GLA_CHUNK = 64
X_HEADS = 4
EPS = 1e-6
IN_SIZES = (ML_DIM, ML_DIM, ML_DIM, ML_DIM, 4 * ML_HEADS,
            NA_DIM, NA_DIM, NA_DIM,
            GLA_KDIM, GLA_KDIM, GLA_VDIM, GLA_VDIM, 2 * GLA_RANK)

LANES = 128
SUBLANES = 8

PAIR_W = 2 * HEAD_DIM
ML_COLS = 4 * ML_DIM
NA_PAD = 3 * PAIR_W
NA_COLS = 3 * NA_PAD
GLA_K_PAD = 256
GLA_V_PAD = 3 * PAIR_W
GLA_COLS = 2 * GLA_K_PAD + 2 * GLA_V_PAD
GATE_COLS = LANES
GLA_GATE_OFF = 4 * SUBLANES
IN_COLS_PAD = ML_COLS + NA_COLS + GLA_COLS + GATE_COLS

ROW_TILE = 512
VMEM_LIMIT = 56 * 1024 * 1024

NT_DIMS = (((1,), (1,)), ((), ()))
TN_DIMS = (((0,), (0,)), ((), ()))


def _cparams(*sem):
    return pltpu.CompilerParams(dimension_semantics=sem, vmem_limit_bytes=VMEM_LIMIT)


def _const_spec(shape):
    nd = len(shape)
    return pl.BlockSpec(shape, lambda *_: (0,) * nd)


def _rms(x, g):
    ms = jnp.mean(x * x, axis=-1, keepdims=True)
    return x * lax.rsqrt(ms + EPS) * g


def _log_sigmoid(x):
    return jnp.minimum(x, 0.0) - jnp.log1p(jnp.exp(-jnp.abs(x)))


def _split_bf16(x):
    hi = x.astype(BF16)
    lo = (x - hi.astype(F32)).astype(BF16)
    return hi, lo


def _group_mean_sq(x, group):
    n = x.shape[-1]
    r = lax.broadcasted_iota(jnp.int32, (n, n), 0) // group
    c = lax.broadcasted_iota(jnp.int32, (n, n), 1) // group
    ones = jnp.where(r == c, 1.0, 0.0).astype(BF16)
    hi, lo = _split_bf16(x * x)
    s = jnp.dot(hi, ones, preferred_element_type=F32) + jnp.dot(lo, ones, preferred_element_type=F32)
    return s * (1.0 / group)


def _halo_flags(i, tiles_per_seq):
    has_prev = (i % tiles_per_seq != 0).astype(F32)
    has_next = (i % tiles_per_seq != tiles_per_seq - 1).astype(F32)
    return has_prev, has_next


def _conv3(a_ext, tm, w, b, has_prev, has_next):
    row = lax.broadcasted_iota(jnp.int32, (tm, 1), 0)
    a = a_ext[0:tm]
    a_prev = jnp.where(row == 0, a_ext[tm + SUBLANES - 1:tm + SUBLANES, :] * has_prev, pltpu.roll(a, 1, 0))
    a_next = jnp.where(row == tm - 1, a_ext[tm + SUBLANES:tm + SUBLANES + 1, :] * has_next,
                       pltpu.roll(a, tm - 1, 0))
    return a_prev * w[0:1, :] + a * w[1:2, :] + a_next * w[2:3, :] + b


def _inproj_kernel(h_ref, hp_ref, hx_ref, g_ref, w_ref, cw_ref, cb_ref, gb_ref,
                   ktok_ref, otok_ref, qT_ref, vT_ref, gsT_ref, na_ref, gla_ref, gate_ref, *, tiles_per_seq):
    tm = h_ref.shape[0]
    L = ML_CHUNK
    has_prev, has_next = _halo_flags(pl.program_id(0), tiles_per_seq)
    xn = _rms(h_ref[...], g_ref[...]).astype(BF16)
    halo = jnp.concatenate([hp_ref[...], hx_ref[...]], axis=0)
    xn_ext = jnp.concatenate([xn, _rms(halo, g_ref[...]).astype(BF16)], axis=0)

    def proj(lhs, c0, c1):
        return jnp.dot(lhs, w_ref[:, c0:c1], preferred_element_type=F32)

    for part in range(2):
        cols = slice(part * ML_DIM, (part + 1) * ML_DIM)
        y = _conv3(proj(xn_ext, cols.start, cols.stop), tm, cw_ref[:, cols], cb_ref[:, cols], has_prev, has_next)
        y = y * jax.nn.sigmoid(y)
        if part == 0:
            y_t = y.T
            for j in range(tm // L):
                qT_ref[j] = y_t[:, j * L:(j + 1) * L].astype(BF16)
        else:
            ktok_ref[...] = (y * HEAD_DIM ** -0.5).astype(BF16)
    v_t = proj(xn, 2 * ML_DIM, 3 * ML_DIM).T
    for j in range(tm // L):
        vT_ref[j] = v_t[:, j * L:(j + 1) * L].astype(BF16)
    otok_ref[...] = proj(xn, 3 * ML_DIM, 4 * ML_DIM).astype(BF16)
    off = ML_COLS
    for ref, chunk in ((na_ref, 384), (gla_ref, 256), (gate_ref, 128)):
        width = ref.shape[-1]
        for c in range(0, width, chunk):
            ref[:, c:c + chunk] = proj(xn, off + c, off + c + chunk).astype(ref.dtype)
        off += width
    g_t = (gate_ref[...] + gb_ref[...]).T
    for j in range(tm // L):
        gsT_ref[j] = g_t[0:4 * SUBLANES, j * L:(j + 1) * L]


def _halo_specs(tm, d, n):
    hb = tm // SUBLANES
    last = n // SUBLANES - 1
    return [pl.BlockSpec((SUBLANES, d), lambda i: (jnp.maximum(i * hb - 1, 0), 0)),
            pl.BlockSpec((SUBLANES, d), lambda i: (jnp.minimum((i + 1) * hb, last), 0))]


def _inproj(h, g, w, cw, cb, gb, batch):
    n, d = h.shape
    tm = ROW_TILE
    nc = tm // ML_CHUNK
    row = lambda cols: pl.BlockSpec((tm, cols), lambda i: (i, 0))
    slab = lambda rows: pl.BlockSpec((nc, rows, ML_CHUNK), lambda i: (i, 0, 0))
    return pl.pallas_call(
        functools.partial(_inproj_kernel, tiles_per_seq=(n // batch) // tm),
        out_shape=(jax.ShapeDtypeStruct((n, ML_DIM), BF16),
                   jax.ShapeDtypeStruct((n, ML_DIM), BF16),
                   jax.ShapeDtypeStruct((n // ML_CHUNK, ML_DIM, ML_CHUNK), BF16),
                   jax.ShapeDtypeStruct((n // ML_CHUNK, ML_DIM, ML_CHUNK), BF16),
                   jax.ShapeDtypeStruct((n // ML_CHUNK, 4 * SUBLANES, ML_CHUNK), F32),
                   jax.ShapeDtypeStruct((n, NA_COLS), BF16),
                   jax.ShapeDtypeStruct((n, GLA_COLS), BF16),
                   jax.ShapeDtypeStruct((n, GATE_COLS), F32)),
        grid=(n // tm,),
        in_specs=[row(d)] + _halo_specs(tm, d, n) + [
            _const_spec((1, d)), _const_spec((d, IN_COLS_PAD)), _const_spec(cw.shape),
            _const_spec(cb.shape), _const_spec(gb.shape)],
        out_specs=(row(ML_DIM), row(ML_DIM), slab(ML_DIM), slab(ML_DIM), slab(4 * SUBLANES),
                   row(NA_COLS), row(GLA_COLS), row(GATE_COLS)),
        compiler_params=_cparams("parallel"),
        name="inproj",
    )(h, h, h, g, w, cw, cb, gb)


def _lane_scan(x, op, fill, reverse):
    n = x.shape[-1]
    lane = lax.broadcasted_iota(jnp.int32, x.shape, 1)
    s = 1
    while s < n:
        if reverse:
            x = op(x, jnp.where(lane < n - s, pltpu.roll(x, n - s, 1), fill))
        else:
            x = op(x, jnp.where(lane >= s, pltpu.roll(x, s, 1), fill))
        s *= 2
    return x


def _mlstm_kernel(ktok_ref, otok_ref, qT_ref, vT_ref, gsT_ref, ng_ref, out_ref,
                  pre_ref, acol_ref, hsumT_ref, ct_ref):
    NC, _, L = qT_ref.shape

    def gate_group(grp):
        return gsT_ref[:, grp * SUBLANES:(grp + 1) * SUBLANES, :].reshape(NC * SUBLANES, L)

    a_all = []
    for d in range(2):
        logf = _log_sigmoid(gate_group(2 * d + 1))
        b = _lane_scan(logf, jnp.add, 0.0, reverse=(d == 1))
        a = gate_group(2 * d) - b
        pre_ref[d, 0] = b
        pre_ref[d, 1] = a
        pre_ref[d, 2] = _lane_scan(a, jnp.maximum, -jnp.inf, reverse=(d == 1))
        pre_ref[d, 3] = jnp.broadcast_to(jnp.sum(logf, axis=1, keepdims=True), a.shape)
        pre_ref[d, 4] = jnp.broadcast_to(jnp.max(a, axis=1, keepdims=True), a.shape)
        a_all.append(a)
    a_pad = jnp.zeros((L - 2 * SUBLANES, L), F32)
    for c in range(NC):
        rows = slice(c * SUBLANES, (c + 1) * SUBLANES)
        acol_ref[c] = jnp.concatenate([a_all[0][rows], a_all[1][rows], a_pad], axis=0).T

    hsumT_ref[...] = jnp.zeros(hsumT_ref.shape, F32)
    ct_ref[...] = jnp.zeros(ct_ref.shape, F32)

    sub = lax.broadcasted_iota(jnp.int32, (PAIR_W, L), 0)
    head_rows = (sub < HEAD_DIM, sub >= HEAD_DIM)
    ri = lax.broadcasted_iota(jnp.int32, (L, L), 0)
    ci = lax.broadcasted_iota(jnp.int32, (L, L), 1)
    tri = (ri <= ci, ri >= ci)

    def body(it, carry):
        m_reps = list(carry)
        chunk = (it, NC - 1 - it)
        gq = []
        for d in range(2):
            r8 = pl.ds(pl.multiple_of(chunk[d] * SUBLANES, SUBLANES), SUBLANES)
            b, a, amax_run, g, amax = (pre_ref[d, n, r8, :] for n in range(5))
            m_old = m_reps[d]
            big_m = jnp.maximum(m_old, amax_run)
            m_new = jnp.maximum(g + m_old, g + amax)
            m_reps[d] = m_new
            gq.append(dict(big_m=big_m, w_inter=jnp.exp(m_old - big_m), e=jnp.exp(-(b + big_m)),
                           wk=jnp.exp(g + a - m_new), decay=jnp.exp(g + m_old - m_new),
                           a_cols=acol_ref[chunk[d]]))

        items = []
        for d in range(2):
            c = chunk[d]
            r0 = pl.multiple_of(c * L, L)
            for p in range(ML_HEADS // 2):
                rows = slice(p * PAIR_W, (p + 1) * PAIR_W)
                k_p = ktok_ref[pl.ds(r0, L), rows]
                q_tp = qT_ref[c, rows, :]
                v_tp = vT_ref[c, rows, :]
                for half in range(2):
                    h = 2 * p + half
                    q_tm = jnp.where(head_rows[half], q_tp, jnp.zeros_like(q_tp))
                    v_ta = jnp.where(head_rows[half], v_tp, jnp.ones_like(v_tp))
                    ct_old = ct_ref[d * ML_HEADS + h]
                    s_t = jnp.dot(k_p, q_tm, preferred_element_type=F32)
                    inter_t = jnp.dot(ct_old.astype(BF16), q_tm, preferred_element_type=F32)
                    v_tw = (v_ta.astype(F32) * gq[d]["wk"][h:h + 1, :]).astype(BF16)
                    kv_t = jnp.dot(v_tw, k_p, preferred_element_type=F32)
                    items.append((d, c, p, half, h, v_ta, ct_old, s_t, inter_t, kv_t))
        s2 = []
        for d, c, p, half, h, v_ta, ct_old, s_t, inter_t, kv_t in items:
            dm = jnp.where(tri[d], gq[d]["a_cols"][:, SUBLANES * d + h:SUBLANES * d + h + 1]
                           - gq[d]["big_m"][h:h + 1, :], -jnp.inf)
            s2.append((s_t * jnp.exp(dm)).astype(BF16))
        num = [jnp.dot(it_[5], s, preferred_element_type=F32) for it_, s in zip(items, s2)]
        for idx in range(0, len(items), 2):
            outs = []
            for (d, c, p, half, h, v_ta, ct_old, s_t, inter_t, kv_t), n_t in zip(items[idx:idx + 2],
                                                                                 num[idx:idx + 2]):
                n_t = n_t + gq[d]["w_inter"][h:h + 1, :] * inter_t
                den = jnp.concatenate([n_t[HEAD_DIM:], n_t[:HEAD_DIM]], axis=0)
                outs.append(n_t / jnp.maximum(jnp.abs(den), gq[d]["e"][h:h + 1, :]))
                ct_ref[d * ML_HEADS + h] = gq[d]["decay"][h:h + 1, :] * ct_old + kv_t
            hsumT_ref[c, p * PAIR_W:(p + 1) * PAIR_W, :] += jnp.where(head_rows[0], outs[0], outs[1])
        return tuple(m_reps)

    m0 = jnp.zeros((SUBLANES, L), F32)
    lax.fori_loop(0, NC, body, (m0, m0))

    for c in range(NC):
        hs_t = hsumT_ref[c]
        parts = []
        for h in range(ML_HEADS):
            blk = hs_t[h * HEAD_DIM:(h + 1) * HEAD_DIM, :]
            ms = jnp.mean(blk * blk, axis=0, keepdims=True)
            parts.append(blk * lax.rsqrt(ms + EPS))
        y = jnp.concatenate(parts, axis=0).T * ng_ref[...]
        o = otok_ref[c * L:(c + 1) * L, :].astype(F32)
        out_ref[c * L:(c + 1) * L, :] = (y * jax.nn.sigmoid(o)).astype(BF16)


def _mlstm(ktok, otok, q_t, v_t, gs_t, ng, batch):
    n = ktok.shape[0]
    T = n // batch
    nc = T // ML_CHUNK
    tok = pl.BlockSpec((T, ML_DIM), lambda b: (b, 0))
    slab = lambda rows: pl.BlockSpec((nc, rows, ML_CHUNK), lambda b: (b, 0, 0))
    return pl.pallas_call(
        _mlstm_kernel,
        out_shape=jax.ShapeDtypeStruct((n, ML_DIM), BF16),
        grid=(batch,),
        in_specs=[tok, tok, slab(ML_DIM), slab(ML_DIM), slab(4 * SUBLANES), _const_spec((1, ML_DIM))],
        out_specs=tok,
        scratch_shapes=[pltpu.VMEM((2, 5, nc * SUBLANES, ML_CHUNK), F32),
                        pltpu.VMEM((nc, ML_CHUNK, LANES), F32),
                        pltpu.VMEM((nc, ML_DIM, ML_CHUNK), F32),
                        pltpu.VMEM((2 * ML_HEADS, PAIR_W, PAIR_W), F32)],
        compiler_params=_cparams("parallel"),
        name="mlstm",
    )(ktok, otok, q_t, v_t, gs_t, ng)


def _gla_kernel(x_ref, gates_ref, w2_ref, ab_ref, ng_ref, out_ref, bc_ref, osum_ref, st_ref):
    T = x_ref.shape[0]
    L = GLA_CHUNK
    NC = T // L
    KP, VP = GLA_K_PAD, GLA_V_PAD
    RB = 256

    ri = lax.broadcasted_iota(jnp.int32, (RB, RB), 0)
    ci = lax.broadcasted_iota(jnp.int32, (RB, RB), 1)
    same = (ri // L) == (ci // L)
    tri_blk = (jnp.where(same & (ci <= ri), 1.0, 0.0).astype(BF16),
               jnp.where(same & (ci >= ri), 1.0, 0.0).astype(BF16))
    for r0 in range(0, T, RB):
        z = jnp.dot(gates_ref[r0:r0 + RB, :].astype(BF16), w2_ref[...],
                    preferred_element_type=F32) + ab_ref[...]
        hi, lo = _split_bf16(_log_sigmoid(z) * (1.0 / GLA_TAU))
        for d in range(2):
            sl = slice(d * KP, (d + 1) * KP)
            bc_ref[r0:r0 + RB, sl] = (jnp.dot(tri_blk[d], hi[:, sl], preferred_element_type=F32)
                                      + jnp.dot(tri_blk[d], lo[:, sl], preferred_element_type=F32))

    osum_ref[...] = jnp.zeros(osum_ref.shape, F32)
    st_ref[...] = jnp.zeros(st_ref.shape, F32)

    ri = lax.broadcasted_iota(jnp.int32, (L, L), 0)
    ci = lax.broadcasted_iota(jnp.int32, (L, L), 1)
    tri = (ci <= ri, ci >= ri)
    qhead = lax.broadcasted_iota(jnp.int32, (1, KP), 1) // GLA_DK
    vhead = lax.broadcasted_iota(jnp.int32, (1, VP), 1) // HEAD_DIM
    st_mask = (lax.broadcasted_iota(jnp.int32, (VP, KP), 0) // HEAD_DIM
               == lax.broadcasted_iota(jnp.int32, (VP, KP), 1) // GLA_DK)

    def body(it, carry):
        chunk = (it, NC - 1 - it)
        w = []
        for d in range(2):
            r0 = pl.multiple_of(chunk[d] * L, L)
            bc = bc_ref[pl.ds(r0, L), d * KP:(d + 1) * KP]
            btot = bc[L - 1:L, :] if d == 0 else bc[0:1, :]
            q = x_ref[pl.ds(r0, L), 0:KP].astype(F32) * (GLA_DK ** -0.5)
            k = x_ref[pl.ds(r0, L), KP:2 * KP].astype(F32)
            v = x_ref[pl.ds(r0, L), 2 * KP:2 * KP + VP]
            q_dec = q * jnp.exp(bc)
            k_dec = (k * jnp.exp(-bc)).astype(BF16)
            k_tail = (k * jnp.exp(btot - bc)).astype(BF16)
            st_old = st_ref[d]
            o = lax.dot_general(q_dec.astype(BF16), st_old.astype(BF16), NT_DIMS,
                                preferred_element_type=F32)
            a = [lax.dot_general(jnp.where(qhead == h, q_dec, 0.0).astype(BF16), k_dec, NT_DIMS,
                                 preferred_element_type=F32) for h in range(GLA_HEADS)]
            kv = lax.dot_general(v, k_tail, TN_DIMS, preferred_element_type=F32)
            w.append((r0, btot, v, st_old, o, a, kv))
        ab = [[jnp.where(tri[d], a_h, 0.0).astype(BF16) for a_h in w[d][5]] for d in range(2)]
        oh = [[jnp.dot(a_h, w[d][2], preferred_element_type=F32) for a_h in ab[d]] for d in range(2)]
        for d in range(2):
            r0, btot, v, st_old, o, a, kv = w[d]
            for h in range(GLA_HEADS):
                o = o + jnp.where(vhead == h, oh[d][h], 0.0)
            st_ref[d] = st_old * jnp.exp(btot) + jnp.where(st_mask, kv, 0.0)
            osum_ref[pl.ds(r0, L), :] += o
        return carry

    lax.fori_loop(0, NC, body, 0)

    for r0 in range(0, T, RB):
        hs = osum_ref[r0:r0 + RB, :]
        y = hs * lax.rsqrt(_group_mean_sq(hs, HEAD_DIM) + EPS) * ng_ref[...]
        g = x_ref[r0:r0 + RB, 2 * KP + VP:2 * KP + 2 * VP].astype(F32)
        out_ref[r0:r0 + RB, :] = (y * (g * jax.nn.sigmoid(g))).astype(BF16)


def _gla(x, gates, w2, ab, ng, batch):
    n = x.shape[0]
    T = n // batch
    return pl.pallas_call(
        _gla_kernel,
        out_shape=jax.ShapeDtypeStruct((n, GLA_V_PAD), BF16),
        grid=(batch,),
        in_specs=[pl.BlockSpec((T, GLA_COLS), lambda b: (b, 0)),
                  pl.BlockSpec((T, GATE_COLS), lambda b: (b, 0)),
                  _const_spec((GATE_COLS, 2 * GLA_K_PAD)), _const_spec((1, 2 * GLA_K_PAD)),
                  _const_spec((1, GLA_V_PAD))],
        out_specs=pl.BlockSpec((T, GLA_V_PAD), lambda b: (b, 0)),
        scratch_shapes=[pltpu.VMEM((T, 2 * GLA_K_PAD), F32),
                        pltpu.VMEM((T, GLA_V_PAD), F32),
                        pltpu.VMEM((2, GLA_V_PAD, GLA_K_PAD), F32)],
        compiler_params=_cparams("parallel"),
        name="gla",
    )(x, gates, w2, ab, ng)


NA_CLASSES = 8
NA_UNROLL = 4


def _na_kernel(q_ref, k_ref, v_ref, bias_ref, out_ref, *, n_heads_in_pair):
    T = q_ref.shape[0]
    W = GRID_W
    rows = T // W
    band = NA_KH * W
    lane = lax.broadcasted_iota(jnp.int32, (1, PAIR_W), 1)
    head_mask = (lane < HEAD_DIM, lane >= HEAD_DIM)
    scale = HEAD_DIM ** -0.5

    def group(gi, carry):
        items = []
        for u in range(NA_UNROLL):
            r = gi * NA_UNROLL + u
            rs = jnp.clip(r - NA_KH // 2, 0, rows - NA_KH)
            cls = jnp.where(r < NA_KH // 2, r, jnp.where(r > rows - NA_KH // 2, r - (rows - NA_KH), NA_KH // 2))
            q0 = pl.multiple_of(r * W, W)
            k0 = pl.multiple_of(rs * W, W)
            q = q_ref[pl.ds(q0, W), :]
            kb = k_ref[pl.ds(k0, band), :]
            for half in range(n_heads_in_pair):
                q_m = jnp.where(head_mask[half], q, jnp.zeros_like(q))
                s = lax.dot_general(q_m, kb, NT_DIMS, preferred_element_type=F32)
                items.append((u, half, q0, k0, cls, s))
        probs = []
        for u, half, q0, k0, cls, s in items:
            s = s * scale + bias_ref[half, cls]
            m = jnp.max(s, axis=1, keepdims=True)
            p = jnp.exp(s - m)
            l = jnp.sum(p, axis=1, keepdims=True)
            probs.append((u, half, q0, k0, p.astype(BF16), l))
        outs = {}
        for u, half, q0, k0, p, l in probs:
            vb = v_ref[pl.ds(k0, band), :]
            outs[(u, half)] = (q0, jnp.dot(p, vb, preferred_element_type=F32) / l)
        for u in range(NA_UNROLL):
            q0, o0 = outs[(u, 0)]
            second = outs[(u, 1)][1] if n_heads_in_pair == 2 else 0.0
            out_ref[pl.ds(q0, W), :] = jnp.where(head_mask[0], o0, second).astype(BF16)
        return carry

    lax.fori_loop(0, rows // NA_UNROLL, group, 0)


def _na(na, bias, batch):
    n = na.shape[0]
    T = n // batch
    npairs = NA_PAD // PAIR_W
    outs = []
    for p0, p1, nh in ((0, NA_HEADS // 2, 2), (NA_HEADS // 2, npairs, 1)):
        npp = p1 - p0
        outs.append(pl.pallas_call(
            functools.partial(_na_kernel, n_heads_in_pair=nh),
            out_shape=jax.ShapeDtypeStruct((n, npp * PAIR_W), BF16),
            grid=(npp, batch),
            in_specs=[pl.BlockSpec((T, PAIR_W), lambda p, b, o=p0: (b, o + p)),
                      pl.BlockSpec((T, PAIR_W), lambda p, b, o=npairs + p0: (b, o + p)),
                      pl.BlockSpec((T, PAIR_W), lambda p, b, o=2 * npairs + p0: (b, o + p)),
                      pl.BlockSpec((2, NA_CLASSES, GRID_W, NA_KH * GRID_W),
                                   lambda p, b, o=p0: (o + p, 0, 0, 0))],
            out_specs=pl.BlockSpec((T, PAIR_W), lambda p, b: (b, p)),
            compiler_params=_cparams("parallel", "parallel"),
            name=f"natten{nh}",
        )(na, na, na, bias))
    return outs


def _na_bias_table(rpb, rows):
    kh = NA_KH
    rep_rows = np.array(list(range(kh // 2)) + [kh // 2] + list(range(rows - kh // 2 + 1, rows)))
    row_start = np.clip(rep_rows - kh // 2, 0, rows - kh)
    dr = row_start[:, None] + np.arange(kh)[None, :] - rep_rows[:, None] + (kh - 1)
    c = np.arange(GRID_W)
    col_start = np.clip(c - NA_KW // 2, 0, GRID_W - NA_KW)
    in_win = (c[None, :] >= col_start[:, None]) & (c[None, :] < col_start[:, None] + NA_KW)
    dc = np.clip(c[None, :] - c[:, None], -(NA_KW - 1), NA_KW - 1) + (NA_KW - 1)
    sel_r = jnp.asarray(np.eye(2 * kh - 1, dtype=np.float32)[dr])
    sel_c = jnp.asarray(np.eye(2 * NA_KW - 1, dtype=np.float32)[dc])
    tab = jnp.einsum('skr,hrc->hskc', sel_r, rpb.astype(F32), precision=lax.Precision.HIGHEST)
    tab = jnp.einsum('hskc,qwc->hsqkw', tab, sel_c, precision=lax.Precision.HIGHEST)
    tab = jnp.where(in_win[None, None, :, None, :], tab, -jnp.inf)
    tab = tab.reshape(rpb.shape[0], len(rep_rows), GRID_W, kh * GRID_W)
    return jnp.pad(tab, ((0, 2 * (NA_PAD // PAIR_W) - rpb.shape[0]), (0, 0), (0, 0), (0, 0)))


def _outproj_kernel(ml_ref, na0_ref, na1_ref, gla_ref, w_ref, h_ref, o_ref):
    acc = h_ref[...]
    off = 0
    for ref in (ml_ref, na0_ref, na1_ref, gla_ref):
        width = ref.shape[-1]
        acc = acc + jnp.dot(ref[...], w_ref[off:off + width, :], preferred_element_type=F32)
        off += width
    o_ref[...] = acc


def _outproj(ml, na0, na1, gla, w, h):
    n, d = h.shape
    tm = ROW_TILE
    row = lambda cols: pl.BlockSpec((tm, cols), lambda i: (i, 0))
    return pl.pallas_call(
        _outproj_kernel,
        out_shape=jax.ShapeDtypeStruct((n, d), F32),
        grid=(n // tm,),
        in_specs=[row(ml.shape[1]), row(na0.shape[1]), row(na1.shape[1]), row(gla.shape[1]),
                  _const_spec(w.shape), row(d)],
        out_specs=row(d),
        compiler_params=_cparams("parallel"),
        name="outproj",
    )(ml, na0, na1, gla, w, h)


def _memkv_kernel(m_ref, g_ref, wk_ref, wv_ref, k_ref, v_ref):
    mn = _rms(m_ref[...], g_ref[...]).astype(BF16)
    k_ref[...] = jnp.dot(mn, wk_ref[...], preferred_element_type=F32).astype(BF16)
    v_ref[...] = jnp.dot(mn, wv_ref[...], preferred_element_type=F32).astype(BF16)


def _memkv(mem, g, wk, wv):
    n, d = mem.shape
    tm = min(ROW_TILE, n)
    row = pl.BlockSpec((tm, d), lambda i: (i, 0))
    return pl.pallas_call(
        _memkv_kernel,
        out_shape=(jax.ShapeDtypeStruct((n, d), BF16), jax.ShapeDtypeStruct((n, d), BF16)),
        grid=(n // tm,),
        in_specs=[row, _const_spec((1, d)), _const_spec((d, d)), _const_spec((d, d))],
        out_specs=(row, row),
        compiler_params=_cparams("parallel"),
        name="memkv",
    )(mem, g, wk, wv)


def _xattn_kernel(h_ref, g_ref, wq_ref, k_ref, v_ref, wo_ref, o_ref):
    h = h_ref[...]
    d = h.shape[-1]
    hd = d // X_HEADS
    hn = _rms(h, g_ref[...]).astype(BF16)
    q = jnp.dot(hn, wq_ref[...], preferred_element_type=F32).astype(BF16)
    heads = [slice(i * hd, (i + 1) * hd) for i in range(X_HEADS)]
    scores = [lax.dot_general(q[:, sl], k_ref[:, sl], NT_DIMS, preferred_element_type=F32) for sl in heads]
    probs = []
    for s in scores:
        s = s * (hd ** -0.5)
        p = jnp.exp(s - jnp.max(s, axis=1, keepdims=True))
        probs.append((p.astype(BF16), jnp.sum(p, axis=1, keepdims=True)))
    ctx = [jnp.dot(p, v_ref[:, sl], preferred_element_type=F32) for (p, _), sl in zip(probs, heads)]
    acc = h
    for o, (_, l), sl in zip(ctx, probs, heads):
        acc = acc + jnp.dot((o / l).astype(BF16), wo_ref[sl, :], preferred_element_type=F32)
    o_ref[...] = acc


def _xattn(h, g, wq, k, v, wo, batch):
    n, d = h.shape
    tm = ROW_TILE
    tiles_per_seq = (n // batch) // tm
    n_mem = k.shape[0] // batch
    row = pl.BlockSpec((tm, d), lambda i: (i, 0))
    kv = pl.BlockSpec((n_mem, d), lambda i: (i // tiles_per_seq, 0))
    return pl.pallas_call(
        _xattn_kernel,
        out_shape=jax.ShapeDtypeStruct((n, d), F32),
        grid=(n // tm,),
        in_specs=[row, _const_spec((1, d)), _const_spec((d, d)), kv, kv, _const_spec((d, d))],
        out_specs=row,
        compiler_params=_cparams("parallel"),
        name="xattn",
    )(h, g, wq, k, v, wo)


FFN_CHUNK = 256


def _gelu_tanh(x):
    return 0.5 * x * (1.0 + jnp.tanh(np.sqrt(2.0 / np.pi) * (x + 0.044715 * (x * x * x))))


def _ffn_kernel(h_ref, hp_ref, hx_ref, g_ref, wup_ref, cw_ref, cb_ref, wdn_ref, o_ref, *, tiles_per_seq):
    i = pl.program_id(0)
    h = h_ref[...]
    tm = h.shape[0]
    dff = wdn_ref.shape[0]
    hn = _rms(h, g_ref[...]).astype(BF16)
    halo = jnp.concatenate([hp_ref[...], hx_ref[...]], axis=0)
    halo_n = _rms(halo, g_ref[...]).astype(BF16)
    hn_ext = jnp.concatenate([hn, halo_n], axis=0)
    has_prev, has_next = _halo_flags(i, tiles_per_seq)

    def up(c):
        a_ext = jnp.dot(hn_ext, wup_ref[:, c:c + FFN_CHUNK], preferred_element_type=F32)
        gate = jnp.dot(hn, wup_ref[:, dff + c:dff + c + FFN_CHUNK], preferred_element_type=F32)
        return a_ext, gate

    chunks = list(range(0, dff, FFN_CHUNK))
    acc = h
    pre = up(chunks[0])
    for n, c in enumerate(chunks):
        nxt = up(chunks[n + 1]) if n + 1 < len(chunks) else None
        a_ext, gate = pre
        ac = _conv3(a_ext, tm, cw_ref[:, c:c + FFN_CHUNK], cb_ref[:, c:c + FFN_CHUNK], has_prev, has_next)
        act = (_gelu_tanh(ac) * gate).astype(BF16)
        acc = acc + jnp.dot(act, wdn_ref[c:c + FFN_CHUNK, :], preferred_element_type=F32)
        pre = nxt
    o_ref[...] = acc


def _ffn(h, g, wup, cw, cb, wdn, batch):
    n, d = h.shape
    tm = ROW_TILE
    tiles_per_seq = (n // batch) // tm
    row = pl.BlockSpec((tm, d), lambda i: (i, 0))
    return pl.pallas_call(
        functools.partial(_ffn_kernel, tiles_per_seq=tiles_per_seq),
        out_shape=jax.ShapeDtypeStruct((n, d), F32),
        grid=(n // tm,),
        in_specs=[row] + _halo_specs(tm, d, n) + [
            _const_spec((1, d)), _const_spec(wup.shape), _const_spec(cw.shape),
            _const_spec(cb.shape), _const_spec(wdn.shape)],
        out_specs=row,
        compiler_params=_cparams("parallel"),
        name="convffn",
    )(h, h, h, g, wup, cw, cb, wdn)


def _norm_kernel(h_ref, g_ref, o_ref):
    o_ref[...] = _rms(h_ref[...], g_ref[...])


def _final_norm(h, g):
    n, d = h.shape
    tm = ROW_TILE
    row = pl.BlockSpec((tm, d), lambda i: (i, 0))
    return pl.pallas_call(
        _norm_kernel,
        out_shape=jax.ShapeDtypeStruct((n, d), F32),
        grid=(n // tm,),
        in_specs=[row, _const_spec((1, d))],
        out_specs=row,
        compiler_params=_cparams("parallel"),
        name="final_norm",
    )(h, g)


def _pad_cols(a, n):
    return jnp.pad(a, ((0, 0), (0, n - a.shape[1])))


def _pad_rows(a, n):
    return jnp.pad(a, ((0, n - a.shape[0]), (0, 0)))


def _prep_w_in(w):
    offs = np.concatenate([[0], np.cumsum(IN_SIZES)])
    seg = [w[:, offs[i]:offs[i + 1]] for i in range(len(IN_SIZES))]
    ml = seg[0:4]
    na = [_pad_cols(s, NA_PAD) for s in seg[5:8]]
    gla = [_pad_cols(seg[8], GLA_K_PAD), _pad_cols(seg[9], GLA_K_PAD),
           _pad_cols(seg[10], GLA_V_PAD), _pad_cols(seg[11], GLA_V_PAD)]
    gates = [_pad_cols(seg[4][:, j * ML_HEADS:(j + 1) * ML_HEADS], SUBLANES) for j in range(4)]
    gates = _pad_cols(jnp.concatenate(gates + [seg[12]], axis=1), GATE_COLS)
    return jnp.concatenate(ml + na + gla + [gates], axis=1).astype(BF16)


def _prep_gate_bias(gb):
    return _pad_cols(jnp.concatenate([_pad_cols(gb[j:j + 1], SUBLANES) for j in range(4)], axis=1), GATE_COLS)


def _prep_gla_gate(w2, ab):
    w = jnp.zeros((GATE_COLS, 2 * GLA_K_PAD), F32)
    b = jnp.zeros((1, 2 * GLA_K_PAD), F32)
    for z in range(2):
        r0 = GLA_GATE_OFF + z * GLA_RANK
        w = w.at[r0:r0 + GLA_RANK, z * GLA_K_PAD:z * GLA_K_PAD + GLA_KDIM].set(w2[z])
        b = b.at[0, z * GLA_K_PAD:z * GLA_K_PAD + GLA_KDIM].set(ab[z])
    return w.astype(BF16), b


def _prep_w_out(w):
    parts = [w[0:ML_DIM], _pad_rows(w[ML_DIM:ML_DIM + NA_DIM], NA_PAD),
             _pad_rows(w[ML_DIM + NA_DIM:], GLA_V_PAD)]
    return jnp.concatenate(parts, axis=0).astype(BF16)


def kernel(x, mem, mix_norm_g, w_in, ml_conv_w, ml_conv_b, ml_gate_b, ml_norm_g, na_rpb,
           gla_a_w2, gla_a_b, gla_norm_g, w_out, xattn_norm_g, mem_norm_g,
           w_xq, w_xk, w_xv, w_xo, ffn_norm_g, w_up, ffn_conv_w, ffn_conv_b, w_down,
           final_norm_g):
    batch, seq, d = x.shape
    depth = w_in.shape[0]
    rows = seq // GRID_W
    assert seq % ROW_TILE == 0 and rows >= NA_KH and rows % NA_UNROLL == 0
    h = x.reshape(batch * seq, d)
    mem2 = mem.reshape(batch * mem.shape[1], d)
    r2 = lambda v: v.reshape(1, -1)
    for l in range(depth):
        ktok, otok, q_t, v_t, gs_t, na, gla, gates = _inproj(
            h, r2(mix_norm_g[l]), _prep_w_in(w_in[l]), ml_conv_w[l], r2(ml_conv_b[l]),
            _prep_gate_bias(ml_gate_b[l]), batch)
        ml_out = _mlstm(ktok, otok, q_t, v_t, gs_t, r2(ml_norm_g[l]), batch)
        na0, na1 = _na(na, _na_bias_table(na_rpb[l], rows), batch)
        w2, ab = _prep_gla_gate(gla_a_w2[l], gla_a_b[l])
        gla_out = _gla(gla, gates, w2, ab, _pad_cols(r2(gla_norm_g[l]), GLA_V_PAD), batch)
        h = _outproj(ml_out, na0, na1, gla_out, _prep_w_out(w_out[l]), h)
        k, v = _memkv(mem2, r2(mem_norm_g[l]), w_xk[l].astype(BF16), w_xv[l].astype(BF16))
        h = _xattn(h, r2(xattn_norm_g[l]), w_xq[l].astype(BF16), k, v, w_xo[l].astype(BF16), batch)
        h = _ffn(h, r2(ffn_norm_g[l]), w_up[l].astype(BF16), ffn_conv_w[l], r2(ffn_conv_b[l]),
                 w_down[l].astype(BF16), batch)
    return _final_norm(h, final_norm_g.reshape(1, d)).reshape(batch, seq, d)
```

```python
import functools

import numpy as np
import jax
import jax.numpy as jnp
from jax import lax
from jax.experimental import pallas as pl
from jax.experimental.pallas import tpu as pltpu

F32 = jnp.float32
BF16 = jnp.bfloat16

HEAD_DIM = 64
ML_HEADS = 6
NA_HEADS = 5
GLA_HEADS = 5
ML_DIM = ML_HEADS * HEAD_DIM
NA_DIM = NA_HEADS * HEAD_DIM
GLA_DK = 32
GLA_KDIM = GLA_HEADS * GLA_DK
GLA_VDIM = GLA_HEADS * HEAD_DIM
GLA_RANK = 16
GLA_TAU = 16.0
GRID_W = 64
NA_KH = 8
NA_KW = 16
ML_CHUNK = 128
GLA_CHUNK = 64
X_HEADS = 4
EPS = 1e-6
IN_SIZES = (ML_DIM, ML_DIM, ML_DIM, ML_DIM, 4 * ML_HEADS,
            NA_DIM, NA_DIM, NA_DIM,
            GLA_KDIM, GLA_KDIM, GLA_VDIM, GLA_VDIM, 2 * GLA_RANK)

LANES = 128
SUBLANES = 8

PAIR_W = 2 * HEAD_DIM
ML_COLS = 4 * ML_DIM
NA_PAD = 3 * PAIR_W
NA_COLS = 3 * NA_PAD
GLA_K_PAD = 256
GLA_V_PAD = 3 * PAIR_W
GLA_COLS = 2 * GLA_K_PAD + 2 * GLA_V_PAD
GATE_COLS = LANES
GLA_GATE_OFF = 4 * SUBLANES
IN_COLS_PAD = ML_COLS + NA_COLS + GLA_COLS + GATE_COLS

ROW_TILE = 512
VMEM_LIMIT = 56 * 1024 * 1024

NT_DIMS = (((1,), (1,)), ((), ()))
TN_DIMS = (((0,), (0,)), ((), ()))


def _cparams(*sem):
    return pltpu.CompilerParams(dimension_semantics=sem, vmem_limit_bytes=VMEM_LIMIT)


def _const_spec(shape):
    nd = len(shape)
    return pl.BlockSpec(shape, lambda *_: (0,) * nd)


def _rms(x, g):
    ms = jnp.mean(x * x, axis=-1, keepdims=True)
    return x * lax.rsqrt(ms + EPS) * g


def _log_sigmoid(x):
    return jnp.minimum(x, 0.0) - jnp.log1p(jnp.exp(-jnp.abs(x)))


def _split_bf16(x):
    hi = x.astype(BF16)
    lo = (x - hi.astype(F32)).astype(BF16)
    return hi, lo


def _group_mean_sq(x, group):
    n = x.shape[-1]
    r = lax.broadcasted_iota(jnp.int32, (n, n), 0) // group
    c = lax.broadcasted_iota(jnp.int32, (n, n), 1) // group
    ones = jnp.where(r == c, 1.0, 0.0).astype(BF16)
    hi, lo = _split_bf16(x * x)
    s = jnp.dot(hi, ones, preferred_element_type=F32) + jnp.dot(lo, ones, preferred_element_type=F32)
    return s * (1.0 / group)


def _halo_flags(i, tiles_per_seq):
    has_prev = (i % tiles_per_seq != 0).astype(F32)
    has_next = (i % tiles_per_seq != tiles_per_seq - 1).astype(F32)
    return has_prev, has_next


def _conv3(a_ext, tm, w, b, has_prev, has_next):
    row = lax.broadcasted_iota(jnp.int32, (tm, 1), 0)
    a = a_ext[0:tm]
    a_prev = jnp.where(row == 0, a_ext[tm + SUBLANES - 1:tm + SUBLANES, :] * has_prev, pltpu.roll(a, 1, 0))
    a_next = jnp.where(row == tm - 1, a_ext[tm + SUBLANES:tm + SUBLANES + 1, :] * has_next,
                       pltpu.roll(a, tm - 1, 0))
    return a_prev * w[0:1, :] + a * w[1:2, :] + a_next * w[2:3, :] + b


def _inproj_kernel(h_ref, hp_ref, hx_ref, g_ref, w_ref, cw_ref, cb_ref, gb_ref,
                   ktok_ref, otok_ref, qT_ref, vT_ref, gsT_ref, na_ref, gla_ref, gate_ref, *, tiles_per_seq):
    tm = h_ref.shape[0]
    L = ML_CHUNK
    has_prev, has_next = _halo_flags(pl.program_id(0), tiles_per_seq)
    xn = _rms(h_ref[...], g_ref[...]).astype(BF16)
    halo = jnp.concatenate([hp_ref[...], hx_ref[...]], axis=0)
    xn_ext = jnp.concatenate([xn, _rms(halo, g_ref[...]).astype(BF16)], axis=0)

    def proj(lhs, c0, c1):
        return jnp.dot(lhs, w_ref[:, c0:c1], preferred_element_type=F32)

    for part in range(2):
        cols = slice(part * ML_DIM, (part + 1) * ML_DIM)
        y = _conv3(proj(xn_ext, cols.start, cols.stop), tm, cw_ref[:, cols], cb_ref[:, cols], has_prev, has_next)
        y = y * jax.nn.sigmoid(y)
        if part == 0:
            y_t = y.T
            for j in range(tm // L):
                qT_ref[j] = y_t[:, j * L:(j + 1) * L].astype(BF16)
        else:
            ktok_ref[...] = (y * HEAD_DIM ** -0.5).astype(BF16)
    v_t = proj(xn, 2 * ML_DIM, 3 * ML_DIM).T
    for j in range(tm // L):
        vT_ref[j] = v_t[:, j * L:(j + 1) * L].astype(BF16)
    otok_ref[...] = proj(xn, 3 * ML_DIM, 4 * ML_DIM).astype(BF16)
    off = ML_COLS
    for ref, chunk in ((na_ref, 384), (gla_ref, 256), (gate_ref, 128)):
        width = ref.shape[-1]
        for c in range(0, width, chunk):
            ref[:, c:c + chunk] = proj(xn, off + c, off + c + chunk).astype(ref.dtype)
        off += width
    g_t = (gate_ref[...] + gb_ref[...]).T
    for j in range(tm // L):
        gsT_ref[j] = g_t[0:4 * SUBLANES, j * L:(j + 1) * L]


def _halo_specs(tm, d, n):
    hb = tm // SUBLANES
    last = n // SUBLANES - 1
    return [pl.BlockSpec((SUBLANES, d), lambda i: (jnp.maximum(i * hb - 1, 0), 0)),
            pl.BlockSpec((SUBLANES, d), lambda i: (jnp.minimum((i + 1) * hb, last), 0))]


def _inproj(h, g, w, cw, cb, gb, batch):
    n, d = h.shape
    tm = ROW_TILE
    nc = tm // ML_CHUNK
    row = lambda cols: pl.BlockSpec((tm, cols), lambda i: (i, 0))
    slab = lambda rows: pl.BlockSpec((nc, rows, ML_CHUNK), lambda i: (i, 0, 0))
    return pl.pallas_call(
        functools.partial(_inproj_kernel, tiles_per_seq=(n // batch) // tm),
        out_shape=(jax.ShapeDtypeStruct((n, ML_DIM), BF16),
                   jax.ShapeDtypeStruct((n, ML_DIM), BF16),
                   jax.ShapeDtypeStruct((n // ML_CHUNK, ML_DIM, ML_CHUNK), BF16),
                   jax.ShapeDtypeStruct((n // ML_CHUNK, ML_DIM, ML_CHUNK), BF16),
                   jax.ShapeDtypeStruct((n // ML_CHUNK, 4 * SUBLANES, ML_CHUNK), F32),
                   jax.ShapeDtypeStruct((n, NA_COLS), BF16),
                   jax.ShapeDtypeStruct((n, GLA_COLS), BF16),
                   jax.ShapeDtypeStruct((n, GATE_COLS), F32)),
        grid=(n // tm,),
        in_specs=[row(d)] + _halo_specs(tm, d, n) + [
            _const_spec((1, d)), _const_spec((d, IN_COLS_PAD)), _const_spec(cw.shape),
            _const_spec(cb.shape), _const_spec(gb.shape)],
        out_specs=(row(ML_DIM), row(ML_DIM), slab(ML_DIM), slab(ML_DIM), slab(4 * SUBLANES),
                   row(NA_COLS), row(GLA_COLS), row(GATE_COLS)),
        compiler_params=_cparams("parallel"),
        name="inproj",
    )(h, h, h, g, w, cw, cb, gb)


def _lane_scan(x, op, fill, reverse):
    n = x.shape[-1]
    lane = lax.broadcasted_iota(jnp.int32, x.shape, 1)
    s = 1
    while s < n:
        if reverse:
            x = op(x, jnp.where(lane < n - s, pltpu.roll(x, n - s, 1), fill))
        else:
            x = op(x, jnp.where(lane >= s, pltpu.roll(x, s, 1), fill))
        s *= 2
    return x


def _mlstm_kernel(ktok_ref, otok_ref, qT_ref, vT_ref, gsT_ref, ng_ref, out_ref,
                  pre_ref, acol_ref, hsumT_ref, ct_ref):
    NC, _, L = qT_ref.shape

    def gate_group(grp):
        return gsT_ref[:, grp * SUBLANES:(grp + 1) * SUBLANES, :].reshape(NC * SUBLANES, L)

    a_all = []
    for d in range(2):
        logf = _log_sigmoid(gate_group(2 * d + 1))
        b = _lane_scan(logf, jnp.add, 0.0, reverse=(d == 1))
        a = gate_group(2 * d) - b
        pre_ref[d, 0] = b
        pre_ref[d, 1] = a
        pre_ref[d, 2] = _lane_scan(a, jnp.maximum, -jnp.inf, reverse=(d == 1))
        pre_ref[d, 3] = jnp.broadcast_to(jnp.sum(logf, axis=1, keepdims=True), a.shape)
        pre_ref[d, 4] = jnp.broadcast_to(jnp.max(a, axis=1, keepdims=True), a.shape)
        a_all.append(a)
    a_pad = jnp.zeros((L - 2 * SUBLANES, L), F32)
    for c in range(NC):
        rows = slice(c * SUBLANES, (c + 1) * SUBLANES)
        acol_ref[c] = jnp.concatenate([a_all[0][rows], a_all[1][rows], a_pad], axis=0).T

    hsumT_ref[...] = jnp.zeros(hsumT_ref.shape, F32)
    ct_ref[...] = jnp.zeros(ct_ref.shape, F32)

    sub = lax.broadcasted_iota(jnp.int32, (PAIR_W, L), 0)
    head_rows = (sub < HEAD_DIM, sub >= HEAD_DIM)
    ri = lax.broadcasted_iota(jnp.int32, (L, L), 0)
    ci = lax.broadcasted_iota(jnp.int32, (L, L), 1)
    tri = (ri <= ci, ri >= ci)

    def body(it, carry):
        m_reps = list(carry)
        chunk = (it, NC - 1 - it)
        gq = []
        for d in range(2):
            r8 = pl.ds(pl.multiple_of(chunk[d] * SUBLANES, SUBLANES), SUBLANES)
            b, a, amax_run, g, amax = (pre_ref[d, n, r8, :] for n in range(5))
            m_old = m_reps[d]
            big_m = jnp.maximum(m_old, amax_run)
            m_new = jnp.maximum(g + m_old, g + amax)
            m_reps[d] = m_new
            gq.append(dict(big_m=big_m, w_inter=jnp.exp(m_old - big_m), e=jnp.exp(-(b + big_m)),
                           wk=jnp.exp(g + a - m_new), decay=jnp.exp(g + m_old - m_new),
                           a_cols=acol_ref[chunk[d]]))

        items = []
        for d in range(2):
            c = chunk[d]
            r0 = pl.multiple_of(c * L, L)
            for p in range(ML_HEADS // 2):
                rows = slice(p * PAIR_W, (p + 1) * PAIR_W)
                k_p = ktok_ref[pl.ds(r0, L), rows]
                q_tp = qT_ref[c, rows, :]
                v_tp = vT_ref[c, rows, :]
                for half in range(2):
                    h = 2 * p + half
                    q_tm = jnp.where(head_rows[half], q_tp, jnp.zeros_like(q_tp))
                    v_ta = jnp.where(head_rows[half], v_tp, jnp.ones_like(v_tp))
                    ct_old = ct_ref[d * ML_HEADS + h]
                    s_t = jnp.dot(k_p, q_tm, preferred_element_type=F32)
                    inter_t = jnp.dot(ct_old.astype(BF16), q_tm, preferred_element_type=F32)
                    v_tw = (v_ta.astype(F32) * gq[d]["wk"][h:h + 1, :]).astype(BF16)
                    kv_t = jnp.dot(v_tw, k_p, preferred_element_type=F32)
                    items.append((d, c, p, half, h, v_ta, ct_old, s_t, inter_t, kv_t))
        s2 = []
        for d, c, p, half, h, v_ta, ct_old, s_t, inter_t, kv_t in items:
            dm = jnp.where(tri[d], gq[d]["a_cols"][:, SUBLANES * d + h:SUBLANES * d + h + 1]
                           - gq[d]["big_m"][h:h + 1, :], -jnp.inf)
            s2.append((s_t * jnp.exp(dm)).astype(BF16))
        num = [jnp.dot(it_[5], s, preferred_element_type=F32) for it_, s in zip(items, s2)]
        for idx in range(0, len(items), 2):
            outs = []
            for (d, c, p, half, h, v_ta, ct_old, s_t, inter_t, kv_t), n_t in zip(items[idx:idx + 2],
                                                                                 num[idx:idx + 2]):
                n_t = n_t + gq[d]["w_inter"][h:h + 1, :] * inter_t
                den = jnp.concatenate([n_t[HEAD_DIM:], n_t[:HEAD_DIM]], axis=0)
                outs.append(n_t / jnp.maximum(jnp.abs(den), gq[d]["e"][h:h + 1, :]))
                ct_ref[d * ML_HEADS + h] = gq[d]["decay"][h:h + 1, :] * ct_old + kv_t
            hsumT_ref[c, p * PAIR_W:(p + 1) * PAIR_W, :] += jnp.where(head_rows[0], outs[0], outs[1])
        return tuple(m_reps)

    m0 = jnp.zeros((SUBLANES, L), F32)
    lax.fori_loop(0, NC, body, (m0, m0))

    for c in range(NC):
        hs_t = hsumT_ref[c]
        parts = []
        for h in range(ML_HEADS):
            blk = hs_t[h * HEAD_DIM:(h + 1) * HEAD_DIM, :]
            ms = jnp.mean(blk * blk, axis=0, keepdims=True)
            parts.append(blk * lax.rsqrt(ms + EPS))
        y = jnp.concatenate(parts, axis=0).T * ng_ref[...]
        o = otok_ref[c * L:(c + 1) * L, :].astype(F32)
        out_ref[c * L:(c + 1) * L, :] = (y * jax.nn.sigmoid(o)).astype(BF16)


def _mlstm(ktok, otok, q_t, v_t, gs_t, ng, batch):
    n = ktok.shape[0]
    T = n // batch
    nc = T // ML_CHUNK
    tok = pl.BlockSpec((T, ML_DIM), lambda b: (b, 0))
    slab = lambda rows: pl.BlockSpec((nc, rows, ML_CHUNK), lambda b: (b, 0, 0))
    return pl.pallas_call(
        _mlstm_kernel,
        out_shape=jax.ShapeDtypeStruct((n, ML_DIM), BF16),
        grid=(batch,),
        in_specs=[tok, tok, slab(ML_DIM), slab(ML_DIM), slab(4 * SUBLANES), _const_spec((1, ML_DIM))],
        out_specs=tok,
        scratch_shapes=[pltpu.VMEM((2, 5, nc * SUBLANES, ML_CHUNK), F32),
                        pltpu.VMEM((nc, ML_CHUNK, LANES), F32),
                        pltpu.VMEM((nc, ML_DIM, ML_CHUNK), F32),
                        pltpu.VMEM((2 * ML_HEADS, PAIR_W, PAIR_W), F32)],
        compiler_params=_cparams("parallel"),
        name="mlstm",
    )(ktok, otok, q_t, v_t, gs_t, ng)


def _gla_kernel(x_ref, gates_ref, w2_ref, ab_ref, ng_ref, out_ref, bc_ref, osum_ref, st_ref):
    T = x_ref.shape[0]
    L = GLA_CHUNK
    NC = T // L
    KP, VP = GLA_K_PAD, GLA_V_PAD
    RB = 256

    ri = lax.broadcasted_iota(jnp.int32, (RB, RB), 0)
    ci = lax.broadcasted_iota(jnp.int32, (RB, RB), 1)
    same = (ri // L) == (ci // L)
    tri_blk = (jnp.where(same & (ci <= ri), 1.0, 0.0).astype(BF16),
               jnp.where(same & (ci >= ri), 1.0, 0.0).astype(BF16))
    for r0 in range(0, T, RB):
        z = jnp.dot(gates_ref[r0:r0 + RB, :].astype(BF16), w2_ref[...],
                    preferred_element_type=F32) + ab_ref[...]
        hi, lo = _split_bf16(_log_sigmoid(z) * (1.0 / GLA_TAU))
        for d in range(2):
            sl = slice(d * KP, (d + 1) * KP)
            bc_ref[r0:r0 + RB, sl] = (jnp.dot(tri_blk[d], hi[:, sl], preferred_element_type=F32)
                                      + jnp.dot(tri_blk[d], lo[:, sl], preferred_element_type=F32))

    osum_ref[...] = jnp.zeros(osum_ref.shape, F32)
    st_ref[...] = jnp.zeros(st_ref.shape, F32)

    ri = lax.broadcasted_iota(jnp.int32, (L, L), 0)
    ci = lax.broadcasted_iota(jnp.int32, (L, L), 1)
    tri = (ci <= ri, ci >= ri)
    qhead = lax.broadcasted_iota(jnp.int32, (1, KP), 1) // GLA_DK
    pair_first = lax.broadcasted_iota(jnp.int32, (1, PAIR_W), 1) < HEAD_DIM
    st_mask = (lax.broadcasted_iota(jnp.int32, (VP, KP), 0) // HEAD_DIM
               == lax.broadcasted_iota(jnp.int32, (VP, KP), 1) // GLA_DK)

    def body(it, carry):
        chunk = (it, NC - 1 - it)
        w = []
        for d in range(2):
            r0 = pl.multiple_of(chunk[d] * L, L)
            bc = bc_ref[pl.ds(r0, L), d * KP:(d + 1) * KP]
            btot = bc[L - 1:L, :] if d == 0 else bc[0:1, :]
            q = x_ref[pl.ds(r0, L), 0:KP].astype(F32) * (GLA_DK ** -0.5)
            k = x_ref[pl.ds(r0, L), KP:2 * KP].astype(F32)
            v = x_ref[pl.ds(r0, L), 2 * KP:2 * KP + VP]
            q_dec = q * jnp.exp(bc)
            k_dec = (k * jnp.exp(-bc)).astype(BF16)
            k_tail = (k * jnp.exp(btot - bc)).astype(BF16)
            st_old = st_ref[d]
            o = lax.dot_general(q_dec.astype(BF16), st_old.astype(BF16), NT_DIMS,
                                preferred_element_type=F32)
            a = [lax.dot_general(jnp.where(qhead == h, q_dec, 0.0).astype(BF16), k_dec, NT_DIMS,
                                 preferred_element_type=F32) for h in range(GLA_HEADS)]
            kv = lax.dot_general(v, k_tail, TN_DIMS, preferred_element_type=F32)
            w.append((r0, btot, v, st_old, o, a, kv))
        ab = [[jnp.where(tri[d], a_h, 0.0).astype(BF16) for a_h in w[d][5]] for d in range(2)]
        oh = [[jnp.dot(a_h, w[d][2][:, (h // 2) * PAIR_W:(h // 2 + 1) * PAIR_W], preferred_element_type=F32)
               for h, a_h in enumerate(ab[d])] for d in range(2)]
        for d in range(2):
            r0, btot, v, st_old, o, a, kv = w[d]
            blocks = []
            for p in range(VP // PAIR_W):
                heads = [h for h in range(GLA_HEADS) if h // 2 == p]
                blk = jnp.where(pair_first, oh[d][heads[0]], oh[d][heads[1]] if len(heads) > 1 else 0.0)
                blocks.append(blk)
            st_ref[d] = st_old * jnp.exp(btot) + jnp.where(st_mask, kv, 0.0)
            osum_ref[pl.ds(r0, L), :] += o + jnp.concatenate(blocks, axis=1)
        return carry

    lax.fori_loop(0, NC, body, 0)

    for r0 in range(0, T, RB):
        hs = osum_ref[r0:r0 + RB, :]
        y = hs * lax.rsqrt(_group_mean_sq(hs, HEAD_DIM) + EPS) * ng_ref[...]
        g = x_ref[r0:r0 + RB, 2 * KP + VP:2 * KP + 2 * VP].astype(F32)
        out_ref[r0:r0 + RB, :] = (y * (g * jax.nn.sigmoid(g))).astype(BF16)


def _gla(x, gates, w2, ab, ng, batch):
    n = x.shape[0]
    T = n // batch
    return pl.pallas_call(
        _gla_kernel,
        out_shape=jax.ShapeDtypeStruct((n, GLA_V_PAD), BF16),
        grid=(batch,),
        in_specs=[pl.BlockSpec((T, GLA_COLS), lambda b: (b, 0)),
                  pl.BlockSpec((T, GATE_COLS), lambda b: (b, 0)),
                  _const_spec((GATE_COLS, 2 * GLA_K_PAD)), _const_spec((1, 2 * GLA_K_PAD)),
                  _const_spec((1, GLA_V_PAD))],
        out_specs=pl.BlockSpec((T, GLA_V_PAD), lambda b: (b, 0)),
        scratch_shapes=[pltpu.VMEM((T, 2 * GLA_K_PAD), F32),
                        pltpu.VMEM((T, GLA_V_PAD), F32),
                        pltpu.VMEM((2, GLA_V_PAD, GLA_K_PAD), F32)],
        compiler_params=_cparams("parallel"),
        name="gla",
    )(x, gates, w2, ab, ng)


NA_CLASSES = 8
NA_UNROLL = 8


def _na_kernel(q_ref, k_ref, v_ref, bias_ref, out_ref, *, n_heads_in_pair):
    T = q_ref.shape[0]
    W = GRID_W
    rows = T // W
    band = NA_KH * W
    lane = lax.broadcasted_iota(jnp.int32, (1, PAIR_W), 1)
    head_mask = (lane < HEAD_DIM, lane >= HEAD_DIM)
    scale = HEAD_DIM ** -0.5

    def group(gi, carry):
        items = []
        for u in range(NA_UNROLL):
            r = gi * NA_UNROLL + u
            rs = jnp.clip(r - NA_KH // 2, 0, rows - NA_KH)
            cls = jnp.where(r < NA_KH // 2, r, jnp.where(r > rows - NA_KH // 2, r - (rows - NA_KH), NA_KH // 2))
            q0 = pl.multiple_of(r * W, W)
            k0 = pl.multiple_of(rs * W, W)
            q = q_ref[pl.ds(q0, W), :]
            kb = k_ref[pl.ds(k0, band), :]
            for half in range(n_heads_in_pair):
                q_m = jnp.where(head_mask[half], q, jnp.zeros_like(q))
                s = lax.dot_general(q_m, kb, NT_DIMS, preferred_element_type=F32)
                items.append((u, half, q0, k0, cls, s))
        probs = []
        for u, half, q0, k0, cls, s in items:
            s = s * scale + bias_ref[half, cls]
            m = jnp.max(s, axis=1, keepdims=True)
            p = jnp.exp(s - m)
            l = jnp.sum(p, axis=1, keepdims=True)
            probs.append((u, half, q0, k0, p.astype(BF16), l))
        outs = {}
        for u, half, q0, k0, p, l in probs:
            vb = v_ref[pl.ds(k0, band), :]
            outs[(u, half)] = (q0, jnp.dot(p, vb, preferred_element_type=F32) / l)
        for u in range(NA_UNROLL):
            q0, o0 = outs[(u, 0)]
            second = outs[(u, 1)][1] if n_heads_in_pair == 2 else 0.0
            out_ref[pl.ds(q0, W), :] = jnp.where(head_mask[0], o0, second).astype(BF16)
        return carry

    lax.fori_loop(0, rows // NA_UNROLL, group, 0)


def _na(na, bias, batch):
    n = na.shape[0]
    T = n // batch
    npairs = NA_PAD // PAIR_W
    outs = []
    for p0, p1, nh in ((0, NA_HEADS // 2, 2), (NA_HEADS // 2, npairs, 1)):
        npp = p1 - p0
        outs.append(pl.pallas_call(
            functools.partial(_na_kernel, n_heads_in_pair=nh),
            out_shape=jax.ShapeDtypeStruct((n, npp * PAIR_W), BF16),
            grid=(npp, batch),
            in_specs=[pl.BlockSpec((T, PAIR_W), lambda p, b, o=p0: (b, o + p)),
                      pl.BlockSpec((T, PAIR_W), lambda p, b, o=npairs + p0: (b, o + p)),
                      pl.BlockSpec((T, PAIR_W), lambda p, b, o=2 * npairs + p0: (b, o + p)),
                      pl.BlockSpec((2, NA_CLASSES, GRID_W, NA_KH * GRID_W),
                                   lambda p, b, o=p0: (o + p, 0, 0, 0))],
            out_specs=pl.BlockSpec((T, PAIR_W), lambda p, b: (b, p)),
            compiler_params=_cparams("parallel", "parallel"),
            name=f"natten{nh}",
        )(na, na, na, bias))
    return outs


def _na_bias_table(rpb, rows):
    kh = NA_KH
    rep_rows = np.array(list(range(kh // 2)) + [kh // 2] + list(range(rows - kh // 2 + 1, rows)))
    row_start = np.clip(rep_rows - kh // 2, 0, rows - kh)
    dr = row_start[:, None] + np.arange(kh)[None, :] - rep_rows[:, None] + (kh - 1)
    c = np.arange(GRID_W)
    col_start = np.clip(c - NA_KW // 2, 0, GRID_W - NA_KW)
    in_win = (c[None, :] >= col_start[:, None]) & (c[None, :] < col_start[:, None] + NA_KW)
    dc = np.clip(c[None, :] - c[:, None], -(NA_KW - 1), NA_KW - 1) + (NA_KW - 1)
    sel_r = jnp.asarray(np.eye(2 * kh - 1, dtype=np.float32)[dr])
    sel_c = jnp.asarray(np.eye(2 * NA_KW - 1, dtype=np.float32)[dc])
    tab = jnp.einsum('skr,hrc->hskc', sel_r, rpb.astype(F32), precision=lax.Precision.HIGHEST)
    tab = jnp.einsum('hskc,qwc->hsqkw', tab, sel_c, precision=lax.Precision.HIGHEST)
    tab = jnp.where(in_win[None, None, :, None, :], tab, -jnp.inf)
    tab = tab.reshape(rpb.shape[0], len(rep_rows), GRID_W, kh * GRID_W)
    return jnp.pad(tab, ((0, 2 * (NA_PAD // PAIR_W) - rpb.shape[0]), (0, 0), (0, 0), (0, 0)))


def _memkv_kernel(m_ref, g_ref, wk_ref, wv_ref, k_ref, v_ref):
    mn = _rms(m_ref[...], g_ref[...]).astype(BF16)
    k_ref[...] = jnp.dot(mn, wk_ref[...], preferred_element_type=F32).astype(BF16)
    v_ref[...] = jnp.dot(mn, wv_ref[...], preferred_element_type=F32).astype(BF16)


def _memkv(mem, g, wk, wv):
    n, d = mem.shape
    tm = min(ROW_TILE, n)
    row = pl.BlockSpec((tm, d), lambda i: (i, 0))
    return pl.pallas_call(
        _memkv_kernel,
        out_shape=(jax.ShapeDtypeStruct((n, d), BF16), jax.ShapeDtypeStruct((n, d), BF16)),
        grid=(n // tm,),
        in_specs=[row, _const_spec((1, d)), _const_spec((d, d)), _const_spec((d, d))],
        out_specs=(row, row),
        compiler_params=_cparams("parallel"),
        name="memkv",
    )(mem, g, wk, wv)


def _xattn_kernel(ml_ref, na0_ref, na1_ref, gla_ref, wout_ref, h_ref, g_ref, wq_ref, k_ref, v_ref, wo_ref,
                  o_ref):
    mix = None
    off = 0
    for ref in (ml_ref, na0_ref, na1_ref, gla_ref):
        width = ref.shape[-1]
        part = jnp.dot(ref[...], wout_ref[off:off + width, :], preferred_element_type=F32)
        mix = part if mix is None else mix + part
        off += width
    o_ref[...] = h_ref[...] + mix
    h = o_ref[...]
    d = h.shape[-1]
    hd = d // X_HEADS
    hn = _rms(h, g_ref[...]).astype(BF16)
    q = jnp.dot(hn, wq_ref[...], preferred_element_type=F32).astype(BF16)
    heads = [slice(i * hd, (i + 1) * hd) for i in range(X_HEADS)]
    scores = [lax.dot_general(q[:, sl], k_ref[:, sl], NT_DIMS, preferred_element_type=F32) for sl in heads]
    probs = []
    for s in scores:
        s = s * (hd ** -0.5)
        p = jnp.exp(s - jnp.max(s, axis=1, keepdims=True))
        probs.append((p.astype(BF16), jnp.sum(p, axis=1, keepdims=True)))
    ctx = [jnp.dot(p, v_ref[:, sl], preferred_element_type=F32) for (p, _), sl in zip(probs, heads)]
    acc = h
    for o, (_, l), sl in zip(ctx, probs, heads):
        acc = acc + jnp.dot((o / l).astype(BF16), wo_ref[sl, :], preferred_element_type=F32)
    o_ref[...] = acc


def _xattn(mixed, wout, h, g, wq, k, v, wo, batch):
    n, d = h.shape
    tm = ROW_TILE
    tiles_per_seq = (n // batch) // tm
    n_mem = k.shape[0] // batch
    row = lambda cols: pl.BlockSpec((tm, cols), lambda i: (i, 0))
    kv = pl.BlockSpec((n_mem, d), lambda i: (i // tiles_per_seq, 0))
    return pl.pallas_call(
        _xattn_kernel,
        out_shape=jax.ShapeDtypeStruct((n, d), F32),
        grid=(n // tm,),
        in_specs=[row(m.shape[1]) for m in mixed] + [
            _const_spec(wout.shape), row(d), _const_spec((1, d)), _const_spec((d, d)), kv, kv,
            _const_spec((d, d))],
        out_specs=row(d),
        compiler_params=_cparams("parallel"),
        name="xattn",
    )(*mixed, wout, h, g, wq, k, v, wo)


FFN_CHUNK = 256
FFN_AHEAD = 2


def _gelu_tanh(x):
    return 0.5 * x * (1.0 + jnp.tanh(np.sqrt(2.0 / np.pi) * (x + 0.044715 * (x * x * x))))


def _ffn_kernel(h_ref, hp_ref, hx_ref, g_ref, wup_ref, cw_ref, cb_ref, wdn_ref, gf_ref, o_ref, *,
                tiles_per_seq, final_norm):
    i = pl.program_id(0)
    h = h_ref[...]
    tm = h.shape[0]
    dff = wdn_ref.shape[0]
    hn = _rms(h, g_ref[...]).astype(BF16)
    halo = jnp.concatenate([hp_ref[...], hx_ref[...]], axis=0)
    halo_n = _rms(halo, g_ref[...]).astype(BF16)
    hn_ext = jnp.concatenate([hn, halo_n], axis=0)
    has_prev, has_next = _halo_flags(i, tiles_per_seq)

    def up(c):
        a_ext = jnp.dot(hn_ext, wup_ref[:, c], preferred_element_type=F32)
        gate = jnp.dot(hn, wup_ref[:, slice(dff + c.start, dff + c.stop)], preferred_element_type=F32)
        return a_ext, gate

    chunks = [slice(c, min(c + FFN_CHUNK, dff)) for c in range(0, dff, FFN_CHUNK)]
    acc = h
    ups = [up(c) for c in chunks[:FFN_AHEAD]]
    for n, c in enumerate(chunks):
        if n + FFN_AHEAD < len(chunks):
            ups.append(up(chunks[n + FFN_AHEAD]))
        a_ext, gate = ups[n]
        ac = _conv3(a_ext, tm, cw_ref[:, c], cb_ref[:, c], has_prev, has_next)
        act = (_gelu_tanh(ac) * gate).astype(BF16)
        acc = acc + jnp.dot(act, wdn_ref[c, :], preferred_element_type=F32)
    o_ref[...] = _rms(acc, gf_ref[...]) if final_norm else acc


def _ffn(h, g, wup, cw, cb, wdn, gf, batch, final_norm):
    n, d = h.shape
    tm = ROW_TILE
    tiles_per_seq = (n // batch) // tm
    row = pl.BlockSpec((tm, d), lambda i: (i, 0))
    return pl.pallas_call(
        functools.partial(_ffn_kernel, tiles_per_seq=tiles_per_seq, final_norm=final_norm),
        out_shape=jax.ShapeDtypeStruct((n, d), F32),
        grid=(n // tm,),
        in_specs=[row] + _halo_specs(tm, d, n) + [
            _const_spec((1, d)), _const_spec(wup.shape), _const_spec(cw.shape),
            _const_spec(cb.shape), _const_spec(wdn.shape), _const_spec((1, d))],
        out_specs=row,
        compiler_params=_cparams("parallel"),
        name="convffn",
    )(h, h, h, g, wup, cw, cb, wdn, gf)


def _pad_cols(a, n):
    return jnp.pad(a, ((0, 0), (0, n - a.shape[1])))


def _pad_rows(a, n):
    return jnp.pad(a, ((0, n - a.shape[0]), (0, 0)))


def _prep_w_in(w):
    offs = np.concatenate([[0], np.cumsum(IN_SIZES)])
    seg = [w[:, offs[i]:offs[i + 1]] for i in range(len(IN_SIZES))]
    ml = seg[0:4]
    na = [_pad_cols(s, NA_PAD) for s in seg[5:8]]
    gla = [_pad_cols(seg[8], GLA_K_PAD), _pad_cols(seg[9], GLA_K_PAD),
           _pad_cols(seg[10], GLA_V_PAD), _pad_cols(seg[11], GLA_V_PAD)]
    gates = [_pad_cols(seg[4][:, j * ML_HEADS:(j + 1) * ML_HEADS], SUBLANES) for j in range(4)]
    gates = _pad_cols(jnp.concatenate(gates + [seg[12]], axis=1), GATE_COLS)
    return jnp.concatenate(ml + na + gla + [gates], axis=1).astype(BF16)


def _prep_gate_bias(gb):
    return _pad_cols(jnp.concatenate([_pad_cols(gb[j:j + 1], SUBLANES) for j in range(4)], axis=1), GATE_COLS)


def _prep_gla_gate(w2, ab):
    w = jnp.zeros((GATE_COLS, 2 * GLA_K_PAD), F32)
    b = jnp.zeros((1, 2 * GLA_K_PAD), F32)
    for z in range(2):
        r0 = GLA_GATE_OFF + z * GLA_RANK
        w = w.at[r0:r0 + GLA_RANK, z * GLA_K_PAD:z * GLA_K_PAD + GLA_KDIM].set(w2[z])
        b = b.at[0, z * GLA_K_PAD:z * GLA_K_PAD + GLA_KDIM].set(ab[z])
    return w.astype(BF16), b


def _prep_w_out(w):
    parts = [w[0:ML_DIM], _pad_rows(w[ML_DIM:ML_DIM + NA_DIM], NA_PAD),
             _pad_rows(w[ML_DIM + NA_DIM:], GLA_V_PAD)]
    return jnp.concatenate(parts, axis=0).astype(BF16)


def kernel(x, mem, mix_norm_g, w_in, ml_conv_w, ml_conv_b, ml_gate_b, ml_norm_g, na_rpb,
           gla_a_w2, gla_a_b, gla_norm_g, w_out, xattn_norm_g, mem_norm_g,
           w_xq, w_xk, w_xv, w_xo, ffn_norm_g, w_up, ffn_conv_w, ffn_conv_b, w_down,
           final_norm_g):
    batch, seq, d = x.shape
    depth = w_in.shape[0]
    rows = seq // GRID_W
    assert seq % ROW_TILE == 0 and rows >= NA_KH and rows % NA_UNROLL == 0
    h = x.reshape(batch * seq, d)
    mem2 = mem.reshape(batch * mem.shape[1], d)
    r2 = lambda v: v.reshape(1, -1)
    for l in range(depth):
        ktok, otok, q_t, v_t, gs_t, na, gla, gates = _inproj(
            h, r2(mix_norm_g[l]), _prep_w_in(w_in[l]), ml_conv_w[l], r2(ml_conv_b[l]),
            _prep_gate_bias(ml_gate_b[l]), batch)
        ml_out = _mlstm(ktok, otok, q_t, v_t, gs_t, r2(ml_norm_g[l]), batch)
        na0, na1 = _na(na, _na_bias_table(na_rpb[l], rows), batch)
        w2, ab = _prep_gla_gate(gla_a_w2[l], gla_a_b[l])
        gla_out = _gla(gla, gates, w2, ab, _pad_cols(r2(gla_norm_g[l]), GLA_V_PAD), batch)
        k, v = _memkv(mem2, r2(mem_norm_g[l]), w_xk[l].astype(BF16), w_xv[l].astype(BF16))
        h = _xattn((ml_out, na0, na1, gla_out), _prep_w_out(w_out[l]), h, r2(xattn_norm_g[l]),
                   w_xq[l].astype(BF16), k, v, w_xo[l].astype(BF16), batch)
        h = _ffn(h, r2(ffn_norm_g[l]), w_up[l].astype(BF16), ffn_conv_w[l], r2(ffn_conv_b[l]),
                 w_down[l].astype(BF16), r2(final_norm_g), batch, final_norm=(l == depth - 1))
    return h.reshape(batch, seq, d)
```

```python
import functools

import numpy as np
import jax
import jax.numpy as jnp
from jax import lax
from jax.experimental import pallas as pl
from jax.experimental.pallas import tpu as pltpu

F32 = jnp.float32
BF16 = jnp.bfloat16

HEAD_DIM = 64
ML_HEADS = 6
NA_HEADS = 5
GLA_HEADS = 5
ML_DIM = ML_HEADS * HEAD_DIM
NA_DIM = NA_HEADS * HEAD_DIM
GLA_DK = 32
GLA_KDIM = GLA_HEADS * GLA_DK
GLA_VDIM = GLA_HEADS * HEAD_DIM
GLA_RANK = 16
GLA_TAU = 16.0
GRID_W = 64
NA_KH = 8
NA_KW = 16
ML_CHUNK = 128
GLA_CHUNK = 64
X_HEADS = 4
EPS = 1e-6
IN_SIZES = (ML_DIM, ML_DIM, ML_DIM, ML_DIM, 4 * ML_HEADS,
            NA_DIM, NA_DIM, NA_DIM,
            GLA_KDIM, GLA_KDIM, GLA_VDIM, GLA_VDIM, 2 * GLA_RANK)

LANES = 128
SUBLANES = 8

PAIR_W = 2 * HEAD_DIM
ML_COLS = 4 * ML_DIM
NA_PAD = 3 * PAIR_W
NA_COLS = 3 * NA_PAD
GLA_K_PAD = 256
GLA_V_PAD = 3 * PAIR_W
GLA_COLS = 2 * GLA_K_PAD + 2 * GLA_V_PAD
GATE_COLS = LANES
GLA_GATE_OFF = 4 * SUBLANES
IN_COLS_PAD = ML_COLS + NA_COLS + GLA_COLS + GATE_COLS

ROW_TILE = 512
NORM_BLOCK = 128
VMEM_LIMIT = 56 * 1024 * 1024

NT_DIMS = (((1,), (1,)), ((), ()))
TN_DIMS = (((0,), (0,)), ((), ()))


def _cparams(*sem):
    return pltpu.CompilerParams(dimension_semantics=sem, vmem_limit_bytes=VMEM_LIMIT)


def _const_spec(shape):
    nd = len(shape)
    return pl.BlockSpec(shape, lambda *_: (0,) * nd)


def _rms(x, g):
    ms = jnp.mean(x * x, axis=-1, keepdims=True)
    return x * lax.rsqrt(ms + EPS) * g


def _rms_row_blocks(x_ref, g, tm):
    return [_rms(x_ref[r0:r0 + NORM_BLOCK, :], g).astype(BF16) for r0 in range(0, tm, NORM_BLOCK)]


def _dot_row_blocks(blocks, w):
    return jnp.concatenate([jnp.dot(b, w, preferred_element_type=F32) for b in blocks], axis=0)


def _log_sigmoid(x):
    return jnp.minimum(x, 0.0) - jnp.log1p(jnp.exp(-jnp.abs(x)))


def _split_bf16(x):
    hi = x.astype(BF16)
    lo = (x - hi.astype(F32)).astype(BF16)
    return hi, lo


def _group_mean_sq(x, group):
    n = x.shape[-1]
    r = lax.broadcasted_iota(jnp.int32, (n, n), 0) // group
    c = lax.broadcasted_iota(jnp.int32, (n, n), 1) // group
    ones = jnp.where(r == c, 1.0, 0.0).astype(BF16)
    hi, lo = _split_bf16(x * x)
    s = jnp.dot(hi, ones, preferred_element_type=F32) + jnp.dot(lo, ones, preferred_element_type=F32)
    return s * (1.0 / group)


def _halo_flags(i, tiles_per_seq):
    has_prev = (i % tiles_per_seq != 0).astype(F32)
    has_next = (i % tiles_per_seq != tiles_per_seq - 1).astype(F32)
    return has_prev, has_next


def _conv3(a_ext, tm, w, b, has_prev, has_next):
    row = lax.broadcasted_iota(jnp.int32, (SUBLANES, 1), 0)
    a = a_ext[0:tm]
    down = pltpu.roll(a, 1, 0)
    up = pltpu.roll(a, tm - 1, 0)
    prev_row = a_ext[tm + SUBLANES - 1:tm + SUBLANES, :] * has_prev
    next_row = a_ext[tm + SUBLANES:tm + SUBLANES + 1, :] * has_next
    a_prev = jnp.concatenate([jnp.where(row == 0, prev_row, down[0:SUBLANES]), down[SUBLANES:]], axis=0)
    a_next = jnp.concatenate([up[:tm - SUBLANES],
                              jnp.where(row == SUBLANES - 1, next_row, up[tm - SUBLANES:])], axis=0)
    return a_prev * w[0:1, :] + a * w[1:2, :] + a_next * w[2:3, :] + b


def _inproj_kernel(h_ref, hp_ref, hx_ref, g_ref, w_ref, cw_ref, cb_ref, gb_ref,
                   ktok_ref, otok_ref, qT_ref, vT_ref, gsT_ref, na_ref, gla_ref, gate_ref, *, tiles_per_seq):
    tm = h_ref.shape[0]
    L = ML_CHUNK
    has_prev, has_next = _halo_flags(pl.program_id(0), tiles_per_seq)
    halo = jnp.concatenate([hp_ref[...], hx_ref[...]], axis=0)
    blocks = _rms_row_blocks(h_ref, g_ref[...], tm) + [_rms(halo, g_ref[...]).astype(BF16)]
    xn = jnp.concatenate(blocks[:-1], axis=0)
    xn_ext = jnp.concatenate(blocks, axis=0)

    def proj(lhs, c0, c1):
        return jnp.dot(lhs, w_ref[:, c0:c1], preferred_element_type=F32)

    for part in range(2):
        cols = slice(part * ML_DIM, (part + 1) * ML_DIM)
        a_ext = _dot_row_blocks(blocks, w_ref[:, cols]) if part == 0 else proj(xn_ext, cols.start, cols.stop)
        y = _conv3(a_ext, tm, cw_ref[:, cols], cb_ref[:, cols], has_prev, has_next)
        y = y * jax.nn.sigmoid(y)
        if part == 0:
            y_t = y.T
            for j in range(tm // L):
                qT_ref[j] = y_t[:, j * L:(j + 1) * L].astype(BF16)
        else:
            ktok_ref[...] = (y * HEAD_DIM ** -0.5).astype(BF16)
    v_t = proj(xn, 2 * ML_DIM, 3 * ML_DIM).T
    for j in range(tm // L):
        vT_ref[j] = v_t[:, j * L:(j + 1) * L].astype(BF16)
    otok_ref[...] = proj(xn, 3 * ML_DIM, 4 * ML_DIM).astype(BF16)
    off = ML_COLS
    for ref, chunk in ((na_ref, 384), (gla_ref, 256), (gate_ref, 128)):
        width = ref.shape[-1]
        for c in range(0, width, chunk):
            ref[:, c:c + chunk] = proj(xn, off + c, off + c + chunk).astype(ref.dtype)
        off += width
    g_t = (gate_ref[...] + gb_ref[...]).T
    for j in range(tm // L):
        gsT_ref[j] = g_t[0:4 * SUBLANES, j * L:(j + 1) * L]


def _halo_specs(tm, d, n):
    hb = tm // SUBLANES
    last = n // SUBLANES - 1
    return [pl.BlockSpec((SUBLANES, d), lambda i: (jnp.maximum(i * hb - 1, 0), 0)),
            pl.BlockSpec((SUBLANES, d), lambda i: (jnp.minimum((i + 1) * hb, last), 0))]


def _inproj(h, g, w, cw, cb, gb, batch):
    n, d = h.shape
    tm = ROW_TILE
    nc = tm // ML_CHUNK
    row = lambda cols: pl.BlockSpec((tm, cols), lambda i: (i, 0))
    slab = lambda rows: pl.BlockSpec((nc, rows, ML_CHUNK), lambda i: (i, 0, 0))
    return pl.pallas_call(
        functools.partial(_inproj_kernel, tiles_per_seq=(n // batch) // tm),
        out_shape=(jax.ShapeDtypeStruct((n, ML_DIM), BF16),
                   jax.ShapeDtypeStruct((n, ML_DIM), BF16),
                   jax.ShapeDtypeStruct((n // ML_CHUNK, ML_DIM, ML_CHUNK), BF16),
                   jax.ShapeDtypeStruct((n // ML_CHUNK, ML_DIM, ML_CHUNK), BF16),
                   jax.ShapeDtypeStruct((n // ML_CHUNK, 4 * SUBLANES, ML_CHUNK), F32),
                   jax.ShapeDtypeStruct((n, NA_COLS), BF16),
                   jax.ShapeDtypeStruct((n, GLA_COLS), BF16),
                   jax.ShapeDtypeStruct((n, GATE_COLS), F32)),
        grid=(n // tm,),
        in_specs=[row(d)] + _halo_specs(tm, d, n) + [
            _const_spec((1, d)), _const_spec((d, IN_COLS_PAD)), _const_spec(cw.shape),
            _const_spec(cb.shape), _const_spec(gb.shape)],
        out_specs=(row(ML_DIM), row(ML_DIM), slab(ML_DIM), slab(ML_DIM), slab(4 * SUBLANES),
                   row(NA_COLS), row(GLA_COLS), row(GATE_COLS)),
        compiler_params=_cparams("parallel"),
        name="inproj",
    )(h, h, h, g, w, cw, cb, gb)


def _lane_scan(x, op, fill, reverse):
    n = x.shape[-1]
    lane = lax.broadcasted_iota(jnp.int32, x.shape, 1)
    s = 1
    while s < n:
        if reverse:
            x = op(x, jnp.where(lane < n - s, pltpu.roll(x, n - s, 1), fill))
        else:
            x = op(x, jnp.where(lane >= s, pltpu.roll(x, s, 1), fill))
        s *= 2
    return x


def _mlstm_kernel(ktok_ref, otok_ref, qT_ref, vT_ref, gsT_ref, ng_ref, out_ref,
                  pre_ref, acol_ref, hsumT_ref, ct_ref):
    NC, _, L = qT_ref.shape

    def gate_group(grp):
        return gsT_ref[:, grp * SUBLANES:(grp + 1) * SUBLANES, :].reshape(NC * SUBLANES, L)

    a_all = []
    for d in range(2):
        logf = _log_sigmoid(gate_group(2 * d + 1))
        b = _lane_scan(logf, jnp.add, 0.0, reverse=(d == 1))
        a = gate_group(2 * d) - b
        pre_ref[d, 0] = b
        pre_ref[d, 1] = a
        pre_ref[d, 2] = _lane_scan(a, jnp.maximum, -jnp.inf, reverse=(d == 1))
        pre_ref[d, 3] = jnp.broadcast_to(jnp.sum(logf, axis=1, keepdims=True), a.shape)
        pre_ref[d, 4] = jnp.broadcast_to(jnp.max(a, axis=1, keepdims=True), a.shape)
        a_all.append(a)
    a_pad = jnp.zeros((L - 2 * SUBLANES, L), F32)
    for c in range(NC):
        rows = slice(c * SUBLANES, (c + 1) * SUBLANES)
        acol_ref[c] = jnp.concatenate([a_all[0][rows], a_all[1][rows], a_pad], axis=0).T

    hsumT_ref[...] = jnp.zeros(hsumT_ref.shape, F32)
    ct_ref[...] = jnp.zeros(ct_ref.shape, F32)

    sub = lax.broadcasted_iota(jnp.int32, (PAIR_W, L), 0)
    head_rows = (sub < HEAD_DIM, sub >= HEAD_DIM)
    ri = lax.broadcasted_iota(jnp.int32, (L, L), 0)
    ci = lax.broadcasted_iota(jnp.int32, (L, L), 1)
    tri = (ri <= ci, ri >= ci)

    def body(it, carry):
        m_reps = list(carry)
        chunk = (it, NC - 1 - it)
        gq = []
        for d in range(2):
            r8 = pl.ds(pl.multiple_of(chunk[d] * SUBLANES, SUBLANES), SUBLANES)
            b, a, amax_run, g, amax = (pre_ref[d, n, r8, :] for n in range(5))
            m_old = m_reps[d]
            big_m = jnp.maximum(m_old, amax_run)
            m_new = jnp.maximum(g + m_old, g + amax)
            m_reps[d] = m_new
            gq.append(dict(big_m=big_m, w_inter=jnp.exp(m_old - big_m), e=jnp.exp(-(b + big_m)),
                           wk=jnp.exp(g + a - m_new), decay=jnp.exp(g + m_old - m_new),
                           a_cols=acol_ref[chunk[d]]))

        items = []
        for d in range(2):
            c = chunk[d]
            r0 = pl.multiple_of(c * L, L)
            for p in range(ML_HEADS // 2):
                rows = slice(p * PAIR_W, (p + 1) * PAIR_W)
                k_p = ktok_ref[pl.ds(r0, L), rows]
                q_tp = qT_ref[c, rows, :]
                v_tp = vT_ref[c, rows, :]
                for half in range(2):
                    h = 2 * p + half
                    q_tm = jnp.where(head_rows[half], q_tp, jnp.zeros_like(q_tp))
                    v_ta = jnp.where(head_rows[half], v_tp, jnp.ones_like(v_tp))
                    ct_old = ct_ref[d * ML_HEADS + h]
                    s_t = jnp.dot(k_p, q_tm, preferred_element_type=F32)
                    inter_t = jnp.dot(ct_old.astype(BF16), q_tm, preferred_element_type=F32)
                    v_tw = (v_ta.astype(F32) * gq[d]["wk"][h:h + 1, :]).astype(BF16)
                    kv_t = jnp.dot(v_tw, k_p, preferred_element_type=F32)
                    items.append((d, c, p, half, h, v_ta, ct_old, s_t, inter_t, kv_t))
        s2 = []
        for d, c, p, half, h, v_ta, ct_old, s_t, inter_t, kv_t in items:
            dm = jnp.where(tri[d], gq[d]["a_cols"][:, SUBLANES * d + h:SUBLANES * d + h + 1]
                           - gq[d]["big_m"][h:h + 1, :], -jnp.inf)
            s2.append((s_t * jnp.exp(dm)).astype(BF16))
        num = [jnp.dot(it_[5], s, preferred_element_type=F32) for it_, s in zip(items, s2)]
        for idx in range(0, len(items), 2):
            outs = []
            for (d, c, p, half, h, v_ta, ct_old, s_t, inter_t, kv_t), n_t in zip(items[idx:idx + 2],
                                                                                 num[idx:idx + 2]):
                n_t = n_t + gq[d]["w_inter"][h:h + 1, :] * inter_t
                den = jnp.concatenate([n_t[HEAD_DIM:], n_t[:HEAD_DIM]], axis=0)
                outs.append(n_t / jnp.maximum(jnp.abs(den), gq[d]["e"][h:h + 1, :]))
                ct_ref[d * ML_HEADS + h] = gq[d]["decay"][h:h + 1, :] * ct_old + kv_t
            hsumT_ref[c, p * PAIR_W:(p + 1) * PAIR_W, :] += jnp.where(head_rows[0], outs[0], outs[1])
        return tuple(m_reps)

    m0 = jnp.zeros((SUBLANES, L), F32)
    lax.fori_loop(0, NC, body, (m0, m0))

    for c in range(NC):
        hs_t = hsumT_ref[c]
        parts = []
        for h in range(ML_HEADS):
            blk = hs_t[h * HEAD_DIM:(h + 1) * HEAD_DIM, :]
            ms = jnp.mean(blk * blk, axis=0, keepdims=True)
            parts.append(blk * lax.rsqrt(ms + EPS))
        y = jnp.concatenate(parts, axis=0).T * ng_ref[...]
        o = otok_ref[c * L:(c + 1) * L, :].astype(F32)
        out_ref[c * L:(c + 1) * L, :] = (y * jax.nn.sigmoid(o)).astype(BF16)


def _mlstm(ktok, otok, q_t, v_t, gs_t, ng, batch):
    n = ktok.shape[0]
    T = n // batch
    nc = T // ML_CHUNK
    tok = pl.BlockSpec((T, ML_DIM), lambda b: (b, 0))
    slab = lambda rows: pl.BlockSpec((nc, rows, ML_CHUNK), lambda b: (b, 0, 0))
    return pl.pallas_call(
        _mlstm_kernel,
        out_shape=jax.ShapeDtypeStruct((n, ML_DIM), BF16),
        grid=(batch,),
        in_specs=[tok, tok, slab(ML_DIM), slab(ML_DIM), slab(4 * SUBLANES), _const_spec((1, ML_DIM))],
        out_specs=tok,
        scratch_shapes=[pltpu.VMEM((2, 5, nc * SUBLANES, ML_CHUNK), F32),
                        pltpu.VMEM((nc, ML_CHUNK, LANES), F32),
                        pltpu.VMEM((nc, ML_DIM, ML_CHUNK), F32),
                        pltpu.VMEM((2 * ML_HEADS, PAIR_W, PAIR_W), F32)],
        compiler_params=_cparams("parallel"),
        name="mlstm",
    )(ktok, otok, q_t, v_t, gs_t, ng)


def _gla_kernel(x_ref, gates_ref, w2_ref, ab_ref, ng_ref, out_ref, bc_ref, osum_ref, st_ref):
    T = x_ref.shape[0]
    L = GLA_CHUNK
    NC = T // L
    KP, VP = GLA_K_PAD, GLA_V_PAD
    RB = 256

    ri = lax.broadcasted_iota(jnp.int32, (RB, RB), 0)
    ci = lax.broadcasted_iota(jnp.int32, (RB, RB), 1)
    same = (ri // L) == (ci // L)
    tri_blk = (jnp.where(same & (ci <= ri), 1.0, 0.0).astype(BF16),
               jnp.where(same & (ci >= ri), 1.0, 0.0).astype(BF16))
    for r0 in range(0, T, RB):
        z = jnp.dot(gates_ref[r0:r0 + RB, :].astype(BF16), w2_ref[...],
                    preferred_element_type=F32) + ab_ref[...]
        hi, lo = _split_bf16(_log_sigmoid(z) * (1.0 / GLA_TAU))
        for d in range(2):
            sl = slice(d * KP, (d + 1) * KP)
            bc_ref[r0:r0 + RB, sl] = (jnp.dot(tri_blk[d], hi[:, sl], preferred_element_type=F32)
                                      + jnp.dot(tri_blk[d], lo[:, sl], preferred_element_type=F32))

    osum_ref[...] = jnp.zeros(osum_ref.shape, F32)
    st_ref[...] = jnp.zeros(st_ref.shape, F32)

    ri = lax.broadcasted_iota(jnp.int32, (L, L), 0)
    ci = lax.broadcasted_iota(jnp.int32, (L, L), 1)
    tri = (ci <= ri, ci >= ri)
    qhead = lax.broadcasted_iota(jnp.int32, (1, KP), 1) // GLA_DK
    pair_first = lax.broadcasted_iota(jnp.int32, (1, PAIR_W), 1) < HEAD_DIM
    st_mask = (lax.broadcasted_iota(jnp.int32, (VP, KP), 0) // HEAD_DIM
               == lax.broadcasted_iota(jnp.int32, (VP, KP), 1) // GLA_DK)

    def body(it, carry):
        chunk = (it, NC - 1 - it)
        w = []
        for d in range(2):
            r0 = pl.multiple_of(chunk[d] * L, L)
            bc = bc_ref[pl.ds(r0, L), d * KP:(d + 1) * KP]
            btot = bc[L - 1:L, :] if d == 0 else bc[0:1, :]
            q = x_ref[pl.ds(r0, L), 0:KP].astype(F32) * (GLA_DK ** -0.5)
            k = x_ref[pl.ds(r0, L), KP:2 * KP].astype(F32)
            v = x_ref[pl.ds(r0, L), 2 * KP:2 * KP + VP]
            q_dec = q * jnp.exp(bc)
            k_dec = (k * jnp.exp(-bc)).astype(BF16)
            k_tail = (k * jnp.exp(btot - bc)).astype(BF16)
            st_old = st_ref[d]
            o = lax.dot_general(q_dec.astype(BF16), st_old.astype(BF16), NT_DIMS,
                                preferred_element_type=F32)
            a = [lax.dot_general(jnp.where(qhead == h, q_dec, 0.0).astype(BF16), k_dec, NT_DIMS,
                                 preferred_element_type=F32) for h in range(GLA_HEADS)]
            kv = lax.dot_general(v, k_tail, TN_DIMS, preferred_element_type=F32)
            w.append((r0, btot, v, st_old, o, a, kv))
        ab = [[jnp.where(tri[d], a_h, 0.0).astype(BF16) for a_h in w[d][5]] for d in range(2)]
        oh = [[jnp.dot(a_h, w[d][2][:, (h // 2) * PAIR_W:(h // 2 + 1) * PAIR_W], preferred_element_type=F32)
               for h, a_h in enumerate(ab[d])] for d in range(2)]
        for d in range(2):
            r0, btot, v, st_old, o, a, kv = w[d]
            blocks = []
            for p in range(VP // PAIR_W):
                heads = [h for h in range(GLA_HEADS) if h // 2 == p]
                blk = jnp.where(pair_first, oh[d][heads[0]], oh[d][heads[1]] if len(heads) > 1 else 0.0)
                blocks.append(blk)
            st_ref[d] = st_old * jnp.exp(btot) + jnp.where(st_mask, kv, 0.0)
            osum_ref[pl.ds(r0, L), :] += o + jnp.concatenate(blocks, axis=1)
        return carry

    lax.fori_loop(0, NC, body, 0)

    for r0 in range(0, T, RB):
        hs = osum_ref[r0:r0 + RB, :]
        y = hs * lax.rsqrt(_group_mean_sq(hs, HEAD_DIM) + EPS) * ng_ref[...]
        g = x_ref[r0:r0 + RB, 2 * KP + VP:2 * KP + 2 * VP].astype(F32)
        out_ref[r0:r0 + RB, :] = (y * (g * jax.nn.sigmoid(g))).astype(BF16)


def _gla(x, gates, w2, ab, ng, batch):
    n = x.shape[0]
    T = n // batch
    return pl.pallas_call(
        _gla_kernel,
        out_shape=jax.ShapeDtypeStruct((n, GLA_V_PAD), BF16),
        grid=(batch,),
        in_specs=[pl.BlockSpec((T, GLA_COLS), lambda b: (b, 0)),
                  pl.BlockSpec((T, GATE_COLS), lambda b: (b, 0)),
                  _const_spec((GATE_COLS, 2 * GLA_K_PAD)), _const_spec((1, 2 * GLA_K_PAD)),
                  _const_spec((1, GLA_V_PAD))],
        out_specs=pl.BlockSpec((T, GLA_V_PAD), lambda b: (b, 0)),
        scratch_shapes=[pltpu.VMEM((T, 2 * GLA_K_PAD), F32),
                        pltpu.VMEM((T, GLA_V_PAD), F32),
                        pltpu.VMEM((2, GLA_V_PAD, GLA_K_PAD), F32)],
        compiler_params=_cparams("parallel"),
        name="gla",
    )(x, gates, w2, ab, ng)


NA_CLASSES = 8
NA_UNROLL = 8


def _na_kernel(q_ref, k_ref, v_ref, bias_ref, out_ref, *, n_heads_in_pair):
    T = q_ref.shape[0]
    W = GRID_W
    rows = T // W
    band = NA_KH * W
    lane = lax.broadcasted_iota(jnp.int32, (1, PAIR_W), 1)
    head_mask = (lane < HEAD_DIM, lane >= HEAD_DIM)
    scale = HEAD_DIM ** -0.5

    def group(gi, carry):
        items = []
        for u in range(NA_UNROLL):
            r = gi * NA_UNROLL + u
            rs = jnp.clip(r - NA_KH // 2, 0, rows - NA_KH)
            cls = jnp.where(r < NA_KH // 2, r, jnp.where(r > rows - NA_KH // 2, r - (rows - NA_KH), NA_KH // 2))
            q0 = pl.multiple_of(r * W, W)
            k0 = pl.multiple_of(rs * W, W)
            q = q_ref[pl.ds(q0, W), :]
            kb = k_ref[pl.ds(k0, band), :]
            for half in range(n_heads_in_pair):
                q_m = jnp.where(head_mask[half], q, jnp.zeros_like(q))
                s = lax.dot_general(q_m, kb, NT_DIMS, preferred_element_type=F32)
                items.append((u, half, q0, k0, cls, s))
        probs = []
        for u, half, q0, k0, cls, s in items:
            s = s * scale + bias_ref[half, cls]
            m = jnp.max(s, axis=1, keepdims=True)
            p = jnp.exp(s - m)
            l = jnp.sum(p, axis=1, keepdims=True)
            probs.append((u, half, q0, k0, p.astype(BF16), l))
        outs = {}
        for u, half, q0, k0, p, l in probs:
            vb = v_ref[pl.ds(k0, band), :]
            outs[(u, half)] = (q0, jnp.dot(p, vb, preferred_element_type=F32) / l)
        for u in range(NA_UNROLL):
            q0, o0 = outs[(u, 0)]
            second = outs[(u, 1)][1] if n_heads_in_pair == 2 else 0.0
            out_ref[pl.ds(q0, W), :] = jnp.where(head_mask[0], o0, second).astype(BF16)
        return carry

    lax.fori_loop(0, rows // NA_UNROLL, group, 0)


def _na(na, bias, batch):
    n = na.shape[0]
    T = n // batch
    npairs = NA_PAD // PAIR_W
    outs = []
    for p0, p1, nh in ((0, NA_HEADS // 2, 2), (NA_HEADS // 2, npairs, 1)):
        npp = p1 - p0
        outs.append(pl.pallas_call(
            functools.partial(_na_kernel, n_heads_in_pair=nh),
            out_shape=jax.ShapeDtypeStruct((n, npp * PAIR_W), BF16),
            grid=(npp, batch),
            in_specs=[pl.BlockSpec((T, PAIR_W), lambda p, b, o=p0: (b, o + p)),
                      pl.BlockSpec((T, PAIR_W), lambda p, b, o=npairs + p0: (b, o + p)),
                      pl.BlockSpec((T, PAIR_W), lambda p, b, o=2 * npairs + p0: (b, o + p)),
                      pl.BlockSpec((2, NA_CLASSES, GRID_W, NA_KH * GRID_W),
                                   lambda p, b, o=p0: (o + p, 0, 0, 0))],
            out_specs=pl.BlockSpec((T, PAIR_W), lambda p, b: (b, p)),
            compiler_params=_cparams("parallel", "parallel"),
            name=f"natten{nh}",
        )(na, na, na, bias))
    return outs


def _na_bias_table(rpb, rows):
    kh = NA_KH
    rep_rows = np.array(list(range(kh // 2)) + [kh // 2] + list(range(rows - kh // 2 + 1, rows)))
    row_start = np.clip(rep_rows - kh // 2, 0, rows - kh)
    dr = row_start[:, None] + np.arange(kh)[None, :] - rep_rows[:, None] + (kh - 1)
    c = np.arange(GRID_W)
    col_start = np.clip(c - NA_KW // 2, 0, GRID_W - NA_KW)
    in_win = (c[None, :] >= col_start[:, None]) & (c[None, :] < col_start[:, None] + NA_KW)
    dc = np.clip(c[None, :] - c[:, None], -(NA_KW - 1), NA_KW - 1) + (NA_KW - 1)
    sel_r = jnp.asarray(np.eye(2 * kh - 1, dtype=np.float32)[dr])
    sel_c = jnp.asarray(np.eye(2 * NA_KW - 1, dtype=np.float32)[dc])
    tab = jnp.einsum('skr,hrc->hskc', sel_r, rpb.astype(F32), precision=lax.Precision.HIGHEST)
    tab = jnp.einsum('hskc,qwc->hsqkw', tab, sel_c, precision=lax.Precision.HIGHEST)
    tab = jnp.where(in_win[None, None, :, None, :], tab, -jnp.inf)
    tab = tab.reshape(rpb.shape[0], len(rep_rows), GRID_W, kh * GRID_W)
    return jnp.pad(tab, ((0, 2 * (NA_PAD // PAIR_W) - rpb.shape[0]), (0, 0), (0, 0), (0, 0)))


def _memkv_kernel(m_ref, g_ref, wk_ref, wv_ref, k_ref, v_ref):
    mn = _rms(m_ref[...], g_ref[...]).astype(BF16)
    k_ref[...] = jnp.dot(mn, wk_ref[...], preferred_element_type=F32).astype(BF16)
    v_ref[...] = jnp.dot(mn, wv_ref[...], preferred_element_type=F32).astype(BF16)


def _memkv(mem, g, wk, wv):
    n, d = mem.shape
    tm = min(ROW_TILE, n)
    row = pl.BlockSpec((tm, d), lambda i: (i, 0))
    return pl.pallas_call(
        _memkv_kernel,
        out_shape=(jax.ShapeDtypeStruct((n, d), BF16), jax.ShapeDtypeStruct((n, d), BF16)),
        grid=(n // tm,),
        in_specs=[row, _const_spec((1, d)), _const_spec((d, d)), _const_spec((d, d))],
        out_specs=(row, row),
        compiler_params=_cparams("parallel"),
        name="memkv",
    )(mem, g, wk, wv)


def _xattn_kernel(ml_ref, na0_ref, na1_ref, gla_ref, wout_ref, h_ref, g_ref, wq_ref, k_ref, v_ref, wo_ref,
                  o_ref):
    mixed = jnp.concatenate([ml_ref[...], na0_ref[...], na1_ref[...], gla_ref[...]], axis=1)
    o_ref[...] = h_ref[...] + jnp.dot(mixed, wout_ref[...], preferred_element_type=F32)
    h = o_ref[...]
    tm, d = h.shape
    hd = d // X_HEADS
    heads = [slice(i * hd, (i + 1) * hd) for i in range(X_HEADS)]
    blocks = _rms_row_blocks(o_ref, g_ref[...], tm)
    hn = jnp.concatenate(blocks, axis=0)
    q = [(_dot_row_blocks(blocks, wq_ref[:, sl]) if n == 0 else
          jnp.dot(hn, wq_ref[:, sl], preferred_element_type=F32)).astype(BF16) for n, sl in enumerate(heads)]
    scores = [lax.dot_general(q_h, k_ref[:, sl], NT_DIMS, preferred_element_type=F32)
              for q_h, sl in zip(q, heads)]
    probs = []
    for s in scores:
        s = s * (hd ** -0.5)
        p = jnp.exp(s - jnp.max(s, axis=1, keepdims=True))
        probs.append((p.astype(BF16), jnp.sum(p, axis=1, keepdims=True)))
    ctx = [jnp.dot(p, v_ref[:, sl], preferred_element_type=F32) for (p, _), sl in zip(probs, heads)]
    o = jnp.concatenate([(c / l).astype(BF16) for c, (_, l) in zip(ctx, probs)], axis=1)
    o_ref[...] = h + jnp.dot(o, wo_ref[...], preferred_element_type=F32)


def _xattn(mixed, wout, h, g, wq, k, v, wo, batch):
    n, d = h.shape
    tm = ROW_TILE
    tiles_per_seq = (n // batch) // tm
    n_mem = k.shape[0] // batch
    row = lambda cols: pl.BlockSpec((tm, cols), lambda i: (i, 0))
    kv = pl.BlockSpec((n_mem, d), lambda i: (i // tiles_per_seq, 0))
    return pl.pallas_call(
        _xattn_kernel,
        out_shape=jax.ShapeDtypeStruct((n, d), F32),
        grid=(n // tm,),
        in_specs=[row(m.shape[1]) for m in mixed] + [
            _const_spec(wout.shape), row(d), _const_spec((1, d)), _const_spec((d, d)), kv, kv,
            _const_spec((d, d))],
        out_specs=row(d),
        compiler_params=_cparams("parallel"),
        name="xattn",
    )(*mixed, wout, h, g, wq, k, v, wo)


FFN_CHUNK = 256


def _gelu_tanh(x):
    c = float(np.sqrt(2.0 / np.pi))
    half = 0.5 * x
    return half + half * jnp.tanh(x * (c + (c * 0.044715) * (x * x)))


def _ffn_kernel(h_ref, hp_ref, hx_ref, g_ref, wup_ref, cw_ref, cb_ref, wdn_ref, gf_ref, o_ref, *,
                tiles_per_seq, final_norm):
    i = pl.program_id(0)
    h = h_ref[...]
    tm = h.shape[0]
    dff = wdn_ref.shape[0]
    halo = jnp.concatenate([hp_ref[...], hx_ref[...]], axis=0)
    blocks = _rms_row_blocks(h_ref, g_ref[...], tm) + [_rms(halo, g_ref[...]).astype(BF16)]
    hn = jnp.concatenate(blocks[:-1], axis=0)
    hn_ext = jnp.concatenate(blocks, axis=0)
    has_prev, has_next = _halo_flags(i, tiles_per_seq)

    def up(c):
        wa = wup_ref[:, c]
        wg = wup_ref[:, slice(dff + c.start, dff + c.stop)]
        return (jnp.dot(hn_ext, wa, preferred_element_type=F32), jnp.dot(hn, wg, preferred_element_type=F32))

    acts = []
    for c in [slice(c, min(c + FFN_CHUNK, dff)) for c in range(0, dff, FFN_CHUNK)]:
        a_ext, gate = up(c)
        ac = _conv3(a_ext, tm, cw_ref[:, c], cb_ref[:, c], has_prev, has_next)
        acts.append((_gelu_tanh(ac) * gate).astype(BF16))
    acc = h + jnp.dot(jnp.concatenate(acts, axis=1), wdn_ref[...], preferred_element_type=F32)
    o_ref[...] = _rms(acc, gf_ref[...]) if final_norm else acc


def _ffn(h, g, wup, cw, cb, wdn, gf, batch, final_norm):
    n, d = h.shape
    tm = ROW_TILE
    tiles_per_seq = (n // batch) // tm
    row = pl.BlockSpec((tm, d), lambda i: (i, 0))
    return pl.pallas_call(
        functools.partial(_ffn_kernel, tiles_per_seq=tiles_per_seq, final_norm=final_norm),
        out_shape=jax.ShapeDtypeStruct((n, d), F32),
        grid=(n // tm,),
        in_specs=[row] + _halo_specs(tm, d, n) + [
            _const_spec((1, d)), _const_spec(wup.shape), _const_spec(cw.shape),
            _const_spec(cb.shape), _const_spec(wdn.shape), _const_spec((1, d))],
        out_specs=row,
        compiler_params=_cparams("parallel"),
        name="convffn",
    )(h, h, h, g, wup, cw, cb, wdn, gf)


def _pad_cols(a, n):
    return jnp.pad(a, ((0, 0), (0, n - a.shape[1])))


def _pad_rows(a, n):
    return jnp.pad(a, ((0, n - a.shape[0]), (0, 0)))


def _prep_w_in(w):
    offs = np.concatenate([[0], np.cumsum(IN_SIZES)])
    seg = [w[:, offs[i]:offs[i + 1]] for i in range(len(IN_SIZES))]
    ml = seg[0:4]
    na = [_pad_cols(s, NA_PAD) for s in seg[5:8]]
    gla = [_pad_cols(seg[8], GLA_K_PAD), _pad_cols(seg[9], GLA_K_PAD),
           _pad_cols(seg[10], GLA_V_PAD), _pad_cols(seg[11], GLA_V_PAD)]
    gates = [_pad_cols(seg[4][:, j * ML_HEADS:(j + 1) * ML_HEADS], SUBLANES) for j in range(4)]
    gates = _pad_cols(jnp.concatenate(gates + [seg[12]], axis=1), GATE_COLS)
    return jnp.concatenate(ml + na + gla + [gates], axis=1).astype(BF16)


def _prep_gate_bias(gb):
    return _pad_cols(jnp.concatenate([_pad_cols(gb[j:j + 1], SUBLANES) for j in range(4)], axis=1), GATE_COLS)


def _prep_gla_gate(w2, ab):
    w = jnp.zeros((GATE_COLS, 2 * GLA_K_PAD), F32)
    b = jnp.zeros((1, 2 * GLA_K_PAD), F32)
    for z in range(2):
        r0 = GLA_GATE_OFF + z * GLA_RANK
        w = w.at[r0:r0 + GLA_RANK, z * GLA_K_PAD:z * GLA_K_PAD + GLA_KDIM].set(w2[z])
        b = b.at[0, z * GLA_K_PAD:z * GLA_K_PAD + GLA_KDIM].set(ab[z])
    return w.astype(BF16), b


def _prep_w_out(w):
    parts = [w[0:ML_DIM], _pad_rows(w[ML_DIM:ML_DIM + NA_DIM], NA_PAD),
             _pad_rows(w[ML_DIM + NA_DIM:], GLA_V_PAD)]
    return jnp.concatenate(parts, axis=0).astype(BF16)


def kernel(x, mem, mix_norm_g, w_in, ml_conv_w, ml_conv_b, ml_gate_b, ml_norm_g, na_rpb,
           gla_a_w2, gla_a_b, gla_norm_g, w_out, xattn_norm_g, mem_norm_g,
           w_xq, w_xk, w_xv, w_xo, ffn_norm_g, w_up, ffn_conv_w, ffn_conv_b, w_down,
           final_norm_g):
    batch, seq, d = x.shape
    depth = w_in.shape[0]
    rows = seq // GRID_W
    assert seq % ROW_TILE == 0 and rows >= NA_KH and rows % NA_UNROLL == 0
    h = x.reshape(batch * seq, d)
    mem2 = mem.reshape(batch * mem.shape[1], d)
    r2 = lambda v: v.reshape(1, -1)
    w_in_p = jax.vmap(_prep_w_in)(w_in)
    gate_b_p = jax.vmap(_prep_gate_bias)(ml_gate_b)
    na_bias_p = jax.vmap(functools.partial(_na_bias_table, rows=rows))(na_rpb)
    gla_w2_p, gla_ab_p = jax.vmap(_prep_gla_gate)(gla_a_w2, gla_a_b)
    gla_ng_p = jnp.pad(gla_norm_g, ((0, 0), (0, GLA_V_PAD - gla_norm_g.shape[1])))
    w_out_p = jax.vmap(_prep_w_out)(w_out)
    w_xq_b, w_xk_b, w_xv_b, w_xo_b, w_up_b, w_down_b = (
        w.astype(BF16) for w in (w_xq, w_xk, w_xv, w_xo, w_up, w_down))
    for l in range(depth):
        ktok, otok, q_t, v_t, gs_t, na, gla, gates = _inproj(
            h, r2(mix_norm_g[l]), w_in_p[l], ml_conv_w[l], r2(ml_conv_b[l]), gate_b_p[l], batch)
        ml_out = _mlstm(ktok, otok, q_t, v_t, gs_t, r2(ml_norm_g[l]), batch)
        na0, na1 = _na(na, na_bias_p[l], batch)
        gla_out = _gla(gla, gates, gla_w2_p[l], gla_ab_p[l], r2(gla_ng_p[l]), batch)
        k, v = _memkv(mem2, r2(mem_norm_g[l]), w_xk_b[l], w_xv_b[l])
        h = _xattn((ml_out, na0, na1, gla_out), w_out_p[l], h, r2(xattn_norm_g[l]),
                   w_xq_b[l], k, v, w_xo_b[l], batch)
        h = _ffn(h, r2(ffn_norm_g[l]), w_up_b[l], ffn_conv_w[l], r2(ffn_conv_b[l]),
                 w_down_b[l], r2(final_norm_g), batch, final_norm=(l == depth - 1))
    return h.reshape(batch, seq, d)
```

```python
import functools

import numpy as np
import jax
import jax.numpy as jnp
from jax import lax
from jax.experimental import pallas as pl
from jax.experimental.pallas import tpu as pltpu

F32 = jnp.float32
BF16 = jnp.bfloat16

HEAD_DIM = 64
ML_HEADS = 6
NA_HEADS = 5
GLA_HEADS = 5
ML_DIM = ML_HEADS * HEAD_DIM
NA_DIM = NA_HEADS * HEAD_DIM
GLA_DK = 32
GLA_KDIM = GLA_HEADS * GLA_DK
GLA_VDIM = GLA_HEADS * HEAD_DIM
GLA_RANK = 16
GLA_TAU = 16.0
GRID_W = 64
NA_KH = 8
NA_KW = 16
ML_CHUNK = 128
GLA_CHUNK = 64
X_HEADS = 4
EPS = 1e-6
IN_SIZES = (ML_DIM, ML_DIM, ML_DIM, ML_DIM, 4 * ML_HEADS,
            NA_DIM, NA_DIM, NA_DIM,
            GLA_KDIM, GLA_KDIM, GLA_VDIM, GLA_VDIM, 2 * GLA_RANK)

LANES = 128
SUBLANES = 8

PAIR_W = 2 * HEAD_DIM
ML_COLS = 4 * ML_DIM
NA_PAD = 3 * PAIR_W
NA_COLS = 3 * NA_PAD
GLA_K_PAD = 256
GLA_V_PAD = 3 * PAIR_W
GLA_COLS = 2 * GLA_K_PAD + 2 * GLA_V_PAD
GATE_COLS = LANES
GLA_GATE_OFF = 4 * SUBLANES
IN_COLS_PAD = ML_COLS + NA_COLS + GLA_COLS + GATE_COLS

ROW_TILE = 512
NORM_BLOCK = 128
VMEM_LIMIT = 56 * 1024 * 1024

NT_DIMS = (((1,), (1,)), ((), ()))
TN_DIMS = (((0,), (0,)), ((), ()))


def _cparams(*sem):
    return pltpu.CompilerParams(dimension_semantics=sem, vmem_limit_bytes=VMEM_LIMIT)


def _const_spec(shape):
    nd = len(shape)
    return pl.BlockSpec(shape, lambda *_: (0,) * nd)


def _layer_spec(a, l):
    nd = a.ndim - 1
    return pl.BlockSpec((None,) + a.shape[1:], lambda *_, l=l, nd=nd: (l,) + (0,) * nd)


def _rms(x, g):
    ms = jnp.mean(x * x, axis=-1, keepdims=True)
    return x * lax.rsqrt(ms + EPS) * g


def _rms_row_blocks(x_ref, g, tm):
    return [_rms(x_ref[r0:r0 + NORM_BLOCK, :], g).astype(BF16) for r0 in range(0, tm, NORM_BLOCK)]


def _dot_row_blocks(blocks, w):
    return jnp.concatenate([jnp.dot(b, w, preferred_element_type=F32) for b in blocks], axis=0)


def _log_sigmoid(x):
    return jnp.minimum(x, 0.0) - jnp.log1p(jnp.exp(-jnp.abs(x)))


def _split_bf16(x):
    hi = x.astype(BF16)
    lo = (x - hi.astype(F32)).astype(BF16)
    return hi, lo


def _group_mean_sq(x, group):
    n = x.shape[-1]
    r = lax.broadcasted_iota(jnp.int32, (n, n), 0) // group
    c = lax.broadcasted_iota(jnp.int32, (n, n), 1) // group
    ones = jnp.where(r == c, 1.0, 0.0).astype(BF16)
    hi, lo = _split_bf16(x * x)
    s = jnp.dot(hi, ones, preferred_element_type=F32) + jnp.dot(lo, ones, preferred_element_type=F32)
    return s * (1.0 / group)


def _halo_flags(i, tiles_per_seq):
    has_prev = (i % tiles_per_seq != 0).astype(F32)
    has_next = (i % tiles_per_seq != tiles_per_seq - 1).astype(F32)
    return has_prev, has_next


def _conv3(a_ext, tm, w, b, has_prev, has_next):
    row = lax.broadcasted_iota(jnp.int32, (SUBLANES, 1), 0)
    a = a_ext[0:tm]
    down = pltpu.roll(a, 1, 0)
    up = pltpu.roll(a, tm - 1, 0)
    prev_row = a_ext[tm + SUBLANES - 1:tm + SUBLANES, :] * has_prev
    next_row = a_ext[tm + SUBLANES:tm + SUBLANES + 1, :] * has_next
    a_prev = jnp.concatenate([jnp.where(row == 0, prev_row, down[0:SUBLANES]), down[SUBLANES:]], axis=0)
    a_next = jnp.concatenate([up[:tm - SUBLANES],
                              jnp.where(row == SUBLANES - 1, next_row, up[tm - SUBLANES:])], axis=0)
    return a_prev * w[0:1, :] + a * w[1:2, :] + a_next * w[2:3, :] + b


def _inproj_kernel(h_ref, hp_ref, hx_ref, g_ref, w_ref, cw_ref, cb_ref, gb_ref,
                   ktok_ref, otok_ref, qT_ref, vT_ref, gsT_ref, na_ref, gla_ref, gate_ref, *, tiles_per_seq):
    tm = h_ref.shape[0]
    L = ML_CHUNK
    has_prev, has_next = _halo_flags(pl.program_id(0), tiles_per_seq)
    halo = jnp.concatenate([hp_ref[...], hx_ref[...]], axis=0)
    blocks = _rms_row_blocks(h_ref, g_ref[...], tm) + [_rms(halo, g_ref[...]).astype(BF16)]
    xn = jnp.concatenate(blocks[:-1], axis=0)
    xn_ext = jnp.concatenate(blocks, axis=0)

    def proj(lhs, c0, c1):
        return jnp.dot(lhs, w_ref[:, c0:c1], preferred_element_type=F32)

    for part in range(2):
        cols = slice(part * ML_DIM, (part + 1) * ML_DIM)
        a_ext = _dot_row_blocks(blocks, w_ref[:, cols]) if part == 0 else proj(xn_ext, cols.start, cols.stop)
        y = _conv3(a_ext, tm, cw_ref[:, cols], cb_ref[:, cols], has_prev, has_next)
        y = y * jax.nn.sigmoid(y)
        if part == 0:
            y_t = y.T
            for j in range(tm // L):
                qT_ref[j] = y_t[:, j * L:(j + 1) * L].astype(BF16)
        else:
            ktok_ref[...] = (y * HEAD_DIM ** -0.5).astype(BF16)
    v_t = proj(xn, 2 * ML_DIM, 3 * ML_DIM).T
    for j in range(tm // L):
        vT_ref[j] = v_t[:, j * L:(j + 1) * L].astype(BF16)
    otok_ref[...] = proj(xn, 3 * ML_DIM, 4 * ML_DIM).astype(BF16)
    off = ML_COLS
    for ref, chunk in ((na_ref, 384), (gla_ref, 256), (gate_ref, 128)):
        width = ref.shape[-1]
        for c in range(0, width, chunk):
            ref[:, c:c + chunk] = proj(xn, off + c, off + c + chunk).astype(ref.dtype)
        off += width
    g_t = (gate_ref[...] + gb_ref[...]).T
    for j in range(tm // L):
        gsT_ref[j] = g_t[0:4 * SUBLANES, j * L:(j + 1) * L]


def _halo_specs(tm, d, n):
    hb = tm // SUBLANES
    last = n // SUBLANES - 1
    return [pl.BlockSpec((SUBLANES, d), lambda i: (jnp.maximum(i * hb - 1, 0), 0)),
            pl.BlockSpec((SUBLANES, d), lambda i: (jnp.minimum((i + 1) * hb, last), 0))]


def _inproj(h, g, w, cw, cb, gb, l, batch):
    n, d = h.shape
    tm = ROW_TILE
    nc = tm // ML_CHUNK
    row = lambda cols: pl.BlockSpec((tm, cols), lambda i: (i, 0))
    slab = lambda rows: pl.BlockSpec((nc, rows, ML_CHUNK), lambda i: (i, 0, 0))
    return pl.pallas_call(
        functools.partial(_inproj_kernel, tiles_per_seq=(n // batch) // tm),
        out_shape=(jax.ShapeDtypeStruct((n, ML_DIM), BF16),
                   jax.ShapeDtypeStruct((n, ML_DIM), BF16),
                   jax.ShapeDtypeStruct((n // ML_CHUNK, ML_DIM, ML_CHUNK), BF16),
                   jax.ShapeDtypeStruct((n // ML_CHUNK, ML_DIM, ML_CHUNK), BF16),
                   jax.ShapeDtypeStruct((n // ML_CHUNK, 4 * SUBLANES, ML_CHUNK), F32),
                   jax.ShapeDtypeStruct((n, NA_COLS), BF16),
                   jax.ShapeDtypeStruct((n, GLA_COLS), BF16),
                   jax.ShapeDtypeStruct((n, GATE_COLS), F32)),
        grid=(n // tm,),
        in_specs=[row(d)] + _halo_specs(tm, d, n) + [_layer_spec(a, l) for a in (g, w, cw, cb, gb)],
        out_specs=(row(ML_DIM), row(ML_DIM), slab(ML_DIM), slab(ML_DIM), slab(4 * SUBLANES),
                   row(NA_COLS), row(GLA_COLS), row(GATE_COLS)),
        compiler_params=_cparams("parallel"),
        name="inproj",
    )(h, h, h, g, w, cw, cb, gb)


def _lane_scan(x, op, fill, reverse):
    n = x.shape[-1]
    lane = lax.broadcasted_iota(jnp.int32, x.shape, 1)
    s = 1
    while s < n:
        if reverse:
            x = op(x, jnp.where(lane < n - s, pltpu.roll(x, n - s, 1), fill))
        else:
            x = op(x, jnp.where(lane >= s, pltpu.roll(x, s, 1), fill))
        s *= 2
    return x


def _mlstm_kernel(ktok_ref, otok_ref, qT_ref, vT_ref, gsT_ref, ng_ref, out_ref,
                  pre_ref, acol_ref, hsumT_ref, ct_ref):
    NC, _, L = qT_ref.shape

    def gate_group(grp):
        return gsT_ref[:, grp * SUBLANES:(grp + 1) * SUBLANES, :].reshape(NC * SUBLANES, L)

    a_all = []
    for d in range(2):
        logf = _log_sigmoid(gate_group(2 * d + 1))
        b = _lane_scan(logf, jnp.add, 0.0, reverse=(d == 1))
        a = gate_group(2 * d) - b
        pre_ref[d, 0] = b
        pre_ref[d, 1] = a
        pre_ref[d, 2] = _lane_scan(a, jnp.maximum, -jnp.inf, reverse=(d == 1))
        pre_ref[d, 3] = jnp.broadcast_to(jnp.sum(logf, axis=1, keepdims=True), a.shape)
        pre_ref[d, 4] = jnp.broadcast_to(jnp.max(a, axis=1, keepdims=True), a.shape)
        a_all.append(a)
    a_pad = jnp.zeros((L - 2 * SUBLANES, L), F32)
    for c in range(NC):
        rows = slice(c * SUBLANES, (c + 1) * SUBLANES)
        acol_ref[c] = jnp.concatenate([a_all[0][rows], a_all[1][rows], a_pad], axis=0).T

    hsumT_ref[...] = jnp.zeros(hsumT_ref.shape, F32)
    ct_ref[...] = jnp.zeros(ct_ref.shape, F32)

    sub = lax.broadcasted_iota(jnp.int32, (PAIR_W, L), 0)
    head_rows = (sub < HEAD_DIM, sub >= HEAD_DIM)
    ri = lax.broadcasted_iota(jnp.int32, (L, L), 0)
    ci = lax.broadcasted_iota(jnp.int32, (L, L), 1)
    tri = (ri <= ci, ri >= ci)

    def body(it, carry):
        m_reps = list(carry)
        chunk = (it, NC - 1 - it)
        gq = []
        for d in range(2):
            r8 = pl.ds(pl.multiple_of(chunk[d] * SUBLANES, SUBLANES), SUBLANES)
            b, a, amax_run, g, amax = (pre_ref[d, n, r8, :] for n in range(5))
            m_old = m_reps[d]
            big_m = jnp.maximum(m_old, amax_run)
            m_new = jnp.maximum(g + m_old, g + amax)
            m_reps[d] = m_new
            gq.append(dict(big_m=big_m, w_inter=jnp.exp(m_old - big_m), e=jnp.exp(-(b + big_m)),
                           wk=jnp.exp(g + a - m_new), decay=jnp.exp(g + m_old - m_new),
                           a_cols=acol_ref[chunk[d]]))

        items = []
        for d in range(2):
            c = chunk[d]
            r0 = pl.multiple_of(c * L, L)
            for p in range(ML_HEADS // 2):
                rows = slice(p * PAIR_W, (p + 1) * PAIR_W)
                k_p = ktok_ref[pl.ds(r0, L), rows]
                q_tp = qT_ref[c, rows, :]
                v_tp = vT_ref[c, rows, :]
                for half in range(2):
                    h = 2 * p + half
                    q_tm = jnp.where(head_rows[half], q_tp, jnp.zeros_like(q_tp))
                    v_ta = jnp.where(head_rows[half], v_tp, jnp.ones_like(v_tp))
                    ct_old = ct_ref[d * ML_HEADS + h]
                    s_t = jnp.dot(k_p, q_tm, preferred_element_type=F32)
                    inter_t = jnp.dot(ct_old.astype(BF16), q_tm, preferred_element_type=F32)
                    v_tw = (v_ta.astype(F32) * gq[d]["wk"][h:h + 1, :]).astype(BF16)
                    kv_t = jnp.dot(v_tw, k_p, preferred_element_type=F32)
                    items.append((d, c, p, half, h, v_ta, ct_old, s_t, inter_t, kv_t))
        s2 = []
        for d, c, p, half, h, v_ta, ct_old, s_t, inter_t, kv_t in items:
            dm = jnp.where(tri[d], gq[d]["a_cols"][:, SUBLANES * d + h:SUBLANES * d + h + 1]
                           - gq[d]["big_m"][h:h + 1, :], -jnp.inf)
            s2.append((s_t * jnp.exp(dm)).astype(BF16))
        num = [jnp.dot(it_[5], s, preferred_element_type=F32) for it_, s in zip(items, s2)]
        for idx in range(0, len(items), 2):
            outs = []
            for (d, c, p, half, h, v_ta, ct_old, s_t, inter_t, kv_t), n_t in zip(items[idx:idx + 2],
                                                                                 num[idx:idx + 2]):
                n_t = n_t + gq[d]["w_inter"][h:h + 1, :] * inter_t
                den = jnp.concatenate([n_t[HEAD_DIM:], n_t[:HEAD_DIM]], axis=0)
                outs.append(n_t / jnp.maximum(jnp.abs(den), gq[d]["e"][h:h + 1, :]))
                ct_ref[d * ML_HEADS + h] = gq[d]["decay"][h:h + 1, :] * ct_old + kv_t
            hsumT_ref[c, p * PAIR_W:(p + 1) * PAIR_W, :] += jnp.where(head_rows[0], outs[0], outs[1])
        return tuple(m_reps)

    m0 = jnp.zeros((SUBLANES, L), F32)
    lax.fori_loop(0, NC, body, (m0, m0))

    for c in range(NC):
        hs_t = hsumT_ref[c]
        parts = []
        for h in range(ML_HEADS):
            blk = hs_t[h * HEAD_DIM:(h + 1) * HEAD_DIM, :]
            ms = jnp.mean(blk * blk, axis=0, keepdims=True)
            parts.append(blk * lax.rsqrt(ms + EPS))
        y = jnp.concatenate(parts, axis=0).T * ng_ref[...]
        o = otok_ref[c * L:(c + 1) * L, :].astype(F32)
        out_ref[c * L:(c + 1) * L, :] = (y * jax.nn.sigmoid(o)).astype(BF16)


def _mlstm(ktok, otok, q_t, v_t, gs_t, ng, l, batch):
    n = ktok.shape[0]
    T = n // batch
    nc = T // ML_CHUNK
    tok = pl.BlockSpec((T, ML_DIM), lambda b: (b, 0))
    slab = lambda rows: pl.BlockSpec((nc, rows, ML_CHUNK), lambda b: (b, 0, 0))
    return pl.pallas_call(
        _mlstm_kernel,
        out_shape=jax.ShapeDtypeStruct((n, ML_DIM), BF16),
        grid=(batch,),
        in_specs=[tok, tok, slab(ML_DIM), slab(ML_DIM), slab(4 * SUBLANES), _layer_spec(ng, l)],
        out_specs=tok,
        scratch_shapes=[pltpu.VMEM((2, 5, nc * SUBLANES, ML_CHUNK), F32),
                        pltpu.VMEM((nc, ML_CHUNK, LANES), F32),
                        pltpu.VMEM((nc, ML_DIM, ML_CHUNK), F32),
                        pltpu.VMEM((2 * ML_HEADS, PAIR_W, PAIR_W), F32)],
        compiler_params=_cparams("parallel"),
        name="mlstm",
    )(ktok, otok, q_t, v_t, gs_t, ng)


def _gla_kernel(x_ref, gates_ref, w2_ref, ab_ref, ng_ref, out_ref, bc_ref, osum_ref, st_ref):
    T = x_ref.shape[0]
    L = GLA_CHUNK
    NC = T // L
    KP, VP = GLA_K_PAD, GLA_V_PAD
    RB = 256

    ri = lax.broadcasted_iota(jnp.int32, (RB, RB), 0)
    ci = lax.broadcasted_iota(jnp.int32, (RB, RB), 1)
    same = (ri // L) == (ci // L)
    tri_blk = (jnp.where(same & (ci <= ri), 1.0, 0.0).astype(BF16),
               jnp.where(same & (ci >= ri), 1.0, 0.0).astype(BF16))
    for r0 in range(0, T, RB):
        z = jnp.dot(gates_ref[r0:r0 + RB, :].astype(BF16), w2_ref[...],
                    preferred_element_type=F32) + ab_ref[...]
        hi, lo = _split_bf16(_log_sigmoid(z) * (1.0 / GLA_TAU))
        for d in range(2):
            sl = slice(d * KP, (d + 1) * KP)
            bc_ref[r0:r0 + RB, sl] = (jnp.dot(tri_blk[d], hi[:, sl], preferred_element_type=F32)
                                      + jnp.dot(tri_blk[d], lo[:, sl], preferred_element_type=F32))

    osum_ref[...] = jnp.zeros(osum_ref.shape, F32)
    st_ref[...] = jnp.zeros(st_ref.shape, F32)

    ri = lax.broadcasted_iota(jnp.int32, (L, L), 0)
    ci = lax.broadcasted_iota(jnp.int32, (L, L), 1)
    tri = (ci <= ri, ci >= ri)
    qhead = lax.broadcasted_iota(jnp.int32, (1, KP), 1) // GLA_DK
    pair_first = lax.broadcasted_iota(jnp.int32, (1, PAIR_W), 1) < HEAD_DIM
    st_mask = (lax.broadcasted_iota(jnp.int32, (VP, KP), 0) // HEAD_DIM
               == lax.broadcasted_iota(jnp.int32, (VP, KP), 1) // GLA_DK)

    def body(it, carry):
        chunk = (it, NC - 1 - it)
        w = []
        for d in range(2):
            r0 = pl.multiple_of(chunk[d] * L, L)
            bc = bc_ref[pl.ds(r0, L), d * KP:(d + 1) * KP]
            btot = bc[L - 1:L, :] if d == 0 else bc[0:1, :]
            q = x_ref[pl.ds(r0, L), 0:KP].astype(F32) * (GLA_DK ** -0.5)
            k = x_ref[pl.ds(r0, L), KP:2 * KP].astype(F32)
            v = x_ref[pl.ds(r0, L), 2 * KP:2 * KP + VP]
            q_dec = q * jnp.exp(bc)
            k_dec = (k * jnp.exp(-bc)).astype(BF16)
            k_tail = (k * jnp.exp(btot - bc)).astype(BF16)
            st_old = st_ref[d]
            o = lax.dot_general(q_dec.astype(BF16), st_old.astype(BF16), NT_DIMS,
                                preferred_element_type=F32)
            a = [lax.dot_general(jnp.where(qhead == h, q_dec, 0.0).astype(BF16), k_dec, NT_DIMS,
                                 preferred_element_type=F32) for h in range(GLA_HEADS)]
            kv = lax.dot_general(v, k_tail, TN_DIMS, preferred_element_type=F32)
            w.append((r0, btot, v, st_old, o, a, kv))
        ab = [[jnp.where(tri[d], a_h, 0.0).astype(BF16) for a_h in w[d][5]] for d in range(2)]
        oh = [[jnp.dot(a_h, w[d][2][:, (h // 2) * PAIR_W:(h // 2 + 1) * PAIR_W], preferred_element_type=F32)
               for h, a_h in enumerate(ab[d])] for d in range(2)]
        for d in range(2):
            r0, btot, v, st_old, o, a, kv = w[d]
            blocks = []
            for p in range(VP // PAIR_W):
                heads = [h for h in range(GLA_HEADS) if h // 2 == p]
                blk = jnp.where(pair_first, oh[d][heads[0]], oh[d][heads[1]] if len(heads) > 1 else 0.0)
                blocks.append(blk)
            st_ref[d] = st_old * jnp.exp(btot) + jnp.where(st_mask, kv, 0.0)
            osum_ref[pl.ds(r0, L), :] += o + jnp.concatenate(blocks, axis=1)
        return carry

    lax.fori_loop(0, NC, body, 0)

    for r0 in range(0, T, RB):
        hs = osum_ref[r0:r0 + RB, :]
        y = hs * lax.rsqrt(_group_mean_sq(hs, HEAD_DIM) + EPS) * ng_ref[...]
        g = x_ref[r0:r0 + RB, 2 * KP + VP:2 * KP + 2 * VP].astype(F32)
        out_ref[r0:r0 + RB, :] = (y * (g * jax.nn.sigmoid(g))).astype(BF16)


def _gla(x, gates, w2, ab, ng, l, batch):
    n = x.shape[0]
    T = n // batch
    return pl.pallas_call(
        _gla_kernel,
        out_shape=jax.ShapeDtypeStruct((n, GLA_V_PAD), BF16),
        grid=(batch,),
        in_specs=[pl.BlockSpec((T, GLA_COLS), lambda b: (b, 0)),
                  pl.BlockSpec((T, GATE_COLS), lambda b: (b, 0)),
                  _layer_spec(w2, l), _layer_spec(ab, l), _layer_spec(ng, l)],
        out_specs=pl.BlockSpec((T, GLA_V_PAD), lambda b: (b, 0)),
        scratch_shapes=[pltpu.VMEM((T, 2 * GLA_K_PAD), F32),
                        pltpu.VMEM((T, GLA_V_PAD), F32),
                        pltpu.VMEM((2, GLA_V_PAD, GLA_K_PAD), F32)],
        compiler_params=_cparams("parallel"),
        name="gla",
    )(x, gates, w2, ab, ng)


NA_CLASSES = 8
NA_UNROLL = 8


def _na_kernel(q_ref, k_ref, v_ref, bias_ref, out_ref, *, n_heads_in_pair):
    T = q_ref.shape[0]
    W = GRID_W
    rows = T // W
    band = NA_KH * W
    lane = lax.broadcasted_iota(jnp.int32, (1, PAIR_W), 1)
    head_mask = (lane < HEAD_DIM, lane >= HEAD_DIM)
    scale = HEAD_DIM ** -0.5

    def group(gi, carry):
        items = []
        for u in range(NA_UNROLL):
            r = gi * NA_UNROLL + u
            rs = jnp.clip(r - NA_KH // 2, 0, rows - NA_KH)
            cls = jnp.where(r < NA_KH // 2, r, jnp.where(r > rows - NA_KH // 2, r - (rows - NA_KH), NA_KH // 2))
            q0 = pl.multiple_of(r * W, W)
            k0 = pl.multiple_of(rs * W, W)
            q = q_ref[pl.ds(q0, W), :]
            kb = k_ref[pl.ds(k0, band), :]
            for half in range(n_heads_in_pair):
                q_m = jnp.where(head_mask[half], q, jnp.zeros_like(q))
                s = lax.dot_general(q_m, kb, NT_DIMS, preferred_element_type=F32)
                items.append((u, half, q0, k0, cls, s))
        probs = []
        for u, half, q0, k0, cls, s in items:
            s = s * scale + bias_ref[half, cls]
            m = jnp.max(s, axis=1, keepdims=True)
            p = jnp.exp(s - m)
            l = jnp.sum(p, axis=1, keepdims=True)
            probs.append((u, half, q0, k0, p.astype(BF16), l))
        outs = {}
        for u, half, q0, k0, p, l in probs:
            vb = v_ref[pl.ds(k0, band), :]
            outs[(u, half)] = (q0, jnp.dot(p, vb, preferred_element_type=F32) / l)
        for u in range(NA_UNROLL):
            q0, o0 = outs[(u, 0)]
            second = outs[(u, 1)][1] if n_heads_in_pair == 2 else 0.0
            out_ref[pl.ds(q0, W), :] = jnp.where(head_mask[0], o0, second).astype(BF16)
        return carry

    lax.fori_loop(0, rows // NA_UNROLL, group, 0)


def _na(na, bias, l, batch):
    n = na.shape[0]
    T = n // batch
    npairs = NA_PAD // PAIR_W
    outs = []
    for p0, p1, nh in ((0, NA_HEADS // 2, 2), (NA_HEADS // 2, npairs, 1)):
        npp = p1 - p0
        outs.append(pl.pallas_call(
            functools.partial(_na_kernel, n_heads_in_pair=nh),
            out_shape=jax.ShapeDtypeStruct((n, npp * PAIR_W), BF16),
            grid=(npp, batch),
            in_specs=[pl.BlockSpec((T, PAIR_W), lambda p, b, o=p0: (b, o + p)),
                      pl.BlockSpec((T, PAIR_W), lambda p, b, o=npairs + p0: (b, o + p)),
                      pl.BlockSpec((T, PAIR_W), lambda p, b, o=2 * npairs + p0: (b, o + p)),
                      pl.BlockSpec((None, 2, NA_CLASSES, GRID_W, NA_KH * GRID_W),
                                   lambda p, b, o=p0: (l, o + p, 0, 0, 0))],
            out_specs=pl.BlockSpec((T, PAIR_W), lambda p, b: (b, p)),
            compiler_params=_cparams("parallel", "parallel"),
            name=f"natten{nh}",
        )(na, na, na, bias))
    return outs


def _na_bias_table(rpb, rows):
    kh = NA_KH
    rep_rows = np.array(list(range(kh // 2)) + [kh // 2] + list(range(rows - kh // 2 + 1, rows)))
    row_start = np.clip(rep_rows - kh // 2, 0, rows - kh)
    dr = row_start[:, None] + np.arange(kh)[None, :] - rep_rows[:, None] + (kh - 1)
    c = np.arange(GRID_W)
    col_start = np.clip(c - NA_KW // 2, 0, GRID_W - NA_KW)
    in_win = (c[None, :] >= col_start[:, None]) & (c[None, :] < col_start[:, None] + NA_KW)
    dc = np.clip(c[None, :] - c[:, None], -(NA_KW - 1), NA_KW - 1) + (NA_KW - 1)
    sel_r = jnp.asarray(np.eye(2 * kh - 1, dtype=np.float32)[dr])
    sel_c = jnp.asarray(np.eye(2 * NA_KW - 1, dtype=np.float32)[dc])
    depth, heads = rpb.shape[:2]
    rp = rpb.astype(F32).reshape(depth * heads, 2 * kh - 1, 2 * NA_KW - 1)
    tab = jnp.einsum('skr,hrc->hskc', sel_r, rp, precision=lax.Precision.HIGHEST)
    tab = jnp.einsum('hskc,qwc->hsqkw', tab, sel_c, precision=lax.Precision.HIGHEST)
    tab = jnp.where(in_win[None, None, :, None, :], tab, -jnp.inf)
    tab = tab.reshape(depth, heads, len(rep_rows), GRID_W, kh * GRID_W)
    return jnp.pad(tab, ((0, 0), (0, 2 * (NA_PAD // PAIR_W) - heads), (0, 0), (0, 0), (0, 0)))


def _memkv_kernel(m_ref, g_ref, wk_ref, wv_ref, k_ref, v_ref):
    mn = _rms(m_ref[...], g_ref[...]).astype(BF16)
    k_ref[...] = jnp.dot(mn, wk_ref[...], preferred_element_type=F32).astype(BF16)
    v_ref[...] = jnp.dot(mn, wv_ref[...], preferred_element_type=F32).astype(BF16)


def _memkv(mem, g, wk, wv, l):
    n, d = mem.shape
    tm = min(ROW_TILE, n)
    row = pl.BlockSpec((tm, d), lambda i: (i, 0))
    return pl.pallas_call(
        _memkv_kernel,
        out_shape=(jax.ShapeDtypeStruct((n, d), BF16), jax.ShapeDtypeStruct((n, d), BF16)),
        grid=(n // tm,),
        in_specs=[row, _layer_spec(g, l), _layer_spec(wk, l), _layer_spec(wv, l)],
        out_specs=(row, row),
        compiler_params=_cparams("parallel"),
        name="memkv",
    )(mem, g, wk, wv)


def _xattn_kernel(ml_ref, na0_ref, na1_ref, gla_ref, wout_ref, h_ref, g_ref, wq_ref, k_ref, v_ref, wo_ref,
                  o_ref):
    mixed = jnp.concatenate([ml_ref[...], na0_ref[...], na1_ref[...], gla_ref[...]], axis=1)
    o_ref[...] = h_ref[...] + jnp.dot(mixed, wout_ref[...], preferred_element_type=F32)
    h = o_ref[...]
    tm, d = h.shape
    hd = d // X_HEADS
    heads = [slice(i * hd, (i + 1) * hd) for i in range(X_HEADS)]
    blocks = _rms_row_blocks(o_ref, g_ref[...], tm)
    hn = jnp.concatenate(blocks, axis=0)
    q = [(_dot_row_blocks(blocks, wq_ref[:, sl]) if n == 0 else
          jnp.dot(hn, wq_ref[:, sl], preferred_element_type=F32)).astype(BF16) for n, sl in enumerate(heads)]
    scores = [lax.dot_general(q_h, k_ref[:, sl], NT_DIMS, preferred_element_type=F32)
              for q_h, sl in zip(q, heads)]
    probs = []
    for s in scores:
        s = s * (hd ** -0.5)
        p = jnp.exp(s - jnp.max(s, axis=1, keepdims=True))
        probs.append((p.astype(BF16), jnp.sum(p, axis=1, keepdims=True)))
    ctx = [jnp.dot(p, v_ref[:, sl], preferred_element_type=F32) for (p, _), sl in zip(probs, heads)]
    o = jnp.concatenate([(c / l).astype(BF16) for c, (_, l) in zip(ctx, probs)], axis=1)
    o_ref[...] = h + jnp.dot(o, wo_ref[...], preferred_element_type=F32)


def _xattn(mixed, wout, h, g, wq, k, v, wo, l, batch):
    n, d = h.shape
    tm = ROW_TILE
    tiles_per_seq = (n // batch) // tm
    n_mem = k.shape[0] // batch
    row = lambda cols: pl.BlockSpec((tm, cols), lambda i: (i, 0))
    kv = pl.BlockSpec((n_mem, d), lambda i: (i // tiles_per_seq, 0))
    return pl.pallas_call(
        _xattn_kernel,
        out_shape=jax.ShapeDtypeStruct((n, d), F32),
        grid=(n // tm,),
        in_specs=[row(m.shape[1]) for m in mixed] + [
            _layer_spec(wout, l), row(d), _layer_spec(g, l), _layer_spec(wq, l), kv, kv, _layer_spec(wo, l)],
        out_specs=row(d),
        compiler_params=_cparams("parallel"),
        name="xattn",
    )(*mixed, wout, h, g, wq, k, v, wo)


FFN_CHUNK = 256


def _gelu_tanh(x):
    c = float(np.sqrt(2.0 / np.pi))
    half = 0.5 * x
    return half + half * jnp.tanh(x * (c + (c * 0.044715) * (x * x)))


def _ffn_kernel(h_ref, hp_ref, hx_ref, g_ref, wup_ref, cw_ref, cb_ref, wdn_ref, gf_ref, o_ref, *,
                tiles_per_seq, final_norm):
    i = pl.program_id(0)
    h = h_ref[...]
    tm = h.shape[0]
    dff = wdn_ref.shape[0]
    halo = jnp.concatenate([hp_ref[...], hx_ref[...]], axis=0)
    blocks = _rms_row_blocks(h_ref, g_ref[...], tm) + [_rms(halo, g_ref[...]).astype(BF16)]
    hn = jnp.concatenate(blocks[:-1], axis=0)
    hn_ext = jnp.concatenate(blocks, axis=0)
    has_prev, has_next = _halo_flags(i, tiles_per_seq)

    def up(c):
        wa = wup_ref[:, c]
        wg = wup_ref[:, slice(dff + c.start, dff + c.stop)]
        return (jnp.dot(hn_ext, wa, preferred_element_type=F32), jnp.dot(hn, wg, preferred_element_type=F32))

    acts = []
    for c in [slice(c, min(c + FFN_CHUNK, dff)) for c in range(0, dff, FFN_CHUNK)]:
        a_ext, gate = up(c)
        ac = _conv3(a_ext, tm, cw_ref[:, c], cb_ref[:, c], has_prev, has_next)
        acts.append((_gelu_tanh(ac) * gate).astype(BF16))
    acc = h + jnp.dot(jnp.concatenate(acts, axis=1), wdn_ref[...], preferred_element_type=F32)
    o_ref[...] = _rms(acc, gf_ref[...]) if final_norm else acc


def _ffn(h, g, wup, cw, cb, wdn, gf, l, batch, final_norm):
    n, d = h.shape
    tm = ROW_TILE
    tiles_per_seq = (n // batch) // tm
    row = pl.BlockSpec((tm, d), lambda i: (i, 0))
    return pl.pallas_call(
        functools.partial(_ffn_kernel, tiles_per_seq=tiles_per_seq, final_norm=final_norm),
        out_shape=jax.ShapeDtypeStruct((n, d), F32),
        grid=(n // tm,),
        in_specs=[row] + _halo_specs(tm, d, n) + [
            _layer_spec(a, l) for a in (g, wup, cw, cb, wdn)] + [_const_spec((1, d))],
        out_specs=row,
        compiler_params=_cparams("parallel"),
        name="convffn",
    )(h, h, h, g, wup, cw, cb, wdn, gf)


def _pad_axis(a, axis, n, before=0):
    pads = [(0, 0)] * a.ndim
    pads[axis] = (before, n - before - a.shape[axis])
    return jnp.pad(a, pads)


def _prep_w_in(w):
    offs = np.concatenate([[0], np.cumsum(IN_SIZES)])
    seg = [w[..., offs[i]:offs[i + 1]] for i in range(len(IN_SIZES))]
    ml = seg[0:4]
    na = [_pad_axis(s, -1, NA_PAD) for s in seg[5:8]]
    gla = [_pad_axis(seg[8], -1, GLA_K_PAD), _pad_axis(seg[9], -1, GLA_K_PAD),
           _pad_axis(seg[10], -1, GLA_V_PAD), _pad_axis(seg[11], -1, GLA_V_PAD)]
    gates = [_pad_axis(seg[4][..., j * ML_HEADS:(j + 1) * ML_HEADS], -1, SUBLANES) for j in range(4)]
    gates = _pad_axis(jnp.concatenate(gates + [seg[12]], axis=-1), -1, GATE_COLS)
    return jnp.concatenate(ml + na + gla + [gates], axis=-1).astype(BF16)


def _prep_gate_bias(gb):
    groups = [_pad_axis(gb[:, j:j + 1, :], -1, SUBLANES) for j in range(4)]
    return _pad_axis(jnp.concatenate(groups, axis=-1), -1, GATE_COLS)


def _prep_gla_gate(w2, ab):
    width = 2 * GLA_K_PAD
    blocks = [_pad_axis(w2[:, z], -1, width, before=z * GLA_K_PAD) for z in range(2)]
    w = _pad_axis(jnp.concatenate(blocks, axis=1), 1, GATE_COLS, before=GLA_GATE_OFF)
    b = jnp.concatenate([_pad_axis(ab[:, z:z + 1], -1, GLA_K_PAD) for z in range(2)], axis=-1)
    return w.astype(BF16), b


def _prep_w_out(w):
    parts = [w[:, 0:ML_DIM], _pad_axis(w[:, ML_DIM:ML_DIM + NA_DIM], 1, NA_PAD),
             _pad_axis(w[:, ML_DIM + NA_DIM:], 1, GLA_V_PAD)]
    return jnp.concatenate(parts, axis=1).astype(BF16)


def kernel(x, mem, mix_norm_g, w_in, ml_conv_w, ml_conv_b, ml_gate_b, ml_norm_g, na_rpb,
           gla_a_w2, gla_a_b, gla_norm_g, w_out, xattn_norm_g, mem_norm_g,
           w_xq, w_xk, w_xv, w_xo, ffn_norm_g, w_up, ffn_conv_w, ffn_conv_b, w_down,
           final_norm_g):
    batch, seq, d = x.shape
    depth = w_in.shape[0]
    rows = seq // GRID_W
    assert seq % ROW_TILE == 0 and rows >= NA_KH and rows % NA_UNROLL == 0
    h = x.reshape(batch * seq, d)
    mem2 = mem.reshape(batch * mem.shape[1], d)
    row3 = lambda v: v[:, None, :]
    w_in_p = _prep_w_in(w_in)
    gate_b_p = _prep_gate_bias(ml_gate_b)
    na_bias_p = _na_bias_table(na_rpb, rows)
    gla_w2_p, gla_ab_p = _prep_gla_gate(gla_a_w2, gla_a_b)
    gla_ng_p = row3(_pad_axis(gla_norm_g, -1, GLA_V_PAD))
    w_out_p = _prep_w_out(w_out)
    w_xq_b, w_xk_b, w_xv_b, w_xo_b, w_up_b, w_down_b = (
        w.astype(BF16) for w in (w_xq, w_xk, w_xv, w_xo, w_up, w_down))
    mix_g, ml_cb, ml_ng, x_g, mem_g, ffn_g, ffn_cb = (
        row3(v) for v in (mix_norm_g, ml_conv_b, ml_norm_g, xattn_norm_g, mem_norm_g, ffn_norm_g, ffn_conv_b))
    for l in range(depth):
        ktok, otok, q_t, v_t, gs_t, na, gla, gates = _inproj(
            h, mix_g, w_in_p, ml_conv_w, ml_cb, gate_b_p, l, batch)
        ml_out = _mlstm(ktok, otok, q_t, v_t, gs_t, ml_ng, l, batch)
        na0, na1 = _na(na, na_bias_p, l, batch)
        gla_out = _gla(gla, gates, gla_w2_p, gla_ab_p, gla_ng_p, l, batch)
        k, v = _memkv(mem2, mem_g, w_xk_b, w_xv_b, l)
        h = _xattn((ml_out, na0, na1, gla_out), w_out_p, h, x_g, w_xq_b, k, v, w_xo_b, l, batch)
        h = _ffn(h, ffn_g, w_up_b, ffn_conv_w, ffn_cb, w_down_b, final_norm_g.reshape(1, d), l, batch,
                 final_norm=(l == depth - 1))
    return h.reshape(batch, seq, d)
```

```python
import functools

import numpy as np
import jax
import jax.numpy as jnp
from jax import lax
from jax.experimental import pallas as pl
from jax.experimental.pallas import tpu as pltpu

F32 = jnp.float32
BF16 = jnp.bfloat16

HEAD_DIM = 64
ML_HEADS = 6
NA_HEADS = 5
GLA_HEADS = 5
ML_DIM = ML_HEADS * HEAD_DIM
NA_DIM = NA_HEADS * HEAD_DIM
GLA_DK = 32
GLA_KDIM = GLA_HEADS * GLA_DK
GLA_VDIM = GLA_HEADS * HEAD_DIM
GLA_RANK = 16
GLA_TAU = 16.0
GRID_W = 64
NA_KH = 8
NA_KW = 16
ML_CHUNK = 128
GLA_CHUNK = 64
X_HEADS = 4
EPS = 1e-6
IN_SIZES = (ML_DIM, ML_DIM, ML_DIM, ML_DIM, 4 * ML_HEADS,
            NA_DIM, NA_DIM, NA_DIM,
            GLA_KDIM, GLA_KDIM, GLA_VDIM, GLA_VDIM, 2 * GLA_RANK)

LANES = 128
SUBLANES = 8

PAIR_W = 2 * HEAD_DIM
ML_COLS = 4 * ML_DIM
NA_PAD = 3 * PAIR_W
NA_COLS = 3 * NA_PAD
GLA_K_PAD = 256
GLA_V_PAD = 3 * PAIR_W
GLA_COLS = 2 * GLA_K_PAD + 2 * GLA_V_PAD
GATE_COLS = LANES
GLA_GATE_OFF = 4 * SUBLANES
IN_COLS_PAD = ML_COLS + NA_COLS + GLA_COLS + GATE_COLS

ROW_TILE = 512
WIDE_ROW_TILE = 1024
NORM_BLOCK = 128
VMEM_LIMIT = 56 * 1024 * 1024

NT_DIMS = (((1,), (1,)), ((), ()))
TN_DIMS = (((0,), (0,)), ((), ()))


def _cparams(*sem):
    return pltpu.CompilerParams(dimension_semantics=sem, vmem_limit_bytes=VMEM_LIMIT)


def _const_spec(shape):
    nd = len(shape)
    return pl.BlockSpec(shape, lambda *_: (0,) * nd)


def _layer_spec(a, l):
    nd = a.ndim - 1
    return pl.BlockSpec((None,) + a.shape[1:], lambda *_, l=l, nd=nd: (l,) + (0,) * nd)


def _rms(x, g):
    ms = jnp.mean(x * x, axis=-1, keepdims=True)
    return x * lax.rsqrt(ms + EPS) * g


def _rms_row_blocks(x_ref, g, tm):
    return [_rms(x_ref[r0:r0 + NORM_BLOCK, :], g).astype(BF16) for r0 in range(0, tm, NORM_BLOCK)]


def _dot_row_blocks(blocks, w):
    return jnp.concatenate([jnp.dot(b, w, preferred_element_type=F32) for b in blocks], axis=0)


def _log_sigmoid(x):
    return jnp.minimum(x, 0.0) - jnp.log(1.0 + jnp.exp(-jnp.abs(x)))


def _split_bf16(x):
    hi = x.astype(BF16)
    lo = (x - hi.astype(F32)).astype(BF16)
    return hi, lo


def _group_mean_sq(x, group):
    n = x.shape[-1]
    r = lax.broadcasted_iota(jnp.int32, (n, n), 0) // group
    c = lax.broadcasted_iota(jnp.int32, (n, n), 1) // group
    ones = jnp.where(r == c, 1.0, 0.0).astype(BF16)
    hi, lo = _split_bf16(x * x)
    s = jnp.dot(hi, ones, preferred_element_type=F32) + jnp.dot(lo, ones, preferred_element_type=F32)
    return s * (1.0 / group)


def _halo_flags(i, tiles_per_seq):
    has_prev = (i % tiles_per_seq != 0).astype(F32)
    has_next = (i % tiles_per_seq != tiles_per_seq - 1).astype(F32)
    return has_prev, has_next


def _conv3(a_ext, tm, w, b, has_prev, has_next):
    row = lax.broadcasted_iota(jnp.int32, (SUBLANES, 1), 0)
    a = a_ext[0:tm]
    down = pltpu.roll(a, 1, 0)
    up = pltpu.roll(a, tm - 1, 0)
    prev_row = a_ext[tm + SUBLANES - 1:tm + SUBLANES, :] * has_prev
    next_row = a_ext[tm + SUBLANES:tm + SUBLANES + 1, :] * has_next
    a_prev = jnp.concatenate([jnp.where(row == 0, prev_row, down[0:SUBLANES]), down[SUBLANES:]], axis=0)
    a_next = jnp.concatenate([up[:tm - SUBLANES],
                              jnp.where(row == SUBLANES - 1, next_row, up[tm - SUBLANES:])], axis=0)
    return a_prev * w[0:1, :] + a * w[1:2, :] + a_next * w[2:3, :] + b


def _inproj_kernel(h_ref, hp_ref, hx_ref, g_ref, w_ref, cw_ref, cb_ref, gb_ref,
                   ktok_ref, otok_ref, qT_ref, vT_ref, gsT_ref, na_ref, gla_ref, gate_ref, *, tiles_per_seq):
    tm = h_ref.shape[0]
    L = ML_CHUNK
    has_prev, has_next = _halo_flags(pl.program_id(0), tiles_per_seq)
    halo = jnp.concatenate([hp_ref[...], hx_ref[...]], axis=0)
    blocks = _rms_row_blocks(h_ref, g_ref[...], tm) + [_rms(halo, g_ref[...]).astype(BF16)]
    xn = jnp.concatenate(blocks[:-1], axis=0)
    xn_ext = jnp.concatenate(blocks, axis=0)

    def proj(lhs, c0, c1):
        return jnp.dot(lhs, w_ref[:, c0:c1], preferred_element_type=F32)

    for part in range(2):
        cols = slice(part * ML_DIM, (part + 1) * ML_DIM)
        a_ext = _dot_row_blocks(blocks, w_ref[:, cols]) if part == 0 else proj(xn_ext, cols.start, cols.stop)
        y = _conv3(a_ext, tm, cw_ref[:, cols], cb_ref[:, cols], has_prev, has_next)
        y = y * jax.nn.sigmoid(y)
        if part == 0:
            y_t = y.T
            for j in range(tm // L):
                qT_ref[j] = y_t[:, j * L:(j + 1) * L].astype(BF16)
        else:
            ktok_ref[...] = (y * HEAD_DIM ** -0.5).astype(BF16)
    v_t = proj(xn, 2 * ML_DIM, 3 * ML_DIM).T
    for j in range(tm // L):
        vT_ref[j] = v_t[:, j * L:(j + 1) * L].astype(BF16)
    otok_ref[...] = proj(xn, 3 * ML_DIM, 4 * ML_DIM).astype(BF16)
    off = ML_COLS
    for ref, chunk in ((na_ref, 384), (gla_ref, 256), (gate_ref, 128)):
        width = ref.shape[-1]
        for c in range(0, width, chunk):
            ref[:, c:c + chunk] = proj(xn, off + c, off + c + chunk).astype(ref.dtype)
        off += width
    g_t = (gate_ref[...] + gb_ref[...]).T
    for j in range(tm // L):
        gsT_ref[j] = g_t[0:4 * SUBLANES, j * L:(j + 1) * L]


def _halo_specs(tm, d, n):
    hb = tm // SUBLANES
    last = n // SUBLANES - 1
    return [pl.BlockSpec((SUBLANES, d), lambda i: (jnp.maximum(i * hb - 1, 0), 0)),
            pl.BlockSpec((SUBLANES, d), lambda i: (jnp.minimum((i + 1) * hb, last), 0))]


def _inproj(h, g, w, cw, cb, gb, l, batch):
    n, d = h.shape
    tm = WIDE_ROW_TILE
    nc = tm // ML_CHUNK
    row = lambda cols: pl.BlockSpec((tm, cols), lambda i: (i, 0))
    slab = lambda rows: pl.BlockSpec((nc, rows, ML_CHUNK), lambda i: (i, 0, 0))
    return pl.pallas_call(
        functools.partial(_inproj_kernel, tiles_per_seq=(n // batch) // tm),
        out_shape=(jax.ShapeDtypeStruct((n, ML_DIM), BF16),
                   jax.ShapeDtypeStruct((n, ML_DIM), BF16),
                   jax.ShapeDtypeStruct((n // ML_CHUNK, ML_DIM, ML_CHUNK), BF16),
                   jax.ShapeDtypeStruct((n // ML_CHUNK, ML_DIM, ML_CHUNK), BF16),
                   jax.ShapeDtypeStruct((n // ML_CHUNK, 4 * SUBLANES, ML_CHUNK), F32),
                   jax.ShapeDtypeStruct((n, NA_COLS), BF16),
                   jax.ShapeDtypeStruct((n, GLA_COLS), BF16),
                   jax.ShapeDtypeStruct((n, GATE_COLS), F32)),
        grid=(n // tm,),
        in_specs=[row(d)] + _halo_specs(tm, d, n) + [_layer_spec(a, l) for a in (g, w, cw, cb, gb)],
        out_specs=(row(ML_DIM), row(ML_DIM), slab(ML_DIM), slab(ML_DIM), slab(4 * SUBLANES),
                   row(NA_COLS), row(GLA_COLS), row(GATE_COLS)),
        compiler_params=_cparams("parallel"),
        name="inproj",
    )(h, h, h, g, w, cw, cb, gb)


def _lane_scan(x, op, fill, reverse):
    n = x.shape[-1]
    lane = lax.broadcasted_iota(jnp.int32, x.shape, 1)
    s = 1
    while s < n:
        if reverse:
            x = op(x, jnp.where(lane < n - s, pltpu.roll(x, n - s, 1), fill))
        else:
            x = op(x, jnp.where(lane >= s, pltpu.roll(x, s, 1), fill))
        s *= 2
    return x


def _mlstm_kernel(ktok_ref, otok_ref, qT_ref, vT_ref, gsT_ref, ng_ref, out_ref,
                  pre_ref, acol_ref, hsumT_ref, ct_ref):
    NC, _, L = qT_ref.shape

    def gate_group(grp):
        return gsT_ref[:, grp * SUBLANES:(grp + 1) * SUBLANES, :].reshape(NC * SUBLANES, L)

    a_all = []
    for d in range(2):
        logf = _log_sigmoid(gate_group(2 * d + 1))
        b = _lane_scan(logf, jnp.add, 0.0, reverse=(d == 1))
        a = gate_group(2 * d) - b
        pre_ref[d, 0] = b
        pre_ref[d, 1] = a
        pre_ref[d, 2] = _lane_scan(a, jnp.maximum, -jnp.inf, reverse=(d == 1))
        pre_ref[d, 3] = jnp.broadcast_to(jnp.sum(logf, axis=1, keepdims=True), a.shape)
        pre_ref[d, 4] = jnp.broadcast_to(jnp.max(a, axis=1, keepdims=True), a.shape)
        a_all.append(a)
    a_pad = jnp.zeros((L - 2 * SUBLANES, L), F32)
    for c in range(NC):
        rows = slice(c * SUBLANES, (c + 1) * SUBLANES)
        acol_ref[c] = jnp.concatenate([a_all[0][rows], a_all[1][rows], a_pad], axis=0).T

    hsumT_ref[...] = jnp.zeros(hsumT_ref.shape, F32)
    ct_ref[...] = jnp.zeros(ct_ref.shape, F32)

    sub = lax.broadcasted_iota(jnp.int32, (PAIR_W, L), 0)
    head_rows = (sub < HEAD_DIM, sub >= HEAD_DIM)
    ri = lax.broadcasted_iota(jnp.int32, (L, L), 0)
    ci = lax.broadcasted_iota(jnp.int32, (L, L), 1)
    tri = (ri <= ci, ri >= ci)

    def body(it, carry):
        m_reps = list(carry)
        chunk = (it, NC - 1 - it)
        gq = []
        for d in range(2):
            r8 = pl.ds(pl.multiple_of(chunk[d] * SUBLANES, SUBLANES), SUBLANES)
            b, a, amax_run, g, amax = (pre_ref[d, n, r8, :] for n in range(5))
            m_old = m_reps[d]
            big_m = jnp.maximum(m_old, amax_run)
            m_new = jnp.maximum(g + m_old, g + amax)
            m_reps[d] = m_new
            gq.append(dict(big_m=big_m, w_inter=jnp.exp(m_old - big_m), e=jnp.exp(-(b + big_m)),
                           wk=jnp.exp(g + a - m_new), decay=jnp.exp(g + m_old - m_new),
                           a_cols=acol_ref[chunk[d]]))

        items = []
        for d in range(2):
            c = chunk[d]
            r0 = pl.multiple_of(c * L, L)
            for p in range(ML_HEADS // 2):
                rows = slice(p * PAIR_W, (p + 1) * PAIR_W)
                k_p = ktok_ref[pl.ds(r0, L), rows]
                q_tp = qT_ref[c, rows, :]
                v_tp = vT_ref[c, rows, :]
                for half in range(2):
                    h = 2 * p + half
                    q_tm = jnp.where(head_rows[half], q_tp, jnp.zeros_like(q_tp))
                    v_ta = jnp.where(head_rows[half], v_tp, jnp.ones_like(v_tp))
                    ct_old = ct_ref[d * ML_HEADS + h]
                    s_t = jnp.dot(k_p, q_tm, preferred_element_type=F32)
                    inter_t = jnp.dot(ct_old.astype(BF16), q_tm, preferred_element_type=F32)
                    v_tw = (v_ta.astype(F32) * gq[d]["wk"][h:h + 1, :]).astype(BF16)
                    kv_t = jnp.dot(v_tw, k_p, preferred_element_type=F32)
                    items.append((d, c, p, half, h, v_ta, ct_old, s_t, inter_t, kv_t))
        s2 = []
        for d, c, p, half, h, v_ta, ct_old, s_t, inter_t, kv_t in items:
            dm = jnp.where(tri[d], gq[d]["a_cols"][:, SUBLANES * d + h:SUBLANES * d + h + 1]
                           - gq[d]["big_m"][h:h + 1, :], -jnp.inf)
            s2.append((s_t * jnp.exp(dm)).astype(BF16))
        num = [jnp.dot(it_[5], s, preferred_element_type=F32) for it_, s in zip(items, s2)]
        for idx in range(0, len(items), 2):
            outs = []
            for (d, c, p, half, h, v_ta, ct_old, s_t, inter_t, kv_t), n_t in zip(items[idx:idx + 2],
                                                                                 num[idx:idx + 2]):
                n_t = n_t + gq[d]["w_inter"][h:h + 1, :] * inter_t
                den = jnp.concatenate([n_t[HEAD_DIM:], n_t[:HEAD_DIM]], axis=0)
                outs.append(n_t / jnp.maximum(jnp.abs(den), gq[d]["e"][h:h + 1, :]))
                ct_ref[d * ML_HEADS + h] = gq[d]["decay"][h:h + 1, :] * ct_old + kv_t
            hsumT_ref[c, p * PAIR_W:(p + 1) * PAIR_W, :] += jnp.where(head_rows[0], outs[0], outs[1])
        return tuple(m_reps)

    m0 = jnp.zeros((SUBLANES, L), F32)
    lax.fori_loop(0, NC, body, (m0, m0))

    for c in range(NC):
        hs_t = hsumT_ref[c]
        parts = []
        for h in range(ML_HEADS):
            blk = hs_t[h * HEAD_DIM:(h + 1) * HEAD_DIM, :]
            ms = jnp.mean(blk * blk, axis=0, keepdims=True)
            parts.append(blk * lax.rsqrt(ms + EPS))
        y = jnp.concatenate(parts, axis=0).T * ng_ref[...]
        o = otok_ref[c * L:(c + 1) * L, :].astype(F32)
        out_ref[c * L:(c + 1) * L, :] = (y * jax.nn.sigmoid(o)).astype(BF16)


def _mlstm(ktok, otok, q_t, v_t, gs_t, ng, l, batch):
    n = ktok.shape[0]
    T = n // batch
    nc = T // ML_CHUNK
    tok = pl.BlockSpec((T, ML_DIM), lambda b: (b, 0))
    slab = lambda rows: pl.BlockSpec((nc, rows, ML_CHUNK), lambda b: (b, 0, 0))
    return pl.pallas_call(
        _mlstm_kernel,
        out_shape=jax.ShapeDtypeStruct((n, ML_DIM), BF16),
        grid=(batch,),
        in_specs=[tok, tok, slab(ML_DIM), slab(ML_DIM), slab(4 * SUBLANES), _layer_spec(ng, l)],
        out_specs=tok,
        scratch_shapes=[pltpu.VMEM((2, 5, nc * SUBLANES, ML_CHUNK), F32),
                        pltpu.VMEM((nc, ML_CHUNK, LANES), F32),
                        pltpu.VMEM((nc, ML_DIM, ML_CHUNK), F32),
                        pltpu.VMEM((2 * ML_HEADS, PAIR_W, PAIR_W), F32)],
        compiler_params=_cparams("parallel"),
        name="mlstm",
    )(ktok, otok, q_t, v_t, gs_t, ng)


def _gla_kernel(x_ref, gates_ref, w2_ref, ab_ref, ng_ref, out_ref, bc_ref, osum_ref, st_ref):
    T = x_ref.shape[0]
    L = GLA_CHUNK
    NC = T // L
    KP, VP = GLA_K_PAD, GLA_V_PAD
    RB = 256

    ri = lax.broadcasted_iota(jnp.int32, (RB, RB), 0)
    ci = lax.broadcasted_iota(jnp.int32, (RB, RB), 1)
    same = (ri // L) == (ci // L)
    tri_blk = (jnp.where(same & (ci <= ri), 1.0, 0.0).astype(BF16),
               jnp.where(same & (ci >= ri), 1.0, 0.0).astype(BF16))
    for r0 in range(0, T, RB):
        z = jnp.dot(gates_ref[r0:r0 + RB, :].astype(BF16), w2_ref[...],
                    preferred_element_type=F32) + ab_ref[...]
        hi, lo = _split_bf16(_log_sigmoid(z) * (1.0 / GLA_TAU))
        for d in range(2):
            sl = slice(d * KP, (d + 1) * KP)
            bc_ref[r0:r0 + RB, sl] = (jnp.dot(tri_blk[d], hi[:, sl], preferred_element_type=F32)
                                      + jnp.dot(tri_blk[d], lo[:, sl], preferred_element_type=F32))

    osum_ref[...] = jnp.zeros(osum_ref.shape, F32)
    st_ref[...] = jnp.zeros(st_ref.shape, F32)

    ri = lax.broadcasted_iota(jnp.int32, (L, L), 0)
    ci = lax.broadcasted_iota(jnp.int32, (L, L), 1)
    tri = (ci <= ri, ci >= ri)
    qhead = lax.broadcasted_iota(jnp.int32, (1, KP), 1) // GLA_DK
    pair_first = lax.broadcasted_iota(jnp.int32, (1, PAIR_W), 1) < HEAD_DIM
    st_mask = (lax.broadcasted_iota(jnp.int32, (VP, KP), 0) // HEAD_DIM
               == lax.broadcasted_iota(jnp.int32, (VP, KP), 1) // GLA_DK)

    def body(it, carry):
        w = []
        for d in range(2):
            for u in range(GLA_STEPS):
                c = it * GLA_STEPS + u
                r0 = pl.multiple_of((c if d == 0 else NC - 1 - c) * L, L)
                bc = bc_ref[pl.ds(r0, L), d * KP:(d + 1) * KP]
                btot = bc[L - 1:L, :] if d == 0 else bc[0:1, :]
                q = x_ref[pl.ds(r0, L), 0:KP].astype(F32) * (GLA_DK ** -0.5)
                k = x_ref[pl.ds(r0, L), KP:2 * KP].astype(F32)
                v = x_ref[pl.ds(r0, L), 2 * KP:2 * KP + VP]
                q_dec = q * jnp.exp(bc)
                k_dec = (k * jnp.exp(-bc)).astype(BF16)
                k_tail = (k * jnp.exp(btot - bc)).astype(BF16)
                a = [lax.dot_general(jnp.where(qhead == h, q_dec, 0.0).astype(BF16), k_dec, NT_DIMS,
                                     preferred_element_type=F32) for h in range(GLA_HEADS)]
                kv = lax.dot_general(v, k_tail, TN_DIMS, preferred_element_type=F32)
                w.append((d, r0, btot, v, q_dec.astype(BF16), a, kv))
        intra = []
        for d, r0, btot, v, q_dec, a, kv in w:
            ab = [jnp.where(tri[d], a_h, 0.0).astype(BF16) for a_h in a]
            oh = [jnp.dot(a_h, v[:, (h // 2) * PAIR_W:(h // 2 + 1) * PAIR_W], preferred_element_type=F32)
                  for h, a_h in enumerate(ab)]
            blocks = []
            for p in range(VP // PAIR_W):
                heads = [h for h in range(GLA_HEADS) if h // 2 == p]
                blocks.append(jnp.where(pair_first, oh[heads[0]], oh[heads[1]] if len(heads) > 1 else 0.0))
            intra.append(jnp.concatenate(blocks, axis=1))
        for d in range(2):
            st = st_ref[d]
            for (_, r0, btot, v, q_dec, a, kv), o_intra in zip(w[d * GLA_STEPS:(d + 1) * GLA_STEPS],
                                                               intra[d * GLA_STEPS:(d + 1) * GLA_STEPS]):
                o = lax.dot_general(q_dec, st.astype(BF16), NT_DIMS, preferred_element_type=F32)
                osum_ref[pl.ds(r0, L), :] += o + o_intra
                st = st * jnp.exp(btot) + jnp.where(st_mask, kv, 0.0)
            st_ref[d] = st
        return carry

    lax.fori_loop(0, NC // GLA_STEPS, body, 0)

    for r0 in range(0, T, RB):
        hs = osum_ref[r0:r0 + RB, :]
        y = hs * lax.rsqrt(_group_mean_sq(hs, HEAD_DIM) + EPS) * ng_ref[...]
        g = x_ref[r0:r0 + RB, 2 * KP + VP:2 * KP + 2 * VP].astype(F32)
        out_ref[r0:r0 + RB, :] = (y * (g * jax.nn.sigmoid(g))).astype(BF16)


def _gla(x, gates, w2, ab, ng, l, batch):
    n = x.shape[0]
    T = n // batch
    return pl.pallas_call(
        _gla_kernel,
        out_shape=jax.ShapeDtypeStruct((n, GLA_V_PAD), BF16),
        grid=(batch,),
        in_specs=[pl.BlockSpec((T, GLA_COLS), lambda b: (b, 0)),
                  pl.BlockSpec((T, GATE_COLS), lambda b: (b, 0)),
                  _layer_spec(w2, l), _layer_spec(ab, l), _layer_spec(ng, l)],
        out_specs=pl.BlockSpec((T, GLA_V_PAD), lambda b: (b, 0)),
        scratch_shapes=[pltpu.VMEM((T, 2 * GLA_K_PAD), F32),
                        pltpu.VMEM((T, GLA_V_PAD), F32),
                        pltpu.VMEM((2, GLA_V_PAD, GLA_K_PAD), F32)],
        compiler_params=_cparams("parallel"),
        name="gla",
    )(x, gates, w2, ab, ng)


NA_CLASSES = 8
GLA_STEPS = 2
NA_UNROLL = 8


def _na_kernel(q_ref, k_ref, v_ref, bias_ref, out_ref, *, n_heads_in_pair):
    T = q_ref.shape[0]
    W = GRID_W
    rows = T // W
    band = NA_KH * W
    lane = lax.broadcasted_iota(jnp.int32, (1, PAIR_W), 1)
    head_mask = (lane < HEAD_DIM, lane >= HEAD_DIM)
    scale = HEAD_DIM ** -0.5

    def group(gi, carry):
        items = []
        for u in range(NA_UNROLL):
            r = gi * NA_UNROLL + u
            rs = jnp.clip(r - NA_KH // 2, 0, rows - NA_KH)
            cls = jnp.where(r < NA_KH // 2, r, jnp.where(r > rows - NA_KH // 2, r - (rows - NA_KH), NA_KH // 2))
            q0 = pl.multiple_of(r * W, W)
            k0 = pl.multiple_of(rs * W, W)
            q = q_ref[pl.ds(q0, W), :]
            kb = k_ref[pl.ds(k0, band), :]
            for half in range(n_heads_in_pair):
                q_m = jnp.where(head_mask[half], q, jnp.zeros_like(q))
                s = lax.dot_general(q_m, kb, NT_DIMS, preferred_element_type=F32)
                items.append((u, half, q0, k0, cls, s))
        probs = []
        for u, half, q0, k0, cls, s in items:
            s = s * scale + bias_ref[half, cls]
            m = jnp.max(s, axis=1, keepdims=True)
            p = jnp.exp(s - m)
            l = jnp.sum(p, axis=1, keepdims=True)
            probs.append((u, half, q0, k0, p.astype(BF16), l))
        outs = {}
        for u, half, q0, k0, p, l in probs:
            vb = v_ref[pl.ds(k0, band), :]
            outs[(u, half)] = (q0, jnp.dot(p, vb, preferred_element_type=F32) / l)
        for u in range(NA_UNROLL):
            q0, o0 = outs[(u, 0)]
            second = outs[(u, 1)][1] if n_heads_in_pair == 2 else 0.0
            out_ref[pl.ds(q0, W), :] = jnp.where(head_mask[0], o0, second).astype(BF16)
        return carry

    lax.fori_loop(0, rows // NA_UNROLL, group, 0)


def _na(na, bias, l, batch):
    n = na.shape[0]
    T = n // batch
    npairs = NA_PAD // PAIR_W
    outs = []
    for p0, p1, nh in ((0, NA_HEADS // 2, 2), (NA_HEADS // 2, npairs, 1)):
        npp = p1 - p0
        outs.append(pl.pallas_call(
            functools.partial(_na_kernel, n_heads_in_pair=nh),
            out_shape=jax.ShapeDtypeStruct((n, npp * PAIR_W), BF16),
            grid=(npp, batch),
            in_specs=[pl.BlockSpec((T, PAIR_W), lambda p, b, o=p0: (b, o + p)),
                      pl.BlockSpec((T, PAIR_W), lambda p, b, o=npairs + p0: (b, o + p)),
                      pl.BlockSpec((T, PAIR_W), lambda p, b, o=2 * npairs + p0: (b, o + p)),
                      pl.BlockSpec((None, 2, NA_CLASSES, GRID_W, NA_KH * GRID_W),
                                   lambda p, b, o=p0: (l, o + p, 0, 0, 0))],
            out_specs=pl.BlockSpec((T, PAIR_W), lambda p, b: (b, p)),
            compiler_params=_cparams("parallel", "parallel"),
            name=f"natten{nh}",
        )(na, na, na, bias))
    return outs


def _na_bias_table(rpb, rows):
    kh = NA_KH
    rep_rows = np.array(list(range(kh // 2)) + [kh // 2] + list(range(rows - kh // 2 + 1, rows)))
    row_start = np.clip(rep_rows - kh // 2, 0, rows - kh)
    dr = row_start[:, None] + np.arange(kh)[None, :] - rep_rows[:, None] + (kh - 1)
    c = np.arange(GRID_W)
    col_start = np.clip(c - NA_KW // 2, 0, GRID_W - NA_KW)
    in_win = (c[None, :] >= col_start[:, None]) & (c[None, :] < col_start[:, None] + NA_KW)
    dc = np.clip(c[None, :] - c[:, None], -(NA_KW - 1), NA_KW - 1) + (NA_KW - 1)
    sel_r = jnp.asarray(np.eye(2 * kh - 1, dtype=np.float32)[dr])
    sel_c = jnp.asarray(np.eye(2 * NA_KW - 1, dtype=np.float32)[dc])
    depth, heads = rpb.shape[:2]
    rp = rpb.astype(F32).reshape(depth * heads, 2 * kh - 1, 2 * NA_KW - 1)
    tab = jnp.einsum('skr,hrc->hskc', sel_r, rp, precision=lax.Precision.HIGHEST)
    tab = jnp.einsum('hskc,qwc->hsqkw', tab, sel_c, precision=lax.Precision.HIGHEST)
    tab = jnp.where(in_win[None, None, :, None, :], tab, -jnp.inf)
    tab = tab.reshape(depth, heads, len(rep_rows), GRID_W, kh * GRID_W)
    return jnp.pad(tab, ((0, 0), (0, 2 * (NA_PAD // PAIR_W) - heads), (0, 0), (0, 0), (0, 0)))


def _memkv_kernel(m_ref, g_ref, wk_ref, wv_ref, k_ref, v_ref):
    mn = _rms(m_ref[...], g_ref[...]).astype(BF16)
    k_ref[...] = jnp.dot(mn, wk_ref[...], preferred_element_type=F32).astype(BF16)
    v_ref[...] = jnp.dot(mn, wv_ref[...], preferred_element_type=F32).astype(BF16)


def _memkv(mem, g, wk, wv, l):
    n, d = mem.shape
    tm = min(ROW_TILE, n)
    row = pl.BlockSpec((tm, d), lambda i: (i, 0))
    return pl.pallas_call(
        _memkv_kernel,
        out_shape=(jax.ShapeDtypeStruct((n, d), BF16), jax.ShapeDtypeStruct((n, d), BF16)),
        grid=(n // tm,),
        in_specs=[row, _layer_spec(g, l), _layer_spec(wk, l), _layer_spec(wv, l)],
        out_specs=(row, row),
        compiler_params=_cparams("parallel"),
        name="memkv",
    )(mem, g, wk, wv)


def _xattn_kernel(ml_ref, na0_ref, na1_ref, gla_ref, wout_ref, h_ref, g_ref, wq_ref, k_ref, v_ref, wo_ref,
                  o_ref):
    mixed = jnp.concatenate([ml_ref[...], na0_ref[...], na1_ref[...], gla_ref[...]], axis=1)
    o_ref[...] = h_ref[...] + jnp.dot(mixed, wout_ref[...], preferred_element_type=F32)
    h = o_ref[...]
    tm, d = h.shape
    hd = d // X_HEADS
    heads = [slice(i * hd, (i + 1) * hd) for i in range(X_HEADS)]
    blocks = _rms_row_blocks(o_ref, g_ref[...], tm)
    hn = jnp.concatenate(blocks, axis=0)
    q = [(_dot_row_blocks(blocks, wq_ref[:, sl]) if n == 0 else
          jnp.dot(hn, wq_ref[:, sl], preferred_element_type=F32)).astype(BF16) for n, sl in enumerate(heads)]
    scores = [lax.dot_general(q_h, k_ref[:, sl], NT_DIMS, preferred_element_type=F32)
              for q_h, sl in zip(q, heads)]
    probs = []
    for s in scores:
        s = s * (hd ** -0.5)
        p = jnp.exp(s - jnp.max(s, axis=1, keepdims=True))
        probs.append((p.astype(BF16), jnp.sum(p, axis=1, keepdims=True)))
    ctx = [jnp.dot(p, v_ref[:, sl], preferred_element_type=F32) for (p, _), sl in zip(probs, heads)]
    o = jnp.concatenate([(c / l).astype(BF16) for c, (_, l) in zip(ctx, probs)], axis=1)
    o_ref[...] = h + jnp.dot(o, wo_ref[...], preferred_element_type=F32)


def _xattn(mixed, wout, h, g, wq, k, v, wo, l, batch):
    n, d = h.shape
    tm = WIDE_ROW_TILE
    tiles_per_seq = (n // batch) // tm
    n_mem = k.shape[0] // batch
    row = lambda cols: pl.BlockSpec((tm, cols), lambda i: (i, 0))
    kv = pl.BlockSpec((n_mem, d), lambda i: (i // tiles_per_seq, 0))
    return pl.pallas_call(
        _xattn_kernel,
        out_shape=jax.ShapeDtypeStruct((n, d), F32),
        grid=(n // tm,),
        in_specs=[row(m.shape[1]) for m in mixed] + [
            _layer_spec(wout, l), row(d), _layer_spec(g, l), _layer_spec(wq, l), kv, kv, _layer_spec(wo, l)],
        out_specs=row(d),
        compiler_params=_cparams("parallel"),
        name="xattn",
    )(*mixed, wout, h, g, wq, k, v, wo)


FFN_CHUNK = 256


def _gelu_tanh(x):
    c = float(np.sqrt(2.0 / np.pi))
    half = 0.5 * x
    return half + half * jnp.tanh(x * (c + (c * 0.044715) * (x * x)))


def _ffn_kernel(h_ref, hp_ref, hx_ref, g_ref, wup_ref, cw_ref, cb_ref, wdn_ref, gf_ref, o_ref, *,
                tiles_per_seq, final_norm):
    i = pl.program_id(0)
    h = h_ref[...]
    tm = h.shape[0]
    dff = wdn_ref.shape[0]
    halo = jnp.concatenate([hp_ref[...], hx_ref[...]], axis=0)
    blocks = _rms_row_blocks(h_ref, g_ref[...], tm) + [_rms(halo, g_ref[...]).astype(BF16)]
    hn = jnp.concatenate(blocks[:-1], axis=0)
    hn_ext = jnp.concatenate(blocks, axis=0)
    has_prev, has_next = _halo_flags(i, tiles_per_seq)

    def up(c):
        wa = wup_ref[:, c]
        wg = wup_ref[:, slice(dff + c.start, dff + c.stop)]
        return (jnp.dot(hn_ext, wa, preferred_element_type=F32), jnp.dot(hn, wg, preferred_element_type=F32))

    acts = []
    for c in [slice(c, min(c + FFN_CHUNK, dff)) for c in range(0, dff, FFN_CHUNK)]:
        a_ext, gate = up(c)
        ac = _conv3(a_ext, tm, cw_ref[:, c], cb_ref[:, c], has_prev, has_next)
        acts.append((_gelu_tanh(ac) * gate).astype(BF16))
    acc = h + jnp.dot(jnp.concatenate(acts, axis=1), wdn_ref[...], preferred_element_type=F32)
    o_ref[...] = _rms(acc, gf_ref[...]) if final_norm else acc


def _ffn(h, g, wup, cw, cb, wdn, gf, l, batch, final_norm):
    n, d = h.shape
    tm = ROW_TILE
    tiles_per_seq = (n // batch) // tm
    row = pl.BlockSpec((tm, d), lambda i: (i, 0))
    return pl.pallas_call(
        functools.partial(_ffn_kernel, tiles_per_seq=tiles_per_seq, final_norm=final_norm),
        out_shape=jax.ShapeDtypeStruct((n, d), F32),
        grid=(n // tm,),
        in_specs=[row] + _halo_specs(tm, d, n) + [
            _layer_spec(a, l) for a in (g, wup, cw, cb, wdn)] + [_const_spec((1, d))],
        out_specs=row,
        compiler_params=_cparams("parallel"),
        name="convffn",
    )(h, h, h, g, wup, cw, cb, wdn, gf)


def _pad_axis(a, axis, n, before=0):
    pads = [(0, 0)] * a.ndim
    pads[axis] = (before, n - before - a.shape[axis])
    return jnp.pad(a, pads)


def _prep_w_in(w):
    offs = np.concatenate([[0], np.cumsum(IN_SIZES)])
    seg = [w[..., offs[i]:offs[i + 1]] for i in range(len(IN_SIZES))]
    ml = seg[0:4]
    na = [_pad_axis(s, -1, NA_PAD) for s in seg[5:8]]
    gla = [_pad_axis(seg[8], -1, GLA_K_PAD), _pad_axis(seg[9], -1, GLA_K_PAD),
           _pad_axis(seg[10], -1, GLA_V_PAD), _pad_axis(seg[11], -1, GLA_V_PAD)]
    gates = [_pad_axis(seg[4][..., j * ML_HEADS:(j + 1) * ML_HEADS], -1, SUBLANES) for j in range(4)]
    gates = _pad_axis(jnp.concatenate(gates + [seg[12]], axis=-1), -1, GATE_COLS)
    return jnp.concatenate(ml + na + gla + [gates], axis=-1).astype(BF16)


def _prep_gate_bias(gb):
    groups = [_pad_axis(gb[:, j:j + 1, :], -1, SUBLANES) for j in range(4)]
    return _pad_axis(jnp.concatenate(groups, axis=-1), -1, GATE_COLS)


def _prep_gla_gate(w2, ab):
    width = 2 * GLA_K_PAD
    blocks = [_pad_axis(w2[:, z], -1, width, before=z * GLA_K_PAD) for z in range(2)]
    w = _pad_axis(jnp.concatenate(blocks, axis=1), 1, GATE_COLS, before=GLA_GATE_OFF)
    b = jnp.concatenate([_pad_axis(ab[:, z:z + 1], -1, GLA_K_PAD) for z in range(2)], axis=-1)
    return w.astype(BF16), b


def _prep_w_out(w):
    parts = [w[:, 0:ML_DIM], _pad_axis(w[:, ML_DIM:ML_DIM + NA_DIM], 1, NA_PAD),
             _pad_axis(w[:, ML_DIM + NA_DIM:], 1, GLA_V_PAD)]
    return jnp.concatenate(parts, axis=1).astype(BF16)


def kernel(x, mem, mix_norm_g, w_in, ml_conv_w, ml_conv_b, ml_gate_b, ml_norm_g, na_rpb,
           gla_a_w2, gla_a_b, gla_norm_g, w_out, xattn_norm_g, mem_norm_g,
           w_xq, w_xk, w_xv, w_xo, ffn_norm_g, w_up, ffn_conv_w, ffn_conv_b, w_down,
           final_norm_g):
    batch, seq, d = x.shape
    depth = w_in.shape[0]
    rows = seq // GRID_W
    assert seq % WIDE_ROW_TILE == 0 and rows >= NA_KH and rows % NA_UNROLL == 0
    assert seq % (GLA_CHUNK * GLA_STEPS) == 0
    h = x.reshape(batch * seq, d)
    mem2 = mem.reshape(batch * mem.shape[1], d)
    row3 = lambda v: v[:, None, :]
    w_in_p = _prep_w_in(w_in)
    gate_b_p = _prep_gate_bias(ml_gate_b)
    na_bias_p = _na_bias_table(na_rpb, rows)
    gla_w2_p, gla_ab_p = _prep_gla_gate(gla_a_w2, gla_a_b)
    gla_ng_p = row3(_pad_axis(gla_norm_g, -1, GLA_V_PAD))
    w_out_p = _prep_w_out(w_out)
    w_xq_b, w_xk_b, w_xv_b, w_xo_b, w_up_b, w_down_b = (
        w.astype(BF16) for w in (w_xq, w_xk, w_xv, w_xo, w_up, w_down))
    mix_g, ml_cb, ml_ng, x_g, mem_g, ffn_g, ffn_cb = (
        row3(v) for v in (mix_norm_g, ml_conv_b, ml_norm_g, xattn_norm_g, mem_norm_g, ffn_norm_g, ffn_conv_b))
    for l in range(depth):
        ktok, otok, q_t, v_t, gs_t, na, gla, gates = _inproj(
            h, mix_g, w_in_p, ml_conv_w, ml_cb, gate_b_p, l, batch)
        ml_out = _mlstm(ktok, otok, q_t, v_t, gs_t, ml_ng, l, batch)
        na0, na1 = _na(na, na_bias_p, l, batch)
        gla_out = _gla(gla, gates, gla_w2_p, gla_ab_p, gla_ng_p, l, batch)
        k, v = _memkv(mem2, mem_g, w_xk_b, w_xv_b, l)
        h = _xattn((ml_out, na0, na1, gla_out), w_out_p, h, x_g, w_xq_b, k, v, w_xo_b, l, batch)
        h = _ffn(h, ffn_g, w_up_b, ffn_conv_w, ffn_cb, w_down_b, final_norm_g.reshape(1, d), l, batch,
                 final_norm=(l == depth - 1))
    return h.reshape(batch, seq, d)
```

```python
import functools

import numpy as np
import jax
import jax.numpy as jnp
from jax import lax
from jax.experimental import pallas as pl
from jax.experimental.pallas import tpu as pltpu

F32 = jnp.float32
BF16 = jnp.bfloat16

HEAD_DIM = 64
ML_HEADS = 6
NA_HEADS = 5
GLA_HEADS = 5
ML_DIM = ML_HEADS * HEAD_DIM
NA_DIM = NA_HEADS * HEAD_DIM
GLA_DK = 32
GLA_KDIM = GLA_HEADS * GLA_DK
GLA_VDIM = GLA_HEADS * HEAD_DIM
GLA_RANK = 16
GLA_TAU = 16.0
GRID_W = 64
NA_KH = 8
NA_KW = 16
ML_CHUNK = 128
GLA_CHUNK = 64
X_HEADS = 4
EPS = 1e-6
IN_SIZES = (ML_DIM, ML_DIM, ML_DIM, ML_DIM, 4 * ML_HEADS,
            NA_DIM, NA_DIM, NA_DIM,
            GLA_KDIM, GLA_KDIM, GLA_VDIM, GLA_VDIM, 2 * GLA_RANK)

LANES = 128
SUBLANES = 8

PAIR_W = 2 * HEAD_DIM
ML_COLS = 4 * ML_DIM
NA_PAD = 3 * PAIR_W
NA_COLS = 3 * NA_PAD
GLA_K_PAD = 256
GLA_V_PAD = 3 * PAIR_W
GLA_COLS = 2 * GLA_K_PAD + 2 * GLA_V_PAD
GATE_COLS = LANES
GLA_GATE_OFF = 4 * SUBLANES
IN_COLS_PAD = ML_COLS + NA_COLS + GLA_COLS + GATE_COLS

ROW_TILE = 512
WIDE_ROW_TILE = 1024
NORM_BLOCK = 128
VMEM_LIMIT = 56 * 1024 * 1024

NT_DIMS = (((1,), (1,)), ((), ()))
TN_DIMS = (((0,), (0,)), ((), ()))


def _cparams(*sem):
    return pltpu.CompilerParams(dimension_semantics=sem, vmem_limit_bytes=VMEM_LIMIT)


def _const_spec(shape):
    nd = len(shape)
    return pl.BlockSpec(shape, lambda *_: (0,) * nd)


def _layer_spec(a, l):
    nd = a.ndim - 1
    return pl.BlockSpec((None,) + a.shape[1:], lambda *_, l=l, nd=nd: (l,) + (0,) * nd,
                        pipeline_mode=pl.Buffered(1))


def _rms(x, g):
    ms = jnp.mean(x * x, axis=-1, keepdims=True)
    return x * lax.rsqrt(ms + EPS) * g


def _rms_row_blocks(x_ref, g, tm):
    return [_rms(x_ref[r0:r0 + NORM_BLOCK, :], g).astype(BF16) for r0 in range(0, tm, NORM_BLOCK)]


def _dot_row_blocks(blocks, w):
    return jnp.concatenate([jnp.dot(b, w, preferred_element_type=F32) for b in blocks], axis=0)


def _log_sigmoid(x):
    return jnp.minimum(x, 0.0) - jnp.log(1.0 + jnp.exp(-jnp.abs(x)))


def _split_bf16(x):
    hi = x.astype(BF16)
    lo = (x - hi.astype(F32)).astype(BF16)
    return hi, lo


def _group_mean_sq(x, group):
    n = x.shape[-1]
    r = lax.broadcasted_iota(jnp.int32, (n, n), 0) // group
    c = lax.broadcasted_iota(jnp.int32, (n, n), 1) // group
    ones = jnp.where(r == c, 1.0, 0.0).astype(BF16)
    hi, lo = _split_bf16(x * x)
    s = jnp.dot(hi, ones, preferred_element_type=F32) + jnp.dot(lo, ones, preferred_element_type=F32)
    return s * (1.0 / group)


def _halo_flags(i, tiles_per_seq):
    has_prev = (i % tiles_per_seq != 0).astype(F32)
    has_next = (i % tiles_per_seq != tiles_per_seq - 1).astype(F32)
    return has_prev, has_next


def _conv3(a_ext, tm, w, b, has_prev, has_next):
    row = lax.broadcasted_iota(jnp.int32, (SUBLANES, 1), 0)
    a = a_ext[0:tm]
    down = pltpu.roll(a, 1, 0)
    up = pltpu.roll(a, tm - 1, 0)
    prev_row = a_ext[tm + SUBLANES - 1:tm + SUBLANES, :] * has_prev
    next_row = a_ext[tm + SUBLANES:tm + SUBLANES + 1, :] * has_next
    a_prev = jnp.concatenate([jnp.where(row == 0, prev_row, down[0:SUBLANES]), down[SUBLANES:]], axis=0)
    a_next = jnp.concatenate([up[:tm - SUBLANES],
                              jnp.where(row == SUBLANES - 1, next_row, up[tm - SUBLANES:])], axis=0)
    return a_prev * w[0:1, :] + a * w[1:2, :] + a_next * w[2:3, :] + b


def _inproj_kernel(h_ref, hp_ref, hx_ref, g_ref, w_ref, cw_ref, cb_ref, gb_ref,
                   ktok_ref, otok_ref, qT_ref, vT_ref, gsT_ref, na_ref, gla_ref, gate_ref, *, tiles_per_seq):
    tm = h_ref.shape[0]
    L = ML_CHUNK
    has_prev, has_next = _halo_flags(pl.program_id(0), tiles_per_seq)
    halo = jnp.concatenate([hp_ref[...], hx_ref[...]], axis=0)
    blocks = _rms_row_blocks(h_ref, g_ref[...], tm) + [_rms(halo, g_ref[...]).astype(BF16)]
    xn = jnp.concatenate(blocks[:-1], axis=0)
    xn_ext = jnp.concatenate(blocks, axis=0)

    def proj(lhs, c0, c1):
        return jnp.dot(lhs, w_ref[:, c0:c1], preferred_element_type=F32)

    for part in range(2):
        cols = slice(part * ML_DIM, (part + 1) * ML_DIM)
        a_ext = _dot_row_blocks(blocks, w_ref[:, cols]) if part == 0 else proj(xn_ext, cols.start, cols.stop)
        y = _conv3(a_ext, tm, cw_ref[:, cols], cb_ref[:, cols], has_prev, has_next)
        y = y * jax.nn.sigmoid(y)
        if part == 0:
            y_t = y.T
            for j in range(tm // L):
                qT_ref[j] = y_t[:, j * L:(j + 1) * L].astype(BF16)
        else:
            ktok_ref[...] = (y * HEAD_DIM ** -0.5).astype(BF16)
    v_t = proj(xn, 2 * ML_DIM, 3 * ML_DIM).T
    for j in range(tm // L):
        vT_ref[j] = v_t[:, j * L:(j + 1) * L].astype(BF16)
    otok_ref[...] = proj(xn, 3 * ML_DIM, 4 * ML_DIM).astype(BF16)
    off = ML_COLS
    for ref, chunk in ((na_ref, 384), (gla_ref, 256), (gate_ref, 128)):
        width = ref.shape[-1]
        for c in range(0, width, chunk):
            ref[:, c:c + chunk] = proj(xn, off + c, off + c + chunk).astype(ref.dtype)
        off += width
    g_t = (gate_ref[...] + gb_ref[...]).T
    for j in range(tm // L):
        gsT_ref[j] = g_t[0:4 * SUBLANES, j * L:(j + 1) * L]


def _halo_specs(tm, d, n):
    hb = tm // SUBLANES
    last = n // SUBLANES - 1
    return [pl.BlockSpec((SUBLANES, d), lambda i: (jnp.maximum(i * hb - 1, 0), 0)),
            pl.BlockSpec((SUBLANES, d), lambda i: (jnp.minimum((i + 1) * hb, last), 0))]


def _inproj(h, g, w, cw, cb, gb, l, batch):
    n, d = h.shape
    tm = WIDE_ROW_TILE
    nc = tm // ML_CHUNK
    row = lambda cols: pl.BlockSpec((tm, cols), lambda i: (i, 0))
    slab = lambda rows: pl.BlockSpec((nc, rows, ML_CHUNK), lambda i: (i, 0, 0))
    return pl.pallas_call(
        functools.partial(_inproj_kernel, tiles_per_seq=(n // batch) // tm),
        out_shape=(jax.ShapeDtypeStruct((n, ML_DIM), BF16),
                   jax.ShapeDtypeStruct((n, ML_DIM), BF16),
                   jax.ShapeDtypeStruct((n // ML_CHUNK, ML_DIM, ML_CHUNK), BF16),
                   jax.ShapeDtypeStruct((n // ML_CHUNK, ML_DIM, ML_CHUNK), BF16),
                   jax.ShapeDtypeStruct((n // ML_CHUNK, 4 * SUBLANES, ML_CHUNK), F32),
                   jax.ShapeDtypeStruct((n, NA_COLS), BF16),
                   jax.ShapeDtypeStruct((n, GLA_COLS), BF16),
                   jax.ShapeDtypeStruct((n, GATE_COLS), F32)),
        grid=(n // tm,),
        in_specs=[row(d)] + _halo_specs(tm, d, n) + [_layer_spec(a, l) for a in (g, w, cw, cb, gb)],
        out_specs=(row(ML_DIM), row(ML_DIM), slab(ML_DIM), slab(ML_DIM), slab(4 * SUBLANES),
                   row(NA_COLS), row(GLA_COLS), row(GATE_COLS)),
        compiler_params=_cparams("parallel"),
        name="inproj",
    )(h, h, h, g, w, cw, cb, gb)


def _lane_scan(x, op, fill, reverse):
    n = x.shape[-1]
    lane = lax.broadcasted_iota(jnp.int32, x.shape, 1)
    s = 1
    while s < n:
        if reverse:
            x = op(x, jnp.where(lane < n - s, pltpu.roll(x, n - s, 1), fill))
        else:
            x = op(x, jnp.where(lane >= s, pltpu.roll(x, s, 1), fill))
        s *= 2
    return x


def _mlstm_kernel(ktok_ref, otok_ref, qT_ref, vT_ref, gsT_ref, ng_ref, out_ref,
                  pre_ref, acol_ref, hsumT_ref, ct_ref):
    NC, _, L = qT_ref.shape

    def gate_group(grp):
        return gsT_ref[:, grp * SUBLANES:(grp + 1) * SUBLANES, :].reshape(NC * SUBLANES, L)

    a_all = []
    for d in range(2):
        logf = _log_sigmoid(gate_group(2 * d + 1))
        b = _lane_scan(logf, jnp.add, 0.0, reverse=(d == 1))
        a = gate_group(2 * d) - b
        pre_ref[d, 0] = b
        pre_ref[d, 1] = a
        pre_ref[d, 2] = _lane_scan(a, jnp.maximum, -jnp.inf, reverse=(d == 1))
        pre_ref[d, 3] = jnp.broadcast_to(jnp.sum(logf, axis=1, keepdims=True), a.shape)
        pre_ref[d, 4] = jnp.broadcast_to(jnp.max(a, axis=1, keepdims=True), a.shape)
        a_all.append(a)
    a_pad = jnp.zeros((L - 2 * SUBLANES, L), F32)
    for c in range(NC):
        rows = slice(c * SUBLANES, (c + 1) * SUBLANES)
        acol_ref[c] = jnp.concatenate([a_all[0][rows], a_all[1][rows], a_pad], axis=0).T

    hsumT_ref[...] = jnp.zeros(hsumT_ref.shape, F32)
    ct_ref[...] = jnp.zeros(ct_ref.shape, F32)

    sub = lax.broadcasted_iota(jnp.int32, (PAIR_W, L), 0)
    head_rows = (sub < HEAD_DIM, sub >= HEAD_DIM)
    ri = lax.broadcasted_iota(jnp.int32, (L, L), 0)
    ci = lax.broadcasted_iota(jnp.int32, (L, L), 1)
    tri = (ri <= ci, ri >= ci)

    def body(it, carry):
        m_reps = list(carry)
        chunk = (it, NC - 1 - it)
        gq = []
        for d in range(2):
            r8 = pl.ds(pl.multiple_of(chunk[d] * SUBLANES, SUBLANES), SUBLANES)
            b, a, amax_run, g, amax = (pre_ref[d, n, r8, :] for n in range(5))
            m_old = m_reps[d]
            big_m = jnp.maximum(m_old, amax_run)
            m_new = jnp.maximum(g + m_old, g + amax)
            m_reps[d] = m_new
            gq.append(dict(big_m=big_m, w_inter=jnp.exp(m_old - big_m), e=jnp.exp(-(b + big_m)),
                           wk=jnp.exp(g + a - m_new), decay=jnp.exp(g + m_old - m_new),
                           a_cols=acol_ref[chunk[d]]))

        items = []
        for d in range(2):
            c = chunk[d]
            r0 = pl.multiple_of(c * L, L)
            for p in range(ML_HEADS // 2):
                rows = slice(p * PAIR_W, (p + 1) * PAIR_W)
                k_p = ktok_ref[pl.ds(r0, L), rows]
                q_tp = qT_ref[c, rows, :]
                v_tp = vT_ref[c, rows, :]
                for half in range(2):
                    h = 2 * p + half
                    q_tm = jnp.where(head_rows[half], q_tp, jnp.zeros_like(q_tp))
                    v_ta = jnp.where(head_rows[half], v_tp, jnp.ones_like(v_tp))
                    ct_old = ct_ref[d * ML_HEADS + h]
                    s_t = jnp.dot(k_p, q_tm, preferred_element_type=F32)
                    inter_t = jnp.dot(ct_old.astype(BF16), q_tm, preferred_element_type=F32)
                    v_tw = (v_ta.astype(F32) * gq[d]["wk"][h:h + 1, :]).astype(BF16)
                    kv_t = jnp.dot(v_tw, k_p, preferred_element_type=F32)
                    items.append((d, c, p, half, h, v_ta, ct_old, s_t, inter_t, kv_t))
        s2 = []
        for d, c, p, half, h, v_ta, ct_old, s_t, inter_t, kv_t in items:
            dm = jnp.where(tri[d], gq[d]["a_cols"][:, SUBLANES * d + h:SUBLANES * d + h + 1]
                           - gq[d]["big_m"][h:h + 1, :], -jnp.inf)
            s2.append((s_t * jnp.exp(dm)).astype(BF16))
        num = [jnp.dot(it_[5], s, preferred_element_type=F32) for it_, s in zip(items, s2)]
        for idx in range(0, len(items), 2):
            outs = []
            for (d, c, p, half, h, v_ta, ct_old, s_t, inter_t, kv_t), n_t in zip(items[idx:idx + 2],
                                                                                 num[idx:idx + 2]):
                n_t = n_t + gq[d]["w_inter"][h:h + 1, :] * inter_t
                den = jnp.concatenate([n_t[HEAD_DIM:], n_t[:HEAD_DIM]], axis=0)
                outs.append(n_t / jnp.maximum(jnp.abs(den), gq[d]["e"][h:h + 1, :]))
                ct_ref[d * ML_HEADS + h] = gq[d]["decay"][h:h + 1, :] * ct_old + kv_t
            hsumT_ref[c, p * PAIR_W:(p + 1) * PAIR_W, :] += jnp.where(head_rows[0], outs[0], outs[1])
        return tuple(m_reps)

    m0 = jnp.zeros((SUBLANES, L), F32)
    lax.fori_loop(0, NC, body, (m0, m0))

    for c in range(NC):
        hs_t = hsumT_ref[c]
        parts = []
        for h in range(ML_HEADS):
            blk = hs_t[h * HEAD_DIM:(h + 1) * HEAD_DIM, :]
            ms = jnp.mean(blk * blk, axis=0, keepdims=True)
            parts.append(blk * lax.rsqrt(ms + EPS))
        y = jnp.concatenate(parts, axis=0).T * ng_ref[...]
        o = otok_ref[c * L:(c + 1) * L, :].astype(F32)
        out_ref[c * L:(c + 1) * L, :] = (y * jax.nn.sigmoid(o)).astype(BF16)


def _mlstm(ktok, otok, q_t, v_t, gs_t, ng, l, batch):
    n = ktok.shape[0]
    T = n // batch
    nc = T // ML_CHUNK
    tok = pl.BlockSpec((T, ML_DIM), lambda b: (b, 0))
    slab = lambda rows: pl.BlockSpec((nc, rows, ML_CHUNK), lambda b: (b, 0, 0))
    return pl.pallas_call(
        _mlstm_kernel,
        out_shape=jax.ShapeDtypeStruct((n, ML_DIM), BF16),
        grid=(batch,),
        in_specs=[tok, tok, slab(ML_DIM), slab(ML_DIM), slab(4 * SUBLANES), _layer_spec(ng, l)],
        out_specs=tok,
        scratch_shapes=[pltpu.VMEM((2, 5, nc * SUBLANES, ML_CHUNK), F32),
                        pltpu.VMEM((nc, ML_CHUNK, LANES), F32),
                        pltpu.VMEM((nc, ML_DIM, ML_CHUNK), F32),
                        pltpu.VMEM((2 * ML_HEADS, PAIR_W, PAIR_W), F32)],
        compiler_params=_cparams("parallel"),
        name="mlstm",
    )(ktok, otok, q_t, v_t, gs_t, ng)


def _gla_kernel(x_ref, gates_ref, w2_ref, ab_ref, ng_ref, out_ref, bc_ref, osum_ref, st_ref):
    T = x_ref.shape[0]
    L = GLA_CHUNK
    NC = T // L
    KP, VP = GLA_K_PAD, GLA_V_PAD
    RB = 256

    ri = lax.broadcasted_iota(jnp.int32, (RB, RB), 0)
    ci = lax.broadcasted_iota(jnp.int32, (RB, RB), 1)
    same = (ri // L) == (ci // L)
    tri_blk = (jnp.where(same & (ci <= ri), 1.0, 0.0).astype(BF16),
               jnp.where(same & (ci >= ri), 1.0, 0.0).astype(BF16))
    for r0 in range(0, T, RB):
        z = jnp.dot(gates_ref[r0:r0 + RB, :].astype(BF16), w2_ref[...],
                    preferred_element_type=F32) + ab_ref[...]
        hi, lo = _split_bf16(_log_sigmoid(z) * (1.0 / GLA_TAU))
        for d in range(2):
            sl = slice(d * KP, (d + 1) * KP)
            bc_ref[r0:r0 + RB, sl] = (jnp.dot(tri_blk[d], hi[:, sl], preferred_element_type=F32)
                                      + jnp.dot(tri_blk[d], lo[:, sl], preferred_element_type=F32))

    osum_ref[...] = jnp.zeros(osum_ref.shape, F32)
    st_ref[...] = jnp.zeros(st_ref.shape, F32)

    ri = lax.broadcasted_iota(jnp.int32, (L, L), 0)
    ci = lax.broadcasted_iota(jnp.int32, (L, L), 1)
    tri = (ci <= ri, ci >= ri)
    qhead = lax.broadcasted_iota(jnp.int32, (1, KP), 1) // GLA_DK
    pair_first = lax.broadcasted_iota(jnp.int32, (1, PAIR_W), 1) < HEAD_DIM
    st_mask = (lax.broadcasted_iota(jnp.int32, (VP, KP), 0) // HEAD_DIM
               == lax.broadcasted_iota(jnp.int32, (VP, KP), 1) // GLA_DK)

    def body(it, carry):
        w = []
        for d in range(2):
            for u in range(GLA_STEPS):
                c = it * GLA_STEPS + u
                r0 = pl.multiple_of((c if d == 0 else NC - 1 - c) * L, L)
                bc = bc_ref[pl.ds(r0, L), d * KP:(d + 1) * KP]
                btot = bc[L - 1:L, :] if d == 0 else bc[0:1, :]
                q = x_ref[pl.ds(r0, L), 0:KP].astype(F32) * (GLA_DK ** -0.5)
                k = x_ref[pl.ds(r0, L), KP:2 * KP].astype(F32)
                v = x_ref[pl.ds(r0, L), 2 * KP:2 * KP + VP]
                q_dec = q * jnp.exp(bc)
                k_dec = (k * jnp.exp(-bc)).astype(BF16)
                k_tail = (k * jnp.exp(btot - bc)).astype(BF16)
                a = [lax.dot_general(jnp.where(qhead == h, q_dec, 0.0).astype(BF16), k_dec, NT_DIMS,
                                     preferred_element_type=F32) for h in range(GLA_HEADS)]
                kv = lax.dot_general(v, k_tail, TN_DIMS, preferred_element_type=F32)
                w.append((d, r0, btot, v, q_dec.astype(BF16), a, kv))
        intra = []
        for d, r0, btot, v, q_dec, a, kv in w:
            ab = [jnp.where(tri[d], a_h, 0.0).astype(BF16) for a_h in a]
            oh = [jnp.dot(a_h, v[:, (h // 2) * PAIR_W:(h // 2 + 1) * PAIR_W], preferred_element_type=F32)
                  for h, a_h in enumerate(ab)]
            blocks = []
            for p in range(VP // PAIR_W):
                heads = [h for h in range(GLA_HEADS) if h // 2 == p]
                blocks.append(jnp.where(pair_first, oh[heads[0]], oh[heads[1]] if len(heads) > 1 else 0.0))
            intra.append(jnp.concatenate(blocks, axis=1))
        for d in range(2):
            st = st_ref[d]
            for (_, r0, btot, v, q_dec, a, kv), o_intra in zip(w[d * GLA_STEPS:(d + 1) * GLA_STEPS],
                                                               intra[d * GLA_STEPS:(d + 1) * GLA_STEPS]):
                o = lax.dot_general(q_dec, st.astype(BF16), NT_DIMS, preferred_element_type=F32)
                osum_ref[pl.ds(r0, L), :] += o + o_intra
                st = st * jnp.exp(btot) + jnp.where(st_mask, kv, 0.0)
            st_ref[d] = st
        return carry

    lax.fori_loop(0, NC // GLA_STEPS, body, 0)

    for r0 in range(0, T, RB):
        hs = osum_ref[r0:r0 + RB, :]
        y = hs * lax.rsqrt(_group_mean_sq(hs, HEAD_DIM) + EPS) * ng_ref[...]
        g = x_ref[r0:r0 + RB, 2 * KP + VP:2 * KP + 2 * VP].astype(F32)
        out_ref[r0:r0 + RB, :] = (y * (g * jax.nn.sigmoid(g))).astype(BF16)


def _gla(x, gates, w2, ab, ng, l, batch):
    n = x.shape[0]
    T = n // batch
    return pl.pallas_call(
        _gla_kernel,
        out_shape=jax.ShapeDtypeStruct((n, GLA_V_PAD), BF16),
        grid=(batch,),
        in_specs=[pl.BlockSpec((T, GLA_COLS), lambda b: (b, 0)),
                  pl.BlockSpec((T, GATE_COLS), lambda b: (b, 0)),
                  _layer_spec(w2, l), _layer_spec(ab, l), _layer_spec(ng, l)],
        out_specs=pl.BlockSpec((T, GLA_V_PAD), lambda b: (b, 0)),
        scratch_shapes=[pltpu.VMEM((T, 2 * GLA_K_PAD), F32),
                        pltpu.VMEM((T, GLA_V_PAD), F32),
                        pltpu.VMEM((2, GLA_V_PAD, GLA_K_PAD), F32)],
        compiler_params=_cparams("parallel"),
        name="gla",
    )(x, gates, w2, ab, ng)


NA_CLASSES = 8
GLA_STEPS = 2
NA_UNROLL = 8


def _na_kernel(q_ref, k_ref, v_ref, bias_ref, out_ref, *, n_heads_in_pair):
    T = q_ref.shape[0]
    W = GRID_W
    rows = T // W
    band = NA_KH * W
    lane = lax.broadcasted_iota(jnp.int32, (1, PAIR_W), 1)
    head_mask = (lane < HEAD_DIM, lane >= HEAD_DIM)
    scale = HEAD_DIM ** -0.5

    def group(gi, carry):
        items = []
        for u in range(NA_UNROLL):
            r = gi * NA_UNROLL + u
            rs = jnp.clip(r - NA_KH // 2, 0, rows - NA_KH)
            cls = jnp.where(r < NA_KH // 2, r, jnp.where(r > rows - NA_KH // 2, r - (rows - NA_KH), NA_KH // 2))
            q0 = pl.multiple_of(r * W, W)
            k0 = pl.multiple_of(rs * W, W)
            q = q_ref[pl.ds(q0, W), :]
            kb = k_ref[pl.ds(k0, band), :]
            for half in range(n_heads_in_pair):
                q_m = jnp.where(head_mask[half], q, jnp.zeros_like(q))
                s = lax.dot_general(q_m, kb, NT_DIMS, preferred_element_type=F32)
                items.append((u, half, q0, k0, cls, s))
        probs = []
        for u, half, q0, k0, cls, s in items:
            s = s * scale + bias_ref[half, cls]
            m = jnp.max(s, axis=1, keepdims=True)
            p = jnp.exp(s - m)
            l = jnp.sum(p, axis=1, keepdims=True)
            probs.append((u, half, q0, k0, p.astype(BF16), l))
        outs = {}
        for u, half, q0, k0, p, l in probs:
            vb = v_ref[pl.ds(k0, band), :]
            outs[(u, half)] = (q0, jnp.dot(p, vb, preferred_element_type=F32) / l)
        for u in range(NA_UNROLL):
            q0, o0 = outs[(u, 0)]
            second = outs[(u, 1)][1] if n_heads_in_pair == 2 else 0.0
            out_ref[pl.ds(q0, W), :] = jnp.where(head_mask[0], o0, second).astype(BF16)
        return carry

    lax.fori_loop(0, rows // NA_UNROLL, group, 0)


def _na(na, bias, l, batch):
    n = na.shape[0]
    T = n // batch
    npairs = NA_PAD // PAIR_W
    outs = []
    for p0, p1, nh in ((0, NA_HEADS // 2, 2), (NA_HEADS // 2, npairs, 1)):
        npp = p1 - p0
        outs.append(pl.pallas_call(
            functools.partial(_na_kernel, n_heads_in_pair=nh),
            out_shape=jax.ShapeDtypeStruct((n, npp * PAIR_W), BF16),
            grid=(npp, batch),
            in_specs=[pl.BlockSpec((T, PAIR_W), lambda p, b, o=p0: (b, o + p)),
                      pl.BlockSpec((T, PAIR_W), lambda p, b, o=npairs + p0: (b, o + p)),
                      pl.BlockSpec((T, PAIR_W), lambda p, b, o=2 * npairs + p0: (b, o + p)),
                      pl.BlockSpec((None, 2, NA_CLASSES, GRID_W, NA_KH * GRID_W),
                                   lambda p, b, o=p0: (l, o + p, 0, 0, 0))],
            out_specs=pl.BlockSpec((T, PAIR_W), lambda p, b: (b, p)),
            compiler_params=_cparams("parallel", "parallel"),
            name=f"natten{nh}",
        )(na, na, na, bias))
    return outs


def _na_bias_table(rpb, rows):
    kh = NA_KH
    rep_rows = np.array(list(range(kh // 2)) + [kh // 2] + list(range(rows - kh // 2 + 1, rows)))
    row_start = np.clip(rep_rows - kh // 2, 0, rows - kh)
    dr = row_start[:, None] + np.arange(kh)[None, :] - rep_rows[:, None] + (kh - 1)
    c = np.arange(GRID_W)
    col_start = np.clip(c - NA_KW // 2, 0, GRID_W - NA_KW)
    in_win = (c[None, :] >= col_start[:, None]) & (c[None, :] < col_start[:, None] + NA_KW)
    dc = np.clip(c[None, :] - c[:, None], -(NA_KW - 1), NA_KW - 1) + (NA_KW - 1)
    sel_r = jnp.asarray(np.eye(2 * kh - 1, dtype=np.float32)[dr])
    sel_c = jnp.asarray(np.eye(2 * NA_KW - 1, dtype=np.float32)[dc])
    depth, heads = rpb.shape[:2]
    rp = rpb.astype(F32).reshape(depth * heads, 2 * kh - 1, 2 * NA_KW - 1)
    tab = jnp.einsum('skr,hrc->hskc', sel_r, rp, precision=lax.Precision.HIGHEST)
    tab = jnp.einsum('hskc,qwc->hsqkw', tab, sel_c, precision=lax.Precision.HIGHEST)
    tab = jnp.where(in_win[None, None, :, None, :], tab, -jnp.inf)
    tab = tab.reshape(depth, heads, len(rep_rows), GRID_W, kh * GRID_W)
    return jnp.pad(tab, ((0, 0), (0, 2 * (NA_PAD // PAIR_W) - heads), (0, 0), (0, 0), (0, 0)))


def _memkv_kernel(m_ref, g_ref, wk_ref, wv_ref, k_ref, v_ref):
    mn = _rms(m_ref[...], g_ref[...]).astype(BF16)
    k_ref[...] = jnp.dot(mn, wk_ref[...], preferred_element_type=F32).astype(BF16)
    v_ref[...] = jnp.dot(mn, wv_ref[...], preferred_element_type=F32).astype(BF16)


def _memkv(mem, g, wk, wv, l):
    n, d = mem.shape
    tm = min(ROW_TILE, n)
    row = pl.BlockSpec((tm, d), lambda i: (i, 0))
    return pl.pallas_call(
        _memkv_kernel,
        out_shape=(jax.ShapeDtypeStruct((n, d), BF16), jax.ShapeDtypeStruct((n, d), BF16)),
        grid=(n // tm,),
        in_specs=[row, _layer_spec(g, l), _layer_spec(wk, l), _layer_spec(wv, l)],
        out_specs=(row, row),
        compiler_params=_cparams("parallel"),
        name="memkv",
    )(mem, g, wk, wv)


def _xattn_kernel(ml_ref, na0_ref, na1_ref, gla_ref, wout_ref, h_ref, g_ref, wq_ref, k_ref, v_ref, wo_ref,
                  o_ref):
    mixed = jnp.concatenate([ml_ref[...], na0_ref[...], na1_ref[...], gla_ref[...]], axis=1)
    o_ref[...] = h_ref[...] + jnp.dot(mixed, wout_ref[...], preferred_element_type=F32)
    h = o_ref[...]
    tm, d = h.shape
    hd = d // X_HEADS
    heads = [slice(i * hd, (i + 1) * hd) for i in range(X_HEADS)]
    blocks = _rms_row_blocks(o_ref, g_ref[...], tm)
    hn = jnp.concatenate(blocks, axis=0)
    q = [(_dot_row_blocks(blocks, wq_ref[:, sl]) if n == 0 else
          jnp.dot(hn, wq_ref[:, sl], preferred_element_type=F32)).astype(BF16) for n, sl in enumerate(heads)]
    scores = [lax.dot_general(q_h, k_ref[:, sl], NT_DIMS, preferred_element_type=F32)
              for q_h, sl in zip(q, heads)]
    probs = []
    for s in scores:
        s = s * (hd ** -0.5)
        p = jnp.exp(s - jnp.max(s, axis=1, keepdims=True))
        probs.append((p.astype(BF16), jnp.sum(p, axis=1, keepdims=True)))
    ctx = [jnp.dot(p, v_ref[:, sl], preferred_element_type=F32) for (p, _), sl in zip(probs, heads)]
    o = jnp.concatenate([(c / l).astype(BF16) for c, (_, l) in zip(ctx, probs)], axis=1)
    o_ref[...] = h + jnp.dot(o, wo_ref[...], preferred_element_type=F32)


def _xattn(mixed, wout, h, g, wq, k, v, wo, l, batch):
    n, d = h.shape
    tm = WIDE_ROW_TILE
    tiles_per_seq = (n // batch) // tm
    n_mem = k.shape[0] // batch
    row = lambda cols: pl.BlockSpec((tm, cols), lambda i: (i, 0))
    kv = pl.BlockSpec((n_mem, d), lambda i: (i // tiles_per_seq, 0))
    return pl.pallas_call(
        _xattn_kernel,
        out_shape=jax.ShapeDtypeStruct((n, d), F32),
        grid=(n // tm,),
        in_specs=[row(m.shape[1]) for m in mixed] + [
            _layer_spec(wout, l), row(d), _layer_spec(g, l), _layer_spec(wq, l), kv, kv, _layer_spec(wo, l)],
        out_specs=row(d),
        compiler_params=_cparams("parallel"),
        name="xattn",
    )(*mixed, wout, h, g, wq, k, v, wo)


FFN_CHUNK = 256


def _gelu_tanh(x):
    c = float(np.sqrt(2.0 / np.pi))
    half = 0.5 * x
    return half + half * jnp.tanh(x * (c + (c * 0.044715) * (x * x)))


def _ffn_kernel(h_ref, hp_ref, hx_ref, g_ref, wup_ref, cw_ref, cb_ref, wdn_ref, gf_ref, o_ref, *,
                tiles_per_seq, final_norm):
    i = pl.program_id(0)
    h = h_ref[...]
    tm = h.shape[0]
    dff = wdn_ref.shape[0]
    halo = jnp.concatenate([hp_ref[...], hx_ref[...]], axis=0)
    blocks = _rms_row_blocks(h_ref, g_ref[...], tm) + [_rms(halo, g_ref[...]).astype(BF16)]
    hn = jnp.concatenate(blocks[:-1], axis=0)
    hn_ext = jnp.concatenate(blocks, axis=0)
    has_prev, has_next = _halo_flags(i, tiles_per_seq)

    def up(c):
        wa = wup_ref[:, c]
        wg = wup_ref[:, slice(dff + c.start, dff + c.stop)]
        return (jnp.dot(hn_ext, wa, preferred_element_type=F32), jnp.dot(hn, wg, preferred_element_type=F32))

    acts = []
    for c in [slice(c, min(c + FFN_CHUNK, dff)) for c in range(0, dff, FFN_CHUNK)]:
        a_ext, gate = up(c)
        ac = _conv3(a_ext, tm, cw_ref[:, c], cb_ref[:, c], has_prev, has_next)
        acts.append((_gelu_tanh(ac) * gate).astype(BF16))
    acc = h + jnp.dot(jnp.concatenate(acts, axis=1), wdn_ref[...], preferred_element_type=F32)
    o_ref[...] = _rms(acc, gf_ref[...]) if final_norm else acc


def _ffn(h, g, wup, cw, cb, wdn, gf, l, batch, final_norm):
    n, d = h.shape
    tm = WIDE_ROW_TILE
    tiles_per_seq = (n // batch) // tm
    row = pl.BlockSpec((tm, d), lambda i: (i, 0))
    return pl.pallas_call(
        functools.partial(_ffn_kernel, tiles_per_seq=tiles_per_seq, final_norm=final_norm),
        out_shape=jax.ShapeDtypeStruct((n, d), F32),
        grid=(n // tm,),
        in_specs=[row] + _halo_specs(tm, d, n) + [
            _layer_spec(a, l) for a in (g, wup, cw, cb, wdn)] + [_const_spec((1, d))],
        out_specs=row,
        compiler_params=_cparams("parallel"),
        name="convffn",
    )(h, h, h, g, wup, cw, cb, wdn, gf)


def _pad_axis(a, axis, n, before=0):
    pads = [(0, 0)] * a.ndim
    pads[axis] = (before, n - before - a.shape[axis])
    return jnp.pad(a, pads)


def _prep_w_in(w):
    offs = np.concatenate([[0], np.cumsum(IN_SIZES)])
    seg = [w[..., offs[i]:offs[i + 1]] for i in range(len(IN_SIZES))]
    ml = seg[0:4]
    na = [_pad_axis(s, -1, NA_PAD) for s in seg[5:8]]
    gla = [_pad_axis(seg[8], -1, GLA_K_PAD), _pad_axis(seg[9], -1, GLA_K_PAD),
           _pad_axis(seg[10], -1, GLA_V_PAD), _pad_axis(seg[11], -1, GLA_V_PAD)]
    gates = [_pad_axis(seg[4][..., j * ML_HEADS:(j + 1) * ML_HEADS], -1, SUBLANES) for j in range(4)]
    gates = _pad_axis(jnp.concatenate(gates + [seg[12]], axis=-1), -1, GATE_COLS)
    return jnp.concatenate(ml + na + gla + [gates], axis=-1).astype(BF16)


def _prep_gate_bias(gb):
    groups = [_pad_axis(gb[:, j:j + 1, :], -1, SUBLANES) for j in range(4)]
    return _pad_axis(jnp.concatenate(groups, axis=-1), -1, GATE_COLS)


def _prep_gla_gate(w2, ab):
    width = 2 * GLA_K_PAD
    blocks = [_pad_axis(w2[:, z], -1, width, before=z * GLA_K_PAD) for z in range(2)]
    w = _pad_axis(jnp.concatenate(blocks, axis=1), 1, GATE_COLS, before=GLA_GATE_OFF)
    b = jnp.concatenate([_pad_axis(ab[:, z:z + 1], -1, GLA_K_PAD) for z in range(2)], axis=-1)
    return w.astype(BF16), b


def _prep_w_out(w):
    parts = [w[:, 0:ML_DIM], _pad_axis(w[:, ML_DIM:ML_DIM + NA_DIM], 1, NA_PAD),
             _pad_axis(w[:, ML_DIM + NA_DIM:], 1, GLA_V_PAD)]
    return jnp.concatenate(parts, axis=1).astype(BF16)


def kernel(x, mem, mix_norm_g, w_in, ml_conv_w, ml_conv_b, ml_gate_b, ml_norm_g, na_rpb,
           gla_a_w2, gla_a_b, gla_norm_g, w_out, xattn_norm_g, mem_norm_g,
           w_xq, w_xk, w_xv, w_xo, ffn_norm_g, w_up, ffn_conv_w, ffn_conv_b, w_down,
           final_norm_g):
    batch, seq, d = x.shape
    depth = w_in.shape[0]
    rows = seq // GRID_W
    assert seq % WIDE_ROW_TILE == 0 and rows >= NA_KH and rows % NA_UNROLL == 0
    assert seq % (GLA_CHUNK * GLA_STEPS) == 0
    h = x.reshape(batch * seq, d)
    mem2 = mem.reshape(batch * mem.shape[1], d)
    row3 = lambda v: v[:, None, :]
    w_in_p = _prep_w_in(w_in)
    gate_b_p = _prep_gate_bias(ml_gate_b)
    na_bias_p = _na_bias_table(na_rpb, rows)
    gla_w2_p, gla_ab_p = _prep_gla_gate(gla_a_w2, gla_a_b)
    gla_ng_p = row3(_pad_axis(gla_norm_g, -1, GLA_V_PAD))
    w_out_p = _prep_w_out(w_out)
    w_xq_b, w_xk_b, w_xv_b, w_xo_b, w_up_b, w_down_b = (
        w.astype(BF16) for w in (w_xq, w_xk, w_xv, w_xo, w_up, w_down))
    mix_g, ml_cb, ml_ng, x_g, mem_g, ffn_g, ffn_cb = (
        row3(v) for v in (mix_norm_g, ml_conv_b, ml_norm_g, xattn_norm_g, mem_norm_g, ffn_norm_g, ffn_conv_b))
    for l in range(depth):
        ktok, otok, q_t, v_t, gs_t, na, gla, gates = _inproj(
            h, mix_g, w_in_p, ml_conv_w, ml_cb, gate_b_p, l, batch)
        ml_out = _mlstm(ktok, otok, q_t, v_t, gs_t, ml_ng, l, batch)
        na0, na1 = _na(na, na_bias_p, l, batch)
        gla_out = _gla(gla, gates, gla_w2_p, gla_ab_p, gla_ng_p, l, batch)
        k, v = _memkv(mem2, mem_g, w_xk_b, w_xv_b, l)
        h = _xattn((ml_out, na0, na1, gla_out), w_out_p, h, x_g, w_xq_b, k, v, w_xo_b, l, batch)
        h = _ffn(h, ffn_g, w_up_b, ffn_conv_w, ffn_cb, w_down_b, final_norm_g.reshape(1, d), l, batch,
                 final_norm=(l == depth - 1))
    return h.reshape(batch, seq, d)
```

```python
import functools

import numpy as np
import jax
import jax.numpy as jnp
from jax import lax
from jax.experimental import pallas as pl
from jax.experimental.pallas import tpu as pltpu

F32 = jnp.float32
BF16 = jnp.bfloat16

HEAD_DIM = 64
ML_HEADS = 6
NA_HEADS = 5
GLA_HEADS = 5
ML_DIM = ML_HEADS * HEAD_DIM
NA_DIM = NA_HEADS * HEAD_DIM
GLA_DK = 32
GLA_KDIM = GLA_HEADS * GLA_DK
GLA_VDIM = GLA_HEADS * HEAD_DIM
GLA_RANK = 16
GLA_TAU = 16.0
GRID_W = 64
NA_KH = 8
NA_KW = 16
ML_CHUNK = 128
GLA_CHUNK = 64
X_HEADS = 4
EPS = 1e-6
IN_SIZES = (ML_DIM, ML_DIM, ML_DIM, ML_DIM, 4 * ML_HEADS,
            NA_DIM, NA_DIM, NA_DIM,
            GLA_KDIM, GLA_KDIM, GLA_VDIM, GLA_VDIM, 2 * GLA_RANK)

LANES = 128
SUBLANES = 8

PAIR_W = 2 * HEAD_DIM
ML_COLS = 4 * ML_DIM
NA_PAD = 3 * PAIR_W
NA_COLS = 3 * NA_PAD
GLA_K_PAD = 256
GLA_V_PAD = 3 * PAIR_W
GLA_COLS = 2 * GLA_K_PAD + 2 * GLA_V_PAD
GATE_COLS = LANES
GLA_GATE_OFF = 4 * SUBLANES
IN_COLS_PAD = ML_COLS + NA_COLS + GLA_COLS + GATE_COLS

ROW_TILE = 512
WIDE_ROW_TILE = 1024
NORM_BLOCK = 128
VMEM_LIMIT = 56 * 1024 * 1024

NT_DIMS = (((1,), (1,)), ((), ()))
TN_DIMS = (((0,), (0,)), ((), ()))


def _cparams(*sem):
    return pltpu.CompilerParams(dimension_semantics=sem, vmem_limit_bytes=VMEM_LIMIT)


def _const_spec(shape):
    nd = len(shape)
    return pl.BlockSpec(shape, lambda *_: (0,) * nd)


def _layer_spec(a, l):
    nd = a.ndim - 1
    return pl.BlockSpec((None,) + a.shape[1:], lambda *_, l=l, nd=nd: (l,) + (0,) * nd,
                        pipeline_mode=pl.Buffered(1))


def _rms(x, g):
    ms = jnp.mean(x * x, axis=-1, keepdims=True)
    return x * lax.rsqrt(ms + EPS) * g


def _rms_row_blocks(x_ref, g, tm):
    return [_rms(x_ref[r0:r0 + NORM_BLOCK, :], g).astype(BF16) for r0 in range(0, tm, NORM_BLOCK)]


def _dot_row_blocks(blocks, w):
    return jnp.concatenate([jnp.dot(b, w, preferred_element_type=F32) for b in blocks], axis=0)


def _log_sigmoid(x):
    return jnp.minimum(x, 0.0) - jnp.log(1.0 + jnp.exp(-jnp.abs(x)))


def _split_bf16(x):
    hi = x.astype(BF16)
    lo = (x - hi.astype(F32)).astype(BF16)
    return hi, lo


def _group_mean_sq(x, group):
    n = x.shape[-1]
    r = lax.broadcasted_iota(jnp.int32, (n, n), 0) // group
    c = lax.broadcasted_iota(jnp.int32, (n, n), 1) // group
    ones = jnp.where(r == c, 1.0, 0.0).astype(BF16)
    hi, lo = _split_bf16(x * x)
    s = jnp.dot(hi, ones, preferred_element_type=F32) + jnp.dot(lo, ones, preferred_element_type=F32)
    return s * (1.0 / group)


def _halo_flags(i, tiles_per_seq):
    has_prev = (i % tiles_per_seq != 0).astype(F32)
    has_next = (i % tiles_per_seq != tiles_per_seq - 1).astype(F32)
    return has_prev, has_next


def _conv3(a_ext, tm, w, b, has_prev, has_next):
    row = lax.broadcasted_iota(jnp.int32, (SUBLANES, 1), 0)
    a = a_ext[0:tm]
    down = pltpu.roll(a, 1, 0)
    up = pltpu.roll(a, tm - 1, 0)
    prev_row = a_ext[tm + SUBLANES - 1:tm + SUBLANES, :] * has_prev
    next_row = a_ext[tm + SUBLANES:tm + SUBLANES + 1, :] * has_next
    a_prev = jnp.concatenate([jnp.where(row == 0, prev_row, down[0:SUBLANES]), down[SUBLANES:]], axis=0)
    a_next = jnp.concatenate([up[:tm - SUBLANES],
                              jnp.where(row == SUBLANES - 1, next_row, up[tm - SUBLANES:])], axis=0)
    return a_prev * w[0:1, :] + a * w[1:2, :] + a_next * w[2:3, :] + b


def _inproj_kernel(h_ref, hp_ref, hx_ref, g_ref, w_ref, cw_ref, cb_ref, gb_ref,
                   ktok_ref, otok_ref, qT_ref, vT_ref, gsT_ref, na_ref, gla_ref, gate_ref, *, tiles_per_seq):
    tm = h_ref.shape[0]
    L = ML_CHUNK
    has_prev, has_next = _halo_flags(pl.program_id(0), tiles_per_seq)
    halo = jnp.concatenate([hp_ref[...], hx_ref[...]], axis=0)
    blocks = _rms_row_blocks(h_ref, g_ref[...], tm) + [_rms(halo, g_ref[...]).astype(BF16)]
    xn = jnp.concatenate(blocks[:-1], axis=0)
    xn_ext = jnp.concatenate(blocks, axis=0)

    def proj(lhs, c0, c1):
        return jnp.dot(lhs, w_ref[:, c0:c1], preferred_element_type=F32)

    for part in range(2):
        cols = slice(part * ML_DIM, (part + 1) * ML_DIM)
        a_ext = _dot_row_blocks(blocks, w_ref[:, cols]) if part == 0 else proj(xn_ext, cols.start, cols.stop)
        y = _conv3(a_ext, tm, cw_ref[:, cols], cb_ref[:, cols], has_prev, has_next)
        y = y * jax.nn.sigmoid(y)
        if part == 0:
            y_t = y.T
            for j in range(tm // L):
                qT_ref[j] = y_t[:, j * L:(j + 1) * L].astype(BF16)
        else:
            ktok_ref[...] = (y * HEAD_DIM ** -0.5).astype(BF16)
    v_t = proj(xn, 2 * ML_DIM, 3 * ML_DIM).T
    for j in range(tm // L):
        vT_ref[j] = v_t[:, j * L:(j + 1) * L].astype(BF16)
    otok_ref[...] = proj(xn, 3 * ML_DIM, 4 * ML_DIM).astype(BF16)
    off = ML_COLS
    for ref, chunk in ((na_ref, 384), (gla_ref, 256), (gate_ref, 128)):
        width = ref.shape[-1]
        for c in range(0, width, chunk):
            ref[:, c:c + chunk] = proj(xn, off + c, off + c + chunk).astype(ref.dtype)
        off += width
    g_t = (gate_ref[...] + gb_ref[...]).T
    for j in range(tm // L):
        gsT_ref[j] = g_t[0:4 * SUBLANES, j * L:(j + 1) * L]


def _halo_specs(tm, d, n):
    hb = tm // SUBLANES
    last = n // SUBLANES - 1
    return [pl.BlockSpec((SUBLANES, d), lambda i: (jnp.maximum(i * hb - 1, 0), 0)),
            pl.BlockSpec((SUBLANES, d), lambda i: (jnp.minimum((i + 1) * hb, last), 0))]


def _inproj(h, g, w, cw, cb, gb, l, batch):
    n, d = h.shape
    tm = WIDE_ROW_TILE
    nc = tm // ML_CHUNK
    row = lambda cols: pl.BlockSpec((tm, cols), lambda i: (i, 0))
    slab = lambda rows: pl.BlockSpec((nc, rows, ML_CHUNK), lambda i: (i, 0, 0))
    return pl.pallas_call(
        functools.partial(_inproj_kernel, tiles_per_seq=(n // batch) // tm),
        out_shape=(jax.ShapeDtypeStruct((n, ML_DIM), BF16),
                   jax.ShapeDtypeStruct((n, ML_DIM), BF16),
                   jax.ShapeDtypeStruct((n // ML_CHUNK, ML_DIM, ML_CHUNK), BF16),
                   jax.ShapeDtypeStruct((n // ML_CHUNK, ML_DIM, ML_CHUNK), BF16),
                   jax.ShapeDtypeStruct((n // ML_CHUNK, 4 * SUBLANES, ML_CHUNK), F32),
                   jax.ShapeDtypeStruct((n, NA_COLS), BF16),
                   jax.ShapeDtypeStruct((n, GLA_COLS), BF16),
                   jax.ShapeDtypeStruct((n, GATE_COLS), F32)),
        grid=(n // tm,),
        in_specs=[row(d)] + _halo_specs(tm, d, n) + [_layer_spec(a, l) for a in (g, w, cw, cb, gb)],
        out_specs=(row(ML_DIM), row(ML_DIM), slab(ML_DIM), slab(ML_DIM), slab(4 * SUBLANES),
                   row(NA_COLS), row(GLA_COLS), row(GATE_COLS)),
        compiler_params=_cparams("parallel"),
        name="inproj",
    )(h, h, h, g, w, cw, cb, gb)


def _lane_scan(x, op, fill, reverse):
    n = x.shape[-1]
    lane = lax.broadcasted_iota(jnp.int32, x.shape, 1)
    s = 1
    while s < n:
        if reverse:
            x = op(x, jnp.where(lane < n - s, pltpu.roll(x, n - s, 1), fill))
        else:
            x = op(x, jnp.where(lane >= s, pltpu.roll(x, s, 1), fill))
        s *= 2
    return x


def _mlstm_kernel(ktok_ref, otok_ref, qT_ref, vT_ref, gsT_ref, ng_ref, out_ref,
                  pre_ref, acol_ref, hsumT_ref, ct_ref):
    NC, _, L = qT_ref.shape

    def gate_group(grp):
        return gsT_ref[:, grp * SUBLANES:(grp + 1) * SUBLANES, :].reshape(NC * SUBLANES, L)

    a_all = []
    for d in range(2):
        logf = _log_sigmoid(gate_group(2 * d + 1))
        b = _lane_scan(logf, jnp.add, 0.0, reverse=(d == 1))
        a = gate_group(2 * d) - b
        pre_ref[d, 0] = b
        pre_ref[d, 1] = a
        pre_ref[d, 2] = _lane_scan(a, jnp.maximum, -jnp.inf, reverse=(d == 1))
        pre_ref[d, 3] = jnp.broadcast_to(jnp.sum(logf, axis=1, keepdims=True), a.shape)
        pre_ref[d, 4] = jnp.broadcast_to(jnp.max(a, axis=1, keepdims=True), a.shape)
        a_all.append(a)
    a_pad = jnp.zeros((L - 2 * SUBLANES, L), F32)
    for c in range(NC):
        rows = slice(c * SUBLANES, (c + 1) * SUBLANES)
        acol_ref[c] = jnp.concatenate([a_all[0][rows], a_all[1][rows], a_pad], axis=0).T

    hsumT_ref[...] = jnp.zeros(hsumT_ref.shape, F32)
    ct_ref[...] = jnp.zeros(ct_ref.shape, F32)

    sub = lax.broadcasted_iota(jnp.int32, (PAIR_W, L), 0)
    head_rows = (sub < HEAD_DIM, sub >= HEAD_DIM)
    ri = lax.broadcasted_iota(jnp.int32, (L, L), 0)
    ci = lax.broadcasted_iota(jnp.int32, (L, L), 1)
    tri = (ri <= ci, ri >= ci)

    def body(it, carry):
        m_reps = list(carry)
        chunk = (it, NC - 1 - it)
        gq = []
        for d in range(2):
            r8 = pl.ds(pl.multiple_of(chunk[d] * SUBLANES, SUBLANES), SUBLANES)
            b, a, amax_run, g, amax = (pre_ref[d, n, r8, :] for n in range(5))
            m_old = m_reps[d]
            big_m = jnp.maximum(m_old, amax_run)
            m_new = jnp.maximum(g + m_old, g + amax)
            m_reps[d] = m_new
            gq.append(dict(big_m=big_m, w_inter=jnp.exp(m_old - big_m), e=jnp.exp(-(b + big_m)),
                           wk=jnp.exp(g + a - m_new), decay=jnp.exp(g + m_old - m_new),
                           a_cols=acol_ref[chunk[d]]))

        items = []
        for d in range(2):
            c = chunk[d]
            r0 = pl.multiple_of(c * L, L)
            for p in range(ML_HEADS // 2):
                rows = slice(p * PAIR_W, (p + 1) * PAIR_W)
                k_p = ktok_ref[pl.ds(r0, L), rows]
                q_tp = qT_ref[c, rows, :]
                v_tp = vT_ref[c, rows, :]
                for half in range(2):
                    h = 2 * p + half
                    q_tm = jnp.where(head_rows[half], q_tp, jnp.zeros_like(q_tp))
                    v_ta = jnp.where(head_rows[half], v_tp, jnp.ones_like(v_tp))
                    ct_old = ct_ref[d * ML_HEADS + h]
                    s_t = jnp.dot(k_p, q_tm, preferred_element_type=F32)
                    inter_t = jnp.dot(ct_old.astype(BF16), q_tm, preferred_element_type=F32)
                    v_tw = (v_ta.astype(F32) * gq[d]["wk"][h:h + 1, :]).astype(BF16)
                    kv_t = jnp.dot(v_tw, k_p, preferred_element_type=F32)
                    items.append((d, c, p, half, h, v_ta, ct_old, s_t, inter_t, kv_t))
        s2 = []
        for d, c, p, half, h, v_ta, ct_old, s_t, inter_t, kv_t in items:
            dm = jnp.where(tri[d], gq[d]["a_cols"][:, SUBLANES * d + h:SUBLANES * d + h + 1]
                           - gq[d]["big_m"][h:h + 1, :], -jnp.inf)
            s2.append((s_t * jnp.exp(dm)).astype(BF16))
        num = [jnp.dot(it_[5], s, preferred_element_type=F32) for it_, s in zip(items, s2)]
        for idx in range(0, len(items), 2):
            outs = []
            for (d, c, p, half, h, v_ta, ct_old, s_t, inter_t, kv_t), n_t in zip(items[idx:idx + 2],
                                                                                 num[idx:idx + 2]):
                n_t = n_t + gq[d]["w_inter"][h:h + 1, :] * inter_t
                den = jnp.concatenate([n_t[HEAD_DIM:], n_t[:HEAD_DIM]], axis=0)
                outs.append(n_t / jnp.maximum(jnp.abs(den), gq[d]["e"][h:h + 1, :]))
                ct_ref[d * ML_HEADS + h] = gq[d]["decay"][h:h + 1, :] * ct_old + kv_t
            hsumT_ref[c, p * PAIR_W:(p + 1) * PAIR_W, :] += jnp.where(head_rows[0], outs[0], outs[1])
        return tuple(m_reps)

    m0 = jnp.zeros((SUBLANES, L), F32)
    lax.fori_loop(0, NC, body, (m0, m0))

    for c in range(NC):
        hs_t = hsumT_ref[c]
        parts = []
        for h in range(ML_HEADS):
            blk = hs_t[h * HEAD_DIM:(h + 1) * HEAD_DIM, :]
            ms = jnp.mean(blk * blk, axis=0, keepdims=True)
            parts.append(blk * lax.rsqrt(ms + EPS))
        y = jnp.concatenate(parts, axis=0).T * ng_ref[...]
        o = otok_ref[c * L:(c + 1) * L, :].astype(F32)
        out_ref[c * L:(c + 1) * L, :] = (y * jax.nn.sigmoid(o)).astype(BF16)


def _mlstm(ktok, otok, q_t, v_t, gs_t, ng, l, batch):
    n = ktok.shape[0]
    T = n // batch
    nc = T // ML_CHUNK
    tok = pl.BlockSpec((T, ML_DIM), lambda b: (b, 0))
    slab = lambda rows: pl.BlockSpec((nc, rows, ML_CHUNK), lambda b: (b, 0, 0))
    return pl.pallas_call(
        _mlstm_kernel,
        out_shape=jax.ShapeDtypeStruct((n, ML_DIM), BF16),
        grid=(batch,),
        in_specs=[tok, tok, slab(ML_DIM), slab(ML_DIM), slab(4 * SUBLANES), _layer_spec(ng, l)],
        out_specs=tok,
        scratch_shapes=[pltpu.VMEM((2, 5, nc * SUBLANES, ML_CHUNK), F32),
                        pltpu.VMEM((nc, ML_CHUNK, LANES), F32),
                        pltpu.VMEM((nc, ML_DIM, ML_CHUNK), F32),
                        pltpu.VMEM((2 * ML_HEADS, PAIR_W, PAIR_W), F32)],
        compiler_params=_cparams("parallel"),
        name="mlstm",
    )(ktok, otok, q_t, v_t, gs_t, ng)


def _gla_kernel(x_ref, gates_ref, w2_ref, ab_ref, ng_ref, out_ref, bc_ref, osum_ref, st_ref):
    T = x_ref.shape[0]
    L = GLA_CHUNK
    NC = T // L
    KP, VP = GLA_K_PAD, GLA_V_PAD
    RB = 256

    ri = lax.broadcasted_iota(jnp.int32, (RB, RB), 0)
    ci = lax.broadcasted_iota(jnp.int32, (RB, RB), 1)
    same = (ri // L) == (ci // L)
    tri_blk = (jnp.where(same & (ci <= ri), 1.0, 0.0).astype(BF16),
               jnp.where(same & (ci >= ri), 1.0, 0.0).astype(BF16))
    for r0 in range(0, T, RB):
        z = jnp.dot(gates_ref[r0:r0 + RB, :].astype(BF16), w2_ref[...],
                    preferred_element_type=F32) + ab_ref[...]
        hi, lo = _split_bf16(_log_sigmoid(z) * (1.0 / GLA_TAU))
        for d in range(2):
            sl = slice(d * KP, (d + 1) * KP)
            bc_ref[r0:r0 + RB, sl] = (jnp.dot(tri_blk[d], hi[:, sl], preferred_element_type=F32)
                                      + jnp.dot(tri_blk[d], lo[:, sl], preferred_element_type=F32))

    osum_ref[...] = jnp.zeros(osum_ref.shape, F32)
    st_ref[...] = jnp.zeros(st_ref.shape, F32)

    ri = lax.broadcasted_iota(jnp.int32, (L, L), 0)
    ci = lax.broadcasted_iota(jnp.int32, (L, L), 1)
    tri = (ci <= ri, ci >= ri)
    qhead = lax.broadcasted_iota(jnp.int32, (1, KP), 1) // GLA_DK
    pair_first = lax.broadcasted_iota(jnp.int32, (1, PAIR_W), 1) < HEAD_DIM
    st_mask = (lax.broadcasted_iota(jnp.int32, (VP, KP), 0) // HEAD_DIM
               == lax.broadcasted_iota(jnp.int32, (VP, KP), 1) // GLA_DK)

    def body(it, carry):
        w = []
        for d in range(2):
            for u in range(GLA_STEPS):
                c = it * GLA_STEPS + u
                r0 = pl.multiple_of((c if d == 0 else NC - 1 - c) * L, L)
                bc = bc_ref[pl.ds(r0, L), d * KP:(d + 1) * KP]
                btot = bc[L - 1:L, :] if d == 0 else bc[0:1, :]
                q = x_ref[pl.ds(r0, L), 0:KP].astype(F32) * (GLA_DK ** -0.5)
                k = x_ref[pl.ds(r0, L), KP:2 * KP].astype(F32)
                v = x_ref[pl.ds(r0, L), 2 * KP:2 * KP + VP]
                q_dec = q * jnp.exp(bc)
                k_dec = (k * jnp.exp(-bc)).astype(BF16)
                k_tail = (k * jnp.exp(btot - bc)).astype(BF16)
                a = [lax.dot_general(jnp.where(qhead == h, q_dec, 0.0).astype(BF16), k_dec, NT_DIMS,
                                     preferred_element_type=F32) for h in range(GLA_HEADS)]
                kv = lax.dot_general(v, k_tail, TN_DIMS, preferred_element_type=F32)
                w.append((d, r0, btot, v, q_dec.astype(BF16), a, kv))
        intra = []
        for d, r0, btot, v, q_dec, a, kv in w:
            ab = [jnp.where(tri[d], a_h, 0.0).astype(BF16) for a_h in a]
            oh = [jnp.dot(a_h, v[:, (h // 2) * PAIR_W:(h // 2 + 1) * PAIR_W], preferred_element_type=F32)
                  for h, a_h in enumerate(ab)]
            blocks = []
            for p in range(VP // PAIR_W):
                heads = [h for h in range(GLA_HEADS) if h // 2 == p]
                blocks.append(jnp.where(pair_first, oh[heads[0]], oh[heads[1]] if len(heads) > 1 else 0.0))
            intra.append(jnp.concatenate(blocks, axis=1))
        for d in range(2):
            st = st_ref[d]
            for (_, r0, btot, v, q_dec, a, kv), o_intra in zip(w[d * GLA_STEPS:(d + 1) * GLA_STEPS],
                                                               intra[d * GLA_STEPS:(d + 1) * GLA_STEPS]):
                o = lax.dot_general(q_dec, st.astype(BF16), NT_DIMS, preferred_element_type=F32)
                osum_ref[pl.ds(r0, L), :] += o + o_intra
                st = st * jnp.exp(btot) + jnp.where(st_mask, kv, 0.0)
            st_ref[d] = st
        return carry

    lax.fori_loop(0, NC // GLA_STEPS, body, 0)

    for r0 in range(0, T, RB):
        hs = osum_ref[r0:r0 + RB, :]
        y = hs * lax.rsqrt(_group_mean_sq(hs, HEAD_DIM) + EPS) * ng_ref[...]
        g = x_ref[r0:r0 + RB, 2 * KP + VP:2 * KP + 2 * VP].astype(F32)
        out_ref[r0:r0 + RB, :] = (y * (g * jax.nn.sigmoid(g))).astype(BF16)


def _gla(x, gates, w2, ab, ng, l, batch):
    n = x.shape[0]
    T = n // batch
    return pl.pallas_call(
        _gla_kernel,
        out_shape=jax.ShapeDtypeStruct((n, GLA_V_PAD), BF16),
        grid=(batch,),
        in_specs=[pl.BlockSpec((T, GLA_COLS), lambda b: (b, 0)),
                  pl.BlockSpec((T, GATE_COLS), lambda b: (b, 0)),
                  _layer_spec(w2, l), _layer_spec(ab, l), _layer_spec(ng, l)],
        out_specs=pl.BlockSpec((T, GLA_V_PAD), lambda b: (b, 0)),
        scratch_shapes=[pltpu.VMEM((T, 2 * GLA_K_PAD), F32),
                        pltpu.VMEM((T, GLA_V_PAD), F32),
                        pltpu.VMEM((2, GLA_V_PAD, GLA_K_PAD), F32)],
        compiler_params=_cparams("parallel"),
        name="gla",
    )(x, gates, w2, ab, ng)


NA_CLASSES = 8
GLA_STEPS = 4
NA_UNROLL = 8


def _na_kernel(q_ref, k_ref, v_ref, bias_ref, out_ref, *, n_heads_in_pair):
    T = q_ref.shape[0]
    W = GRID_W
    rows = T // W
    band = NA_KH * W
    lane = lax.broadcasted_iota(jnp.int32, (1, PAIR_W), 1)
    head_mask = (lane < HEAD_DIM, lane >= HEAD_DIM)
    scale = HEAD_DIM ** -0.5

    def group(gi, carry):
        items = []
        for u in range(NA_UNROLL):
            r = gi * NA_UNROLL + u
            rs = jnp.clip(r - NA_KH // 2, 0, rows - NA_KH)
            cls = jnp.where(r < NA_KH // 2, r, jnp.where(r > rows - NA_KH // 2, r - (rows - NA_KH), NA_KH // 2))
            q0 = pl.multiple_of(r * W, W)
            k0 = pl.multiple_of(rs * W, W)
            q = q_ref[pl.ds(q0, W), :]
            kb = k_ref[pl.ds(k0, band), :]
            for half in range(n_heads_in_pair):
                q_m = jnp.where(head_mask[half], q, jnp.zeros_like(q))
                s = lax.dot_general(q_m, kb, NT_DIMS, preferred_element_type=F32)
                items.append((u, half, q0, k0, cls, s))
        probs = []
        for u, half, q0, k0, cls, s in items:
            s = s * scale + bias_ref[half, cls]
            m = jnp.max(s, axis=1, keepdims=True)
            p = jnp.exp(s - m)
            l = jnp.sum(p, axis=1, keepdims=True)
            probs.append((u, half, q0, k0, p.astype(BF16), l))
        outs = {}
        for u, half, q0, k0, p, l in probs:
            vb = v_ref[pl.ds(k0, band), :]
            outs[(u, half)] = (q0, jnp.dot(p, vb, preferred_element_type=F32) / l)
        for u in range(NA_UNROLL):
            q0, o0 = outs[(u, 0)]
            second = outs[(u, 1)][1] if n_heads_in_pair == 2 else 0.0
            out_ref[pl.ds(q0, W), :] = jnp.where(head_mask[0], o0, second).astype(BF16)
        return carry

    lax.fori_loop(0, rows // NA_UNROLL, group, 0)


def _na(na, bias, l, batch):
    n = na.shape[0]
    T = n // batch
    npairs = NA_PAD // PAIR_W
    outs = []
    for p0, p1, nh in ((0, NA_HEADS // 2, 2), (NA_HEADS // 2, npairs, 1)):
        npp = p1 - p0
        outs.append(pl.pallas_call(
            functools.partial(_na_kernel, n_heads_in_pair=nh),
            out_shape=jax.ShapeDtypeStruct((n, npp * PAIR_W), BF16),
            grid=(npp, batch),
            in_specs=[pl.BlockSpec((T, PAIR_W), lambda p, b, o=p0: (b, o + p)),
                      pl.BlockSpec((T, PAIR_W), lambda p, b, o=npairs + p0: (b, o + p)),
                      pl.BlockSpec((T, PAIR_W), lambda p, b, o=2 * npairs + p0: (b, o + p)),
                      pl.BlockSpec((None, 2, NA_CLASSES, GRID_W, NA_KH * GRID_W),
                                   lambda p, b, o=p0: (l, o + p, 0, 0, 0))],
            out_specs=pl.BlockSpec((T, PAIR_W), lambda p, b: (b, p)),
            compiler_params=_cparams("parallel", "parallel"),
            name=f"natten{nh}",
        )(na, na, na, bias))
    return outs


def _na_bias_table(rpb, rows):
    kh = NA_KH
    rep_rows = np.array(list(range(kh // 2)) + [kh // 2] + list(range(rows - kh // 2 + 1, rows)))
    row_start = np.clip(rep_rows - kh // 2, 0, rows - kh)
    dr = row_start[:, None] + np.arange(kh)[None, :] - rep_rows[:, None] + (kh - 1)
    c = np.arange(GRID_W)
    col_start = np.clip(c - NA_KW // 2, 0, GRID_W - NA_KW)
    in_win = (c[None, :] >= col_start[:, None]) & (c[None, :] < col_start[:, None] + NA_KW)
    dc = np.clip(c[None, :] - c[:, None], -(NA_KW - 1), NA_KW - 1) + (NA_KW - 1)
    sel_r = jnp.asarray(np.eye(2 * kh - 1, dtype=np.float32)[dr])
    sel_c = jnp.asarray(np.eye(2 * NA_KW - 1, dtype=np.float32)[dc])
    depth, heads = rpb.shape[:2]
    rp = rpb.astype(F32).reshape(depth * heads, 2 * kh - 1, 2 * NA_KW - 1)
    tab = jnp.einsum('skr,hrc->hskc', sel_r, rp, precision=lax.Precision.HIGHEST)
    tab = jnp.einsum('hskc,qwc->hsqkw', tab, sel_c, precision=lax.Precision.HIGHEST)
    tab = jnp.where(in_win[None, None, :, None, :], tab, -jnp.inf)
    tab = tab.reshape(depth, heads, len(rep_rows), GRID_W, kh * GRID_W)
    return jnp.pad(tab, ((0, 0), (0, 2 * (NA_PAD // PAIR_W) - heads), (0, 0), (0, 0), (0, 0)))


def _memkv_kernel(m_ref, g_ref, wk_ref, wv_ref, k_ref, v_ref):
    mn = _rms(m_ref[...], g_ref[...]).astype(BF16)
    k_ref[...] = jnp.dot(mn, wk_ref[...], preferred_element_type=F32).astype(BF16)
    v_ref[...] = jnp.dot(mn, wv_ref[...], preferred_element_type=F32).astype(BF16)


def _memkv(mem, g, wk, wv, l):
    n, d = mem.shape
    tm = min(ROW_TILE, n)
    row = pl.BlockSpec((tm, d), lambda i: (i, 0))
    return pl.pallas_call(
        _memkv_kernel,
        out_shape=(jax.ShapeDtypeStruct((n, d), BF16), jax.ShapeDtypeStruct((n, d), BF16)),
        grid=(n // tm,),
        in_specs=[row, _layer_spec(g, l), _layer_spec(wk, l), _layer_spec(wv, l)],
        out_specs=(row, row),
        compiler_params=_cparams("parallel"),
        name="memkv",
    )(mem, g, wk, wv)


def _xattn_kernel(ml_ref, na0_ref, na1_ref, gla_ref, wout_ref, h_ref, g_ref, wq_ref, k_ref, v_ref, wo_ref,
                  o_ref):
    mixed = jnp.concatenate([ml_ref[...], na0_ref[...], na1_ref[...], gla_ref[...]], axis=1)
    o_ref[...] = h_ref[...] + jnp.dot(mixed, wout_ref[...], preferred_element_type=F32)
    h = o_ref[...]
    tm, d = h.shape
    hd = d // X_HEADS
    heads = [slice(i * hd, (i + 1) * hd) for i in range(X_HEADS)]
    blocks = _rms_row_blocks(o_ref, g_ref[...], tm)
    hn = jnp.concatenate(blocks, axis=0)
    q = [(_dot_row_blocks(blocks, wq_ref[:, sl]) if n == 0 else
          jnp.dot(hn, wq_ref[:, sl], preferred_element_type=F32)).astype(BF16) for n, sl in enumerate(heads)]
    scores = [lax.dot_general(q_h, k_ref[:, sl], NT_DIMS, preferred_element_type=F32)
              for q_h, sl in zip(q, heads)]
    probs = []
    for s in scores:
        s = s * (hd ** -0.5)
        p = jnp.exp(s - jnp.max(s, axis=1, keepdims=True))
        probs.append((p.astype(BF16), jnp.sum(p, axis=1, keepdims=True)))
    ctx = [jnp.dot(p, v_ref[:, sl], preferred_element_type=F32) for (p, _), sl in zip(probs, heads)]
    o = jnp.concatenate([(c / l).astype(BF16) for c, (_, l) in zip(ctx, probs)], axis=1)
    o_ref[...] = h + jnp.dot(o, wo_ref[...], preferred_element_type=F32)


def _xattn(mixed, wout, h, g, wq, k, v, wo, l, batch):
    n, d = h.shape
    tm = WIDE_ROW_TILE
    tiles_per_seq = (n // batch) // tm
    n_mem = k.shape[0] // batch
    row = lambda cols: pl.BlockSpec((tm, cols), lambda i: (i, 0))
    kv = pl.BlockSpec((n_mem, d), lambda i: (i // tiles_per_seq, 0))
    return pl.pallas_call(
        _xattn_kernel,
        out_shape=jax.ShapeDtypeStruct((n, d), F32),
        grid=(n // tm,),
        in_specs=[row(m.shape[1]) for m in mixed] + [
            _layer_spec(wout, l), row(d), _layer_spec(g, l), _layer_spec(wq, l), kv, kv, _layer_spec(wo, l)],
        out_specs=row(d),
        compiler_params=_cparams("parallel"),
        name="xattn",
    )(*mixed, wout, h, g, wq, k, v, wo)


FFN_CHUNK = 256


def _gelu_tanh(x):
    c = float(np.sqrt(2.0 / np.pi))
    half = 0.5 * x
    return half + half * jnp.tanh(x * (c + (c * 0.044715) * (x * x)))


def _ffn_kernel(h_ref, hp_ref, hx_ref, g_ref, wup_ref, cw_ref, cb_ref, wdn_ref, gf_ref, o_ref, *,
                tiles_per_seq, final_norm):
    i = pl.program_id(0)
    h = h_ref[...]
    tm = h.shape[0]
    dff = wdn_ref.shape[0]
    halo = jnp.concatenate([hp_ref[...], hx_ref[...]], axis=0)
    blocks = _rms_row_blocks(h_ref, g_ref[...], tm) + [_rms(halo, g_ref[...]).astype(BF16)]
    hn = jnp.concatenate(blocks[:-1], axis=0)
    hn_ext = jnp.concatenate(blocks, axis=0)
    has_prev, has_next = _halo_flags(i, tiles_per_seq)

    def up(c):
        wa = wup_ref[:, c]
        wg = wup_ref[:, slice(dff + c.start, dff + c.stop)]
        return (jnp.dot(hn_ext, wa, preferred_element_type=F32), jnp.dot(hn, wg, preferred_element_type=F32))

    acts = []
    for c in [slice(c, min(c + FFN_CHUNK, dff)) for c in range(0, dff, FFN_CHUNK)]:
        a_ext, gate = up(c)
        ac = _conv3(a_ext, tm, cw_ref[:, c], cb_ref[:, c], has_prev, has_next)
        acts.append((_gelu_tanh(ac) * gate).astype(BF16))
    acc = h + jnp.dot(jnp.concatenate(acts, axis=1), wdn_ref[...], preferred_element_type=F32)
    o_ref[...] = _rms(acc, gf_ref[...]) if final_norm else acc


def _ffn(h, g, wup, cw, cb, wdn, gf, l, batch, final_norm):
    n, d = h.shape
    tm = WIDE_ROW_TILE
    tiles_per_seq = (n // batch) // tm
    row = pl.BlockSpec((tm, d), lambda i: (i, 0))
    return pl.pallas_call(
        functools.partial(_ffn_kernel, tiles_per_seq=tiles_per_seq, final_norm=final_norm),
        out_shape=jax.ShapeDtypeStruct((n, d), F32),
        grid=(n // tm,),
        in_specs=[row] + _halo_specs(tm, d, n) + [
            _layer_spec(a, l) for a in (g, wup, cw, cb, wdn)] + [_const_spec((1, d))],
        out_specs=row,
        compiler_params=_cparams("parallel"),
        name="convffn",
    )(h, h, h, g, wup, cw, cb, wdn, gf)


def _pad_axis(a, axis, n, before=0):
    pads = [(0, 0)] * a.ndim
    pads[axis] = (before, n - before - a.shape[axis])
    return jnp.pad(a, pads)


def _prep_w_in(w):
    offs = np.concatenate([[0], np.cumsum(IN_SIZES)])
    seg = [w[..., offs[i]:offs[i + 1]] for i in range(len(IN_SIZES))]
    ml = seg[0:4]
    na = [_pad_axis(s, -1, NA_PAD) for s in seg[5:8]]
    gla = [_pad_axis(seg[8], -1, GLA_K_PAD), _pad_axis(seg[9], -1, GLA_K_PAD),
           _pad_axis(seg[10], -1, GLA_V_PAD), _pad_axis(seg[11], -1, GLA_V_PAD)]
    gates = [_pad_axis(seg[4][..., j * ML_HEADS:(j + 1) * ML_HEADS], -1, SUBLANES) for j in range(4)]
    gates = _pad_axis(jnp.concatenate(gates + [seg[12]], axis=-1), -1, GATE_COLS)
    return jnp.concatenate(ml + na + gla + [gates], axis=-1).astype(BF16)


def _prep_gate_bias(gb):
    groups = [_pad_axis(gb[:, j:j + 1, :], -1, SUBLANES) for j in range(4)]
    return _pad_axis(jnp.concatenate(groups, axis=-1), -1, GATE_COLS)


def _prep_gla_gate(w2, ab):
    width = 2 * GLA_K_PAD
    blocks = [_pad_axis(w2[:, z], -1, width, before=z * GLA_K_PAD) for z in range(2)]
    w = _pad_axis(jnp.concatenate(blocks, axis=1), 1, GATE_COLS, before=GLA_GATE_OFF)
    b = jnp.concatenate([_pad_axis(ab[:, z:z + 1], -1, GLA_K_PAD) for z in range(2)], axis=-1)
    return w.astype(BF16), b


def _prep_w_out(w):
    parts = [w[:, 0:ML_DIM], _pad_axis(w[:, ML_DIM:ML_DIM + NA_DIM], 1, NA_PAD),
             _pad_axis(w[:, ML_DIM + NA_DIM:], 1, GLA_V_PAD)]
    return jnp.concatenate(parts, axis=1).astype(BF16)


def kernel(x, mem, mix_norm_g, w_in, ml_conv_w, ml_conv_b, ml_gate_b, ml_norm_g, na_rpb,
           gla_a_w2, gla_a_b, gla_norm_g, w_out, xattn_norm_g, mem_norm_g,
           w_xq, w_xk, w_xv, w_xo, ffn_norm_g, w_up, ffn_conv_w, ffn_conv_b, w_down,
           final_norm_g):
    batch, seq, d = x.shape
    depth = w_in.shape[0]
    rows = seq // GRID_W
    assert seq % WIDE_ROW_TILE == 0 and rows >= NA_KH and rows % NA_UNROLL == 0
    assert seq % (GLA_CHUNK * GLA_STEPS) == 0
    h = x.reshape(batch * seq, d)
    mem2 = mem.reshape(batch * mem.shape[1], d)
    row3 = lambda v: v[:, None, :]
    w_in_p = _prep_w_in(w_in)
    gate_b_p = _prep_gate_bias(ml_gate_b)
    na_bias_p = _na_bias_table(na_rpb, rows)
    gla_w2_p, gla_ab_p = _prep_gla_gate(gla_a_w2, gla_a_b)
    gla_ng_p = row3(_pad_axis(gla_norm_g, -1, GLA_V_PAD))
    w_out_p = _prep_w_out(w_out)
    w_xq_b, w_xk_b, w_xv_b, w_xo_b, w_up_b, w_down_b = (
        w.astype(BF16) for w in (w_xq, w_xk, w_xv, w_xo, w_up, w_down))
    mix_g, ml_cb, ml_ng, x_g, mem_g, ffn_g, ffn_cb = (
        row3(v) for v in (mix_norm_g, ml_conv_b, ml_norm_g, xattn_norm_g, mem_norm_g, ffn_norm_g, ffn_conv_b))
    for l in range(depth):
        ktok, otok, q_t, v_t, gs_t, na, gla, gates = _inproj(
            h, mix_g, w_in_p, ml_conv_w, ml_cb, gate_b_p, l, batch)
        ml_out = _mlstm(ktok, otok, q_t, v_t, gs_t, ml_ng, l, batch)
        na0, na1 = _na(na, na_bias_p, l, batch)
        gla_out = _gla(gla, gates, gla_w2_p, gla_ab_p, gla_ng_p, l, batch)
        k, v = _memkv(mem2, mem_g, w_xk_b, w_xv_b, l)
        h = _xattn((ml_out, na0, na1, gla_out), w_out_p, h, x_g, w_xq_b, k, v, w_xo_b, l, batch)
        h = _ffn(h, ffn_g, w_up_b, ffn_conv_w, ffn_cb, w_down_b, final_norm_g.reshape(1, d), l, batch,
                 final_norm=(l == depth - 1))
    return h.reshape(batch, seq, d)
```

```python
import functools

import numpy as np
import jax
import jax.numpy as jnp
from jax import lax
from jax.experimental import pallas as pl
from jax.experimental.pallas import tpu as pltpu

F32 = jnp.float32
BF16 = jnp.bfloat16

HEAD_DIM = 64
ML_HEADS = 6
NA_HEADS = 5
GLA_HEADS = 5
ML_DIM = ML_HEADS * HEAD_DIM
NA_DIM = NA_HEADS * HEAD_DIM
GLA_DK = 32
GLA_KDIM = GLA_HEADS * GLA_DK
GLA_VDIM = GLA_HEADS * HEAD_DIM
GLA_RANK = 16
GLA_TAU = 16.0
GRID_W = 64
NA_KH = 8
NA_KW = 16
ML_CHUNK = 128
GLA_CHUNK = 64
X_HEADS = 4
EPS = 1e-6
IN_SIZES = (ML_DIM, ML_DIM, ML_DIM, ML_DIM, 4 * ML_HEADS,
            NA_DIM, NA_DIM, NA_DIM,
            GLA_KDIM, GLA_KDIM, GLA_VDIM, GLA_VDIM, 2 * GLA_RANK)

LANES = 128
SUBLANES = 8

PAIR_W = 2 * HEAD_DIM
ML_COLS = 4 * ML_DIM
NA_PAD = 3 * PAIR_W
NA_COLS = 3 * NA_PAD
GLA_K_PAD = 256
GLA_V_PAD = 3 * PAIR_W
GLA_COLS = 2 * GLA_K_PAD + 2 * GLA_V_PAD
GATE_COLS = LANES
GLA_GATE_OFF = 4 * SUBLANES
IN_COLS_PAD = ML_COLS + NA_COLS + GLA_COLS + GATE_COLS

ROW_TILE = 512
WIDE_ROW_TILE = 1024
NORM_BLOCK = 128
VMEM_LIMIT = 56 * 1024 * 1024

NT_DIMS = (((1,), (1,)), ((), ()))
TN_DIMS = (((0,), (0,)), ((), ()))


def _cparams(*sem):
    return pltpu.CompilerParams(dimension_semantics=sem, vmem_limit_bytes=VMEM_LIMIT)


def _const_spec(shape):
    nd = len(shape)
    return pl.BlockSpec(shape, lambda *_: (0,) * nd)


def _layer_spec(a, l):
    nd = a.ndim - 1
    return pl.BlockSpec((None,) + a.shape[1:], lambda *_, l=l, nd=nd: (l,) + (0,) * nd,
                        pipeline_mode=pl.Buffered(1))


def _rms(x, g):
    ms = jnp.mean(x * x, axis=-1, keepdims=True)
    return x * lax.rsqrt(ms + EPS) * g


def _rms_row_blocks(x_ref, g, tm):
    return [_rms(x_ref[r0:r0 + NORM_BLOCK, :], g).astype(BF16) for r0 in range(0, tm, NORM_BLOCK)]


def _dot_row_blocks(blocks, w):
    return jnp.concatenate([jnp.dot(b, w, preferred_element_type=F32) for b in blocks], axis=0)


def _log_sigmoid(x):
    return jnp.minimum(x, 0.0) - jnp.log(1.0 + jnp.exp(-jnp.abs(x)))


def _split_bf16(x):
    hi = x.astype(BF16)
    lo = (x - hi.astype(F32)).astype(BF16)
    return hi, lo


def _group_mean_sq(x, group):
    n = x.shape[-1]
    r = lax.broadcasted_iota(jnp.int32, (n, n), 0) // group
    c = lax.broadcasted_iota(jnp.int32, (n, n), 1) // group
    ones = jnp.where(r == c, 1.0, 0.0).astype(BF16)
    hi, lo = _split_bf16(x * x)
    s = jnp.dot(hi, ones, preferred_element_type=F32) + jnp.dot(lo, ones, preferred_element_type=F32)
    return s * (1.0 / group)


def _halo_flags(i, tiles_per_seq):
    has_prev = (i % tiles_per_seq != 0).astype(F32)
    has_next = (i % tiles_per_seq != tiles_per_seq - 1).astype(F32)
    return has_prev, has_next


def _conv3(a_ext, tm, w, b, has_prev, has_next):
    row = lax.broadcasted_iota(jnp.int32, (SUBLANES, 1), 0)
    a = a_ext[0:tm]
    down = pltpu.roll(a, 1, 0)
    up = pltpu.roll(a, tm - 1, 0)
    prev_row = a_ext[tm + SUBLANES - 1:tm + SUBLANES, :] * has_prev
    next_row = a_ext[tm + SUBLANES:tm + SUBLANES + 1, :] * has_next
    a_prev = jnp.concatenate([jnp.where(row == 0, prev_row, down[0:SUBLANES]), down[SUBLANES:]], axis=0)
    a_next = jnp.concatenate([up[:tm - SUBLANES],
                              jnp.where(row == SUBLANES - 1, next_row, up[tm - SUBLANES:])], axis=0)
    return a_prev * w[0:1, :] + a * w[1:2, :] + a_next * w[2:3, :] + b


def _inproj_kernel(h_ref, hp_ref, hx_ref, g_ref, w_ref, cw_ref, cb_ref, gb_ref,
                   ktok_ref, otok_ref, qT_ref, vT_ref, gsT_ref, na_ref, gla_ref, gate_ref, *, tiles_per_seq):
    tm = h_ref.shape[0]
    L = ML_CHUNK
    has_prev, has_next = _halo_flags(pl.program_id(0), tiles_per_seq)
    halo = jnp.concatenate([hp_ref[...], hx_ref[...]], axis=0)
    blocks = _rms_row_blocks(h_ref, g_ref[...], tm) + [_rms(halo, g_ref[...]).astype(BF16)]
    xn = jnp.concatenate(blocks[:-1], axis=0)
    xn_ext = jnp.concatenate(blocks, axis=0)

    def proj(lhs, c0, c1):
        return jnp.dot(lhs, w_ref[:, c0:c1], preferred_element_type=F32)

    for part in range(2):
        cols = slice(part * ML_DIM, (part + 1) * ML_DIM)
        a_ext = _dot_row_blocks(blocks, w_ref[:, cols]) if part == 0 else proj(xn_ext, cols.start, cols.stop)
        y = _conv3(a_ext, tm, cw_ref[:, cols], cb_ref[:, cols], has_prev, has_next)
        y = y * jax.nn.sigmoid(y)
        if part == 0:
            y_t = y.T
            for j in range(tm // L):
                qT_ref[j] = y_t[:, j * L:(j + 1) * L].astype(BF16)
        else:
            ktok_ref[...] = (y * HEAD_DIM ** -0.5).astype(BF16)
    v_t = proj(xn, 2 * ML_DIM, 3 * ML_DIM).T
    for j in range(tm // L):
        vT_ref[j] = v_t[:, j * L:(j + 1) * L].astype(BF16)
    otok_ref[...] = proj(xn, 3 * ML_DIM, 4 * ML_DIM).astype(BF16)
    off = ML_COLS
    for ref, chunk in ((na_ref, 384), (gla_ref, 256), (gate_ref, 128)):
        width = ref.shape[-1]
        for c in range(0, width, chunk):
            ref[:, c:c + chunk] = proj(xn, off + c, off + c + chunk).astype(ref.dtype)
        off += width
    g_t = (gate_ref[...] + gb_ref[...]).T
    for j in range(tm // L):
        gsT_ref[j] = g_t[0:4 * SUBLANES, j * L:(j + 1) * L]


def _halo_specs(tm, d, n):
    hb = tm // SUBLANES
    last = n // SUBLANES - 1
    return [pl.BlockSpec((SUBLANES, d), lambda i: (jnp.maximum(i * hb - 1, 0), 0)),
            pl.BlockSpec((SUBLANES, d), lambda i: (jnp.minimum((i + 1) * hb, last), 0))]


def _inproj(h, g, w, cw, cb, gb, l, batch):
    n, d = h.shape
    tm = WIDE_ROW_TILE
    nc = tm // ML_CHUNK
    row = lambda cols: pl.BlockSpec((tm, cols), lambda i: (i, 0))
    slab = lambda rows: pl.BlockSpec((nc, rows, ML_CHUNK), lambda i: (i, 0, 0))
    return pl.pallas_call(
        functools.partial(_inproj_kernel, tiles_per_seq=(n // batch) // tm),
        out_shape=(jax.ShapeDtypeStruct((n, ML_DIM), BF16),
                   jax.ShapeDtypeStruct((n, ML_DIM), BF16),
                   jax.ShapeDtypeStruct((n // ML_CHUNK, ML_DIM, ML_CHUNK), BF16),
                   jax.ShapeDtypeStruct((n // ML_CHUNK, ML_DIM, ML_CHUNK), BF16),
                   jax.ShapeDtypeStruct((n // ML_CHUNK, 4 * SUBLANES, ML_CHUNK), F32),
                   jax.ShapeDtypeStruct((n, NA_COLS), BF16),
                   jax.ShapeDtypeStruct((n, GLA_COLS), BF16),
                   jax.ShapeDtypeStruct((n, GATE_COLS), F32)),
        grid=(n // tm,),
        in_specs=[row(d)] + _halo_specs(tm, d, n) + [_layer_spec(a, l) for a in (g, w, cw, cb, gb)],
        out_specs=(row(ML_DIM), row(ML_DIM), slab(ML_DIM), slab(ML_DIM), slab(4 * SUBLANES),
                   row(NA_COLS), row(GLA_COLS), row(GATE_COLS)),
        compiler_params=_cparams("parallel"),
        name="inproj",
    )(h, h, h, g, w, cw, cb, gb)


def _lane_scan(x, op, fill, reverse):
    n = x.shape[-1]
    lane = lax.broadcasted_iota(jnp.int32, x.shape, 1)
    s = 1
    while s < n:
        if reverse:
            x = op(x, jnp.where(lane < n - s, pltpu.roll(x, n - s, 1), fill))
        else:
            x = op(x, jnp.where(lane >= s, pltpu.roll(x, s, 1), fill))
        s *= 2
    return x


def _mlstm_kernel(ktok_ref, otok_ref, qT_ref, vT_ref, gsT_ref, ng_ref, out_ref,
                  pre_ref, acol_ref, hsumT_ref, ct_ref):
    NC, _, L = qT_ref.shape

    def gate_group(grp):
        return gsT_ref[:, grp * SUBLANES:(grp + 1) * SUBLANES, :].reshape(NC * SUBLANES, L)

    a_all = []
    for d in range(2):
        logf = _log_sigmoid(gate_group(2 * d + 1))
        b = _lane_scan(logf, jnp.add, 0.0, reverse=(d == 1))
        a = gate_group(2 * d) - b
        pre_ref[d, 0] = b
        pre_ref[d, 1] = a
        pre_ref[d, 2] = _lane_scan(a, jnp.maximum, -jnp.inf, reverse=(d == 1))
        pre_ref[d, 3] = jnp.broadcast_to(jnp.sum(logf, axis=1, keepdims=True), a.shape)
        pre_ref[d, 4] = jnp.broadcast_to(jnp.max(a, axis=1, keepdims=True), a.shape)
        a_all.append(a)
    a_pad = jnp.zeros((L - 2 * SUBLANES, L), F32)
    for c in range(NC):
        rows = slice(c * SUBLANES, (c + 1) * SUBLANES)
        acol_ref[c] = jnp.concatenate([a_all[0][rows], a_all[1][rows], a_pad], axis=0).T

    hsumT_ref[...] = jnp.zeros(hsumT_ref.shape, F32)
    ct_ref[...] = jnp.zeros(ct_ref.shape, F32)

    sub = lax.broadcasted_iota(jnp.int32, (PAIR_W, L), 0)
    head_rows = (sub < HEAD_DIM, sub >= HEAD_DIM)
    ri = lax.broadcasted_iota(jnp.int32, (L, L), 0)
    ci = lax.broadcasted_iota(jnp.int32, (L, L), 1)
    tri = (ri <= ci, ri >= ci)

    def body(it, carry):
        m_reps = list(carry)
        chunk = (it, NC - 1 - it)
        gq = []
        for d in range(2):
            r8 = pl.ds(pl.multiple_of(chunk[d] * SUBLANES, SUBLANES), SUBLANES)
            b, a, amax_run, g, amax = (pre_ref[d, n, r8, :] for n in range(5))
            m_old = m_reps[d]
            big_m = jnp.maximum(m_old, amax_run)
            m_new = jnp.maximum(g + m_old, g + amax)
            m_reps[d] = m_new
            gq.append(dict(big_m=big_m, w_inter=jnp.exp(m_old - big_m), e=jnp.exp(-(b + big_m)),
                           wk=jnp.exp(g + a - m_new), decay=jnp.exp(g + m_old - m_new),
                           a_cols=acol_ref[chunk[d]]))

        items = []
        for d in range(2):
            c = chunk[d]
            r0 = pl.multiple_of(c * L, L)
            for p in range(ML_HEADS // 2):
                rows = slice(p * PAIR_W, (p + 1) * PAIR_W)
                k_p = ktok_ref[pl.ds(r0, L), rows]
                q_tp = qT_ref[c, rows, :]
                v_tp = vT_ref[c, rows, :]
                for half in range(2):
                    h = 2 * p + half
                    q_tm = jnp.where(head_rows[half], q_tp, jnp.zeros_like(q_tp))
                    v_ta = jnp.where(head_rows[half], v_tp, jnp.ones_like(v_tp))
                    ct_old = ct_ref[d * ML_HEADS + h]
                    s_t = jnp.dot(k_p, q_tm, preferred_element_type=F32)
                    inter_t = jnp.dot(ct_old.astype(BF16), q_tm, preferred_element_type=F32)
                    v_tw = (v_ta.astype(F32) * gq[d]["wk"][h:h + 1, :]).astype(BF16)
                    kv_t = jnp.dot(v_tw, k_p, preferred_element_type=F32)
                    items.append((d, c, p, half, h, v_ta, ct_old, s_t, inter_t, kv_t))
        s2 = []
        for d, c, p, half, h, v_ta, ct_old, s_t, inter_t, kv_t in items:
            dm = jnp.where(tri[d], gq[d]["a_cols"][:, SUBLANES * d + h:SUBLANES * d + h + 1]
                           - gq[d]["big_m"][h:h + 1, :], -jnp.inf)
            s2.append((s_t * jnp.exp(dm)).astype(BF16))
        num = [jnp.dot(it_[5], s, preferred_element_type=F32) for it_, s in zip(items, s2)]
        for idx in range(0, len(items), 2):
            outs = []
            for (d, c, p, half, h, v_ta, ct_old, s_t, inter_t, kv_t), n_t in zip(items[idx:idx + 2],
                                                                                 num[idx:idx + 2]):
                n_t = n_t + gq[d]["w_inter"][h:h + 1, :] * inter_t
                den = jnp.concatenate([n_t[HEAD_DIM:], n_t[:HEAD_DIM]], axis=0)
                outs.append(n_t / jnp.maximum(jnp.abs(den), gq[d]["e"][h:h + 1, :]))
                ct_ref[d * ML_HEADS + h] = gq[d]["decay"][h:h + 1, :] * ct_old + kv_t
            hsumT_ref[c, p * PAIR_W:(p + 1) * PAIR_W, :] += jnp.where(head_rows[0], outs[0], outs[1])
        return tuple(m_reps)

    m0 = jnp.zeros((SUBLANES, L), F32)
    lax.fori_loop(0, NC, body, (m0, m0))

    for c in range(NC):
        hs_t = hsumT_ref[c]
        parts = []
        for h in range(ML_HEADS):
            blk = hs_t[h * HEAD_DIM:(h + 1) * HEAD_DIM, :]
            ms = jnp.mean(blk * blk, axis=0, keepdims=True)
            parts.append(blk * lax.rsqrt(ms + EPS))
        y = jnp.concatenate(parts, axis=0).T * ng_ref[...]
        o = otok_ref[c * L:(c + 1) * L, :].astype(F32)
        out_ref[c * L:(c + 1) * L, :] = (y * jax.nn.sigmoid(o)).astype(BF16)


def _mlstm(ktok, otok, q_t, v_t, gs_t, ng, l, batch):
    n = ktok.shape[0]
    T = n // batch
    nc = T // ML_CHUNK
    tok = pl.BlockSpec((T, ML_DIM), lambda b: (b, 0))
    slab = lambda rows: pl.BlockSpec((nc, rows, ML_CHUNK), lambda b: (b, 0, 0))
    return pl.pallas_call(
        _mlstm_kernel,
        out_shape=jax.ShapeDtypeStruct((n, ML_DIM), BF16),
        grid=(batch,),
        in_specs=[tok, tok, slab(ML_DIM), slab(ML_DIM), slab(4 * SUBLANES), _layer_spec(ng, l)],
        out_specs=tok,
        scratch_shapes=[pltpu.VMEM((2, 5, nc * SUBLANES, ML_CHUNK), F32),
                        pltpu.VMEM((nc, ML_CHUNK, LANES), F32),
                        pltpu.VMEM((nc, ML_DIM, ML_CHUNK), F32),
                        pltpu.VMEM((2 * ML_HEADS, PAIR_W, PAIR_W), F32)],
        compiler_params=_cparams("parallel"),
        name="mlstm",
    )(ktok, otok, q_t, v_t, gs_t, ng)


def _gla_kernel(x_ref, gates_ref, w2_ref, ab_ref, ng_ref, out_ref, bc_ref, osum_ref, st_ref):
    T = x_ref.shape[0]
    L = GLA_CHUNK
    NC = T // L
    KP, VP = GLA_K_PAD, GLA_V_PAD
    RB = 256

    ri = lax.broadcasted_iota(jnp.int32, (RB, RB), 0)
    ci = lax.broadcasted_iota(jnp.int32, (RB, RB), 1)
    same = (ri // L) == (ci // L)
    tri_blk = (jnp.where(same & (ci <= ri), 1.0, 0.0).astype(BF16),
               jnp.where(same & (ci >= ri), 1.0, 0.0).astype(BF16))
    for r0 in range(0, T, RB):
        z = jnp.dot(gates_ref[r0:r0 + RB, :].astype(BF16), w2_ref[...],
                    preferred_element_type=F32) + ab_ref[...]
        hi, lo = _split_bf16(_log_sigmoid(z) * (1.0 / GLA_TAU))
        for d in range(2):
            sl = slice(d * KP, (d + 1) * KP)
            bc_ref[r0:r0 + RB, sl] = (jnp.dot(tri_blk[d], hi[:, sl], preferred_element_type=F32)
                                      + jnp.dot(tri_blk[d], lo[:, sl], preferred_element_type=F32))

    osum_ref[...] = jnp.zeros(osum_ref.shape, F32)
    st_ref[...] = jnp.zeros(st_ref.shape, F32)

    ri = lax.broadcasted_iota(jnp.int32, (L, L), 0)
    ci = lax.broadcasted_iota(jnp.int32, (L, L), 1)
    tri = (ci <= ri, ci >= ri)
    qhead = lax.broadcasted_iota(jnp.int32, (1, KP), 1) // GLA_DK
    pair_first = lax.broadcasted_iota(jnp.int32, (1, PAIR_W), 1) < HEAD_DIM
    st_mask = (lax.broadcasted_iota(jnp.int32, (VP, KP), 0) // HEAD_DIM
               == lax.broadcasted_iota(jnp.int32, (VP, KP), 1) // GLA_DK)

    def body(it, carry):
        w = []
        for d in range(2):
            for u in range(GLA_STEPS):
                c = it * GLA_STEPS + u
                r0 = pl.multiple_of((c if d == 0 else NC - 1 - c) * L, L)
                bc = bc_ref[pl.ds(r0, L), d * KP:(d + 1) * KP]
                btot = bc[L - 1:L, :] if d == 0 else bc[0:1, :]
                q = x_ref[pl.ds(r0, L), 0:KP].astype(F32) * (GLA_DK ** -0.5)
                k = x_ref[pl.ds(r0, L), KP:2 * KP].astype(F32)
                v = x_ref[pl.ds(r0, L), 2 * KP:2 * KP + VP]
                q_dec = q * jnp.exp(bc)
                k_dec = (k * jnp.exp(-bc)).astype(BF16)
                k_tail = (k * jnp.exp(btot - bc)).astype(BF16)
                a = [lax.dot_general(jnp.where(qhead == h, q_dec, 0.0).astype(BF16), k_dec, NT_DIMS,
                                     preferred_element_type=F32) for h in range(GLA_HEADS)]
                kv = lax.dot_general(v, k_tail, TN_DIMS, preferred_element_type=F32)
                w.append((d, r0, btot, v, q_dec.astype(BF16), a, kv))
        intra = []
        for d, r0, btot, v, q_dec, a, kv in w:
            ab = [jnp.where(tri[d], a_h, 0.0).astype(BF16) for a_h in a]
            oh = [jnp.dot(a_h, v[:, (h // 2) * PAIR_W:(h // 2 + 1) * PAIR_W], preferred_element_type=F32)
                  for h, a_h in enumerate(ab)]
            blocks = []
            for p in range(VP // PAIR_W):
                heads = [h for h in range(GLA_HEADS) if h // 2 == p]
                blocks.append(jnp.where(pair_first, oh[heads[0]], oh[heads[1]] if len(heads) > 1 else 0.0))
            intra.append(jnp.concatenate(blocks, axis=1))
        for d in range(2):
            st = st_ref[d]
            for (_, r0, btot, v, q_dec, a, kv), o_intra in zip(w[d * GLA_STEPS:(d + 1) * GLA_STEPS],
                                                               intra[d * GLA_STEPS:(d + 1) * GLA_STEPS]):
                o = lax.dot_general(q_dec, st.astype(BF16), NT_DIMS, preferred_element_type=F32)
                osum_ref[pl.ds(r0, L), :] += o + o_intra
                st = st * jnp.exp(btot) + jnp.where(st_mask, kv, 0.0)
            st_ref[d] = st
        return carry

    lax.fori_loop(0, NC // GLA_STEPS, body, 0)

    for r0 in range(0, T, RB):
        hs = osum_ref[r0:r0 + RB, :]
        y = hs * lax.rsqrt(_group_mean_sq(hs, HEAD_DIM) + EPS) * ng_ref[...]
        g = x_ref[r0:r0 + RB, 2 * KP + VP:2 * KP + 2 * VP].astype(F32)
        out_ref[r0:r0 + RB, :] = (y * (g * jax.nn.sigmoid(g))).astype(BF16)


def _gla(x, gates, w2, ab, ng, l, batch):
    n = x.shape[0]
    T = n // batch
    return pl.pallas_call(
        _gla_kernel,
        out_shape=jax.ShapeDtypeStruct((n, GLA_V_PAD), BF16),
        grid=(batch,),
        in_specs=[pl.BlockSpec((T, GLA_COLS), lambda b: (b, 0)),
                  pl.BlockSpec((T, GATE_COLS), lambda b: (b, 0)),
                  _layer_spec(w2, l), _layer_spec(ab, l), _layer_spec(ng, l)],
        out_specs=pl.BlockSpec((T, GLA_V_PAD), lambda b: (b, 0)),
        scratch_shapes=[pltpu.VMEM((T, 2 * GLA_K_PAD), F32),
                        pltpu.VMEM((T, GLA_V_PAD), F32),
                        pltpu.VMEM((2, GLA_V_PAD, GLA_K_PAD), F32)],
        compiler_params=_cparams("parallel"),
        name="gla",
    )(x, gates, w2, ab, ng)


NA_CLASSES = 8
GLA_STEPS = 8
NA_UNROLL = 16


def _na_kernel(q_ref, k_ref, v_ref, bias_ref, out_ref, *, n_heads_in_pair):
    T = q_ref.shape[0]
    W = GRID_W
    rows = T // W
    band = NA_KH * W
    lane = lax.broadcasted_iota(jnp.int32, (1, PAIR_W), 1)
    head_mask = (lane < HEAD_DIM, lane >= HEAD_DIM)
    scale = HEAD_DIM ** -0.5

    def group(gi, carry):
        items = []
        for u in range(NA_UNROLL):
            r = gi * NA_UNROLL + u
            rs = jnp.clip(r - NA_KH // 2, 0, rows - NA_KH)
            cls = jnp.where(r < NA_KH // 2, r, jnp.where(r > rows - NA_KH // 2, r - (rows - NA_KH), NA_KH // 2))
            q0 = pl.multiple_of(r * W, W)
            k0 = pl.multiple_of(rs * W, W)
            q = q_ref[pl.ds(q0, W), :]
            kb = k_ref[pl.ds(k0, band), :]
            for half in range(n_heads_in_pair):
                q_m = jnp.where(head_mask[half], q, jnp.zeros_like(q))
                s = lax.dot_general(q_m, kb, NT_DIMS, preferred_element_type=F32)
                items.append((u, half, q0, k0, cls, s))
        probs = []
        for u, half, q0, k0, cls, s in items:
            s = s * scale + bias_ref[half, cls]
            m = jnp.max(s, axis=1, keepdims=True)
            p = jnp.exp(s - m)
            l = jnp.sum(p, axis=1, keepdims=True)
            probs.append((u, half, q0, k0, p.astype(BF16), l))
        outs = {}
        for u, half, q0, k0, p, l in probs:
            vb = v_ref[pl.ds(k0, band), :]
            outs[(u, half)] = (q0, jnp.dot(p, vb, preferred_element_type=F32) / l)
        for u in range(NA_UNROLL):
            q0, o0 = outs[(u, 0)]
            second = outs[(u, 1)][1] if n_heads_in_pair == 2 else 0.0
            out_ref[pl.ds(q0, W), :] = jnp.where(head_mask[0], o0, second).astype(BF16)
        return carry

    lax.fori_loop(0, rows // NA_UNROLL, group, 0)


def _na(na, bias, l, batch):
    n = na.shape[0]
    T = n // batch
    npairs = NA_PAD // PAIR_W
    outs = []
    for p0, p1, nh in ((0, NA_HEADS // 2, 2), (NA_HEADS // 2, npairs, 1)):
        npp = p1 - p0
        outs.append(pl.pallas_call(
            functools.partial(_na_kernel, n_heads_in_pair=nh),
            out_shape=jax.ShapeDtypeStruct((n, npp * PAIR_W), BF16),
            grid=(npp, batch),
            in_specs=[pl.BlockSpec((T, PAIR_W), lambda p, b, o=p0: (b, o + p)),
                      pl.BlockSpec((T, PAIR_W), lambda p, b, o=npairs + p0: (b, o + p)),
                      pl.BlockSpec((T, PAIR_W), lambda p, b, o=2 * npairs + p0: (b, o + p)),
                      pl.BlockSpec((None, 2, NA_CLASSES, GRID_W, NA_KH * GRID_W),
                                   lambda p, b, o=p0: (l, o + p, 0, 0, 0))],
            out_specs=pl.BlockSpec((T, PAIR_W), lambda p, b: (b, p)),
            compiler_params=_cparams("parallel", "parallel"),
            name=f"natten{nh}",
        )(na, na, na, bias))
    return outs


def _na_bias_table(rpb, rows):
    kh = NA_KH
    rep_rows = np.array(list(range(kh // 2)) + [kh // 2] + list(range(rows - kh // 2 + 1, rows)))
    row_start = np.clip(rep_rows - kh // 2, 0, rows - kh)
    dr = row_start[:, None] + np.arange(kh)[None, :] - rep_rows[:, None] + (kh - 1)
    c = np.arange(GRID_W)
    col_start = np.clip(c - NA_KW // 2, 0, GRID_W - NA_KW)
    in_win = (c[None, :] >= col_start[:, None]) & (c[None, :] < col_start[:, None] + NA_KW)
    dc = np.clip(c[None, :] - c[:, None], -(NA_KW - 1), NA_KW - 1) + (NA_KW - 1)
    sel_r = jnp.asarray(np.eye(2 * kh - 1, dtype=np.float32)[dr])
    sel_c = jnp.asarray(np.eye(2 * NA_KW - 1, dtype=np.float32)[dc])
    depth, heads = rpb.shape[:2]
    rp = rpb.astype(F32).reshape(depth * heads, 2 * kh - 1, 2 * NA_KW - 1)
    tab = jnp.einsum('skr,hrc->hskc', sel_r, rp, precision=lax.Precision.HIGHEST)
    tab = jnp.einsum('hskc,qwc->hsqkw', tab, sel_c, precision=lax.Precision.HIGHEST)
    tab = jnp.where(in_win[None, None, :, None, :], tab, -jnp.inf)
    tab = tab.reshape(depth, heads, len(rep_rows), GRID_W, kh * GRID_W)
    return jnp.pad(tab, ((0, 0), (0, 2 * (NA_PAD // PAIR_W) - heads), (0, 0), (0, 0), (0, 0)))


def _memkv_kernel(m_ref, g_ref, wk_ref, wv_ref, k_ref, v_ref):
    mn = _rms(m_ref[...], g_ref[...]).astype(BF16)
    k_ref[...] = jnp.dot(mn, wk_ref[...], preferred_element_type=F32).astype(BF16)
    v_ref[...] = jnp.dot(mn, wv_ref[...], preferred_element_type=F32).astype(BF16)


def _memkv(mem, g, wk, wv, l):
    n, d = mem.shape
    tm = min(ROW_TILE, n)
    row = pl.BlockSpec((tm, d), lambda i: (i, 0))
    return pl.pallas_call(
        _memkv_kernel,
        out_shape=(jax.ShapeDtypeStruct((n, d), BF16), jax.ShapeDtypeStruct((n, d), BF16)),
        grid=(n // tm,),
        in_specs=[row, _layer_spec(g, l), _layer_spec(wk, l), _layer_spec(wv, l)],
        out_specs=(row, row),
        compiler_params=_cparams("parallel"),
        name="memkv",
    )(mem, g, wk, wv)


def _xattn_kernel(ml_ref, na0_ref, na1_ref, gla_ref, wout_ref, h_ref, g_ref, wq_ref, k_ref, v_ref, wo_ref,
                  o_ref):
    mixed = jnp.concatenate([ml_ref[...], na0_ref[...], na1_ref[...], gla_ref[...]], axis=1)
    o_ref[...] = h_ref[...] + jnp.dot(mixed, wout_ref[...], preferred_element_type=F32)
    h = o_ref[...]
    tm, d = h.shape
    hd = d // X_HEADS
    heads = [slice(i * hd, (i + 1) * hd) for i in range(X_HEADS)]
    blocks = _rms_row_blocks(o_ref, g_ref[...], tm)
    hn = jnp.concatenate(blocks, axis=0)
    q = [(_dot_row_blocks(blocks, wq_ref[:, sl]) if n == 0 else
          jnp.dot(hn, wq_ref[:, sl], preferred_element_type=F32)).astype(BF16) for n, sl in enumerate(heads)]
    scores = [lax.dot_general(q_h, k_ref[:, sl], NT_DIMS, preferred_element_type=F32)
              for q_h, sl in zip(q, heads)]
    probs = []
    for s in scores:
        s = s * (hd ** -0.5)
        p = jnp.exp(s - jnp.max(s, axis=1, keepdims=True))
        probs.append((p.astype(BF16), jnp.sum(p, axis=1, keepdims=True)))
    ctx = [jnp.dot(p, v_ref[:, sl], preferred_element_type=F32) for (p, _), sl in zip(probs, heads)]
    o = jnp.concatenate([(c / l).astype(BF16) for c, (_, l) in zip(ctx, probs)], axis=1)
    o_ref[...] = h + jnp.dot(o, wo_ref[...], preferred_element_type=F32)


def _xattn(mixed, wout, h, g, wq, k, v, wo, l, batch):
    n, d = h.shape
    tm = WIDE_ROW_TILE
    tiles_per_seq = (n // batch) // tm
    n_mem = k.shape[0] // batch
    row = lambda cols: pl.BlockSpec((tm, cols), lambda i: (i, 0))
    kv = pl.BlockSpec((n_mem, d), lambda i: (i // tiles_per_seq, 0))
    return pl.pallas_call(
        _xattn_kernel,
        out_shape=jax.ShapeDtypeStruct((n, d), F32),
        grid=(n // tm,),
        in_specs=[row(m.shape[1]) for m in mixed] + [
            _layer_spec(wout, l), row(d), _layer_spec(g, l), _layer_spec(wq, l), kv, kv, _layer_spec(wo, l)],
        out_specs=row(d),
        compiler_params=_cparams("parallel"),
        name="xattn",
    )(*mixed, wout, h, g, wq, k, v, wo)


FFN_CHUNK = 256


def _gelu_tanh(x):
    c = float(np.sqrt(2.0 / np.pi))
    half = 0.5 * x
    return half + half * jnp.tanh(x * (c + (c * 0.044715) * (x * x)))


def _ffn_kernel(h_ref, hp_ref, hx_ref, g_ref, wup_ref, cw_ref, cb_ref, wdn_ref, gf_ref, o_ref, *,
                tiles_per_seq, final_norm):
    i = pl.program_id(0)
    h = h_ref[...]
    tm = h.shape[0]
    dff = wdn_ref.shape[0]
    halo = jnp.concatenate([hp_ref[...], hx_ref[...]], axis=0)
    blocks = _rms_row_blocks(h_ref, g_ref[...], tm) + [_rms(halo, g_ref[...]).astype(BF16)]
    hn = jnp.concatenate(blocks[:-1], axis=0)
    hn_ext = jnp.concatenate(blocks, axis=0)
    has_prev, has_next = _halo_flags(i, tiles_per_seq)

    def up(c):
        wa = wup_ref[:, c]
        wg = wup_ref[:, slice(dff + c.start, dff + c.stop)]
        return (jnp.dot(hn_ext, wa, preferred_element_type=F32), jnp.dot(hn, wg, preferred_element_type=F32))

    acts = []
    for c in [slice(c, min(c + FFN_CHUNK, dff)) for c in range(0, dff, FFN_CHUNK)]:
        a_ext, gate = up(c)
        ac = _conv3(a_ext, tm, cw_ref[:, c], cb_ref[:, c], has_prev, has_next)
        acts.append((_gelu_tanh(ac) * gate).astype(BF16))
    acc = h + jnp.dot(jnp.concatenate(acts, axis=1), wdn_ref[...], preferred_element_type=F32)
    o_ref[...] = _rms(acc, gf_ref[...]) if final_norm else acc


def _ffn(h, g, wup, cw, cb, wdn, gf, l, batch, final_norm):
    n, d = h.shape
    tm = WIDE_ROW_TILE
    tiles_per_seq = (n // batch) // tm
    row = pl.BlockSpec((tm, d), lambda i: (i, 0))
    return pl.pallas_call(
        functools.partial(_ffn_kernel, tiles_per_seq=tiles_per_seq, final_norm=final_norm),
        out_shape=jax.ShapeDtypeStruct((n, d), F32),
        grid=(n // tm,),
        in_specs=[row] + _halo_specs(tm, d, n) + [
            _layer_spec(a, l) for a in (g, wup, cw, cb, wdn)] + [_const_spec((1, d))],
        out_specs=row,
        compiler_params=_cparams("parallel"),
        name="convffn",
    )(h, h, h, g, wup, cw, cb, wdn, gf)


def _pad_axis(a, axis, n, before=0):
    pads = [(0, 0)] * a.ndim
    pads[axis] = (before, n - before - a.shape[axis])
    return jnp.pad(a, pads)


def _prep_w_in(w):
    offs = np.concatenate([[0], np.cumsum(IN_SIZES)])
    seg = [w[..., offs[i]:offs[i + 1]] for i in range(len(IN_SIZES))]
    ml = seg[0:4]
    na = [_pad_axis(s, -1, NA_PAD) for s in seg[5:8]]
    gla = [_pad_axis(seg[8], -1, GLA_K_PAD), _pad_axis(seg[9], -1, GLA_K_PAD),
           _pad_axis(seg[10], -1, GLA_V_PAD), _pad_axis(seg[11], -1, GLA_V_PAD)]
    gates = [_pad_axis(seg[4][..., j * ML_HEADS:(j + 1) * ML_HEADS], -1, SUBLANES) for j in range(4)]
    gates = _pad_axis(jnp.concatenate(gates + [seg[12]], axis=-1), -1, GATE_COLS)
    return jnp.concatenate(ml + na + gla + [gates], axis=-1).astype(BF16)


def _prep_gate_bias(gb):
    groups = [_pad_axis(gb[:, j:j + 1, :], -1, SUBLANES) for j in range(4)]
    return _pad_axis(jnp.concatenate(groups, axis=-1), -1, GATE_COLS)


def _prep_gla_gate(w2, ab):
    width = 2 * GLA_K_PAD
    blocks = [_pad_axis(w2[:, z], -1, width, before=z * GLA_K_PAD) for z in range(2)]
    w = _pad_axis(jnp.concatenate(blocks, axis=1), 1, GATE_COLS, before=GLA_GATE_OFF)
    b = jnp.concatenate([_pad_axis(ab[:, z:z + 1], -1, GLA_K_PAD) for z in range(2)], axis=-1)
    return w.astype(BF16), b


def _prep_w_out(w):
    parts = [w[:, 0:ML_DIM], _pad_axis(w[:, ML_DIM:ML_DIM + NA_DIM], 1, NA_PAD),
             _pad_axis(w[:, ML_DIM + NA_DIM:], 1, GLA_V_PAD)]
    return jnp.concatenate(parts, axis=1).astype(BF16)


def kernel(x, mem, mix_norm_g, w_in, ml_conv_w, ml_conv_b, ml_gate_b, ml_norm_g, na_rpb,
           gla_a_w2, gla_a_b, gla_norm_g, w_out, xattn_norm_g, mem_norm_g,
           w_xq, w_xk, w_xv, w_xo, ffn_norm_g, w_up, ffn_conv_w, ffn_conv_b, w_down,
           final_norm_g):
    batch, seq, d = x.shape
    depth = w_in.shape[0]
    rows = seq // GRID_W
    assert seq % WIDE_ROW_TILE == 0 and rows >= NA_KH and rows % NA_UNROLL == 0
    assert seq % (GLA_CHUNK * GLA_STEPS) == 0
    h = x.reshape(batch * seq, d)
    mem2 = mem.reshape(batch * mem.shape[1], d)
    row3 = lambda v: v[:, None, :]
    w_in_p = _prep_w_in(w_in)
    gate_b_p = _prep_gate_bias(ml_gate_b)
    na_bias_p = _na_bias_table(na_rpb, rows)
    gla_w2_p, gla_ab_p = _prep_gla_gate(gla_a_w2, gla_a_b)
    gla_ng_p = row3(_pad_axis(gla_norm_g, -1, GLA_V_PAD))
    w_out_p = _prep_w_out(w_out)
    w_xq_b, w_xk_b, w_xv_b, w_xo_b, w_up_b, w_down_b = (
        w.astype(BF16) for w in (w_xq, w_xk, w_xv, w_xo, w_up, w_down))
    mix_g, ml_cb, ml_ng, x_g, mem_g, ffn_g, ffn_cb = (
        row3(v) for v in (mix_norm_g, ml_conv_b, ml_norm_g, xattn_norm_g, mem_norm_g, ffn_norm_g, ffn_conv_b))
    for l in range(depth):
        ktok, otok, q_t, v_t, gs_t, na, gla, gates = _inproj(
            h, mix_g, w_in_p, ml_conv_w, ml_cb, gate_b_p, l, batch)
        ml_out = _mlstm(ktok, otok, q_t, v_t, gs_t, ml_ng, l, batch)
        na0, na1 = _na(na, na_bias_p, l, batch)
        gla_out = _gla(gla, gates, gla_w2_p, gla_ab_p, gla_ng_p, l, batch)
        k, v = _memkv(mem2, mem_g, w_xk_b, w_xv_b, l)
        h = _xattn((ml_out, na0, na1, gla_out), w_out_p, h, x_g, w_xq_b, k, v, w_xo_b, l, batch)
        h = _ffn(h, ffn_g, w_up_b, ffn_conv_w, ffn_cb, w_down_b, final_norm_g.reshape(1, d), l, batch,
                 final_norm=(l == depth - 1))
    return h.reshape(batch, seq, d)
```
